```python
import jax, jax.numpy as jnp
from jax import lax
import numpy as np

D_MODEL = 1024
BATCH = 8
SEQ = 4096
DEPTH = 2

GRID_W = 64
CTX_LEN = 256
EPS = 1e-6
CHUNK = 128
A_GROUPS = 8
A_WIDTH = D_MODEL // 2
A_GROUP_DIM = A_WIDTH // A_GROUPS
B_GROUPS = 4
B_WIDTH = D_MODEL // 2
B_GROUP_DIM = B_WIDTH // B_GROUPS
AB_IN = 2 * A_WIDTH + B_WIDTH
HEAD_DIM = 64
C_HEADS = 8
C_KV_HEADS = 2
C_GROUP = C_HEADS // C_KV_HEADS
D_HEADS = 8
C_Q_W = C_HEADS * HEAD_DIM
C_KV_W = C_KV_HEADS * HEAD_DIM
D_W = D_HEADS * HEAD_DIM
CD_IN = C_Q_W + 2 * C_KV_W + 3 * D_W
Q_BLOCK = 128
NA_WIN_H = 8
NA_WIN_W = 16
ROPE_THETA = 10000.0
FF_DENSE = 2816
N_EXPERTS = 8
TOP_K = 2
FF_EXPERT = 3584

kernel_name = 'hybrid_flow_backbone_ctx_prefix'


def rmsnorm(x, gain=None):
    xf = x.astype(jnp.float32)
    y = xf * lax.rsqrt(jnp.mean(xf * xf, axis=-1, keepdims=True) + EPS)
    if gain is not None:
        y = y * gain.astype(jnp.float32)
    return y.astype(x.dtype)


def adaln(cond, w, b):
    return jnp.split(jax.nn.silu(cond) @ w + b, 6, axis=-1)


def modulate(x, shift, scale):
    return rmsnorm(x) * (1 + scale) + shift


def axial_rope_tables(length):
    t = jnp.arange(length, dtype=jnp.int32)
    row = (t // GRID_W).astype(jnp.float32)
    col = (t % GRID_W).astype(jnp.float32)
    n_axis = HEAD_DIM // 4
    inv_freq = ROPE_THETA ** (-jnp.arange(n_axis, dtype=jnp.float32) / n_axis)
    ang = jnp.concatenate([row[:, None] * inv_freq, col[:, None] * inv_freq], axis=-1)
    return jnp.cos(ang), jnp.sin(ang)


def apply_rope(x, cos, sin):
    xf = x.astype(jnp.float32).reshape(*x.shape[:-1], HEAD_DIM // 2, 2)
    x0, x1 = xf[..., 0], xf[..., 1]
    cs, sn = cos[None, :, None, :], sin[None, :, None, :]
    out = jnp.stack([x0 * cs - x1 * sn, x0 * sn + x1 * cs], axis=-1)
    return out.reshape(x.shape).astype(x.dtype)


def swiglu(h, wg, wu, wd):
    return (jax.nn.silu(h @ wg) * (h @ wu)) @ wd


def moe_swiglu(h, w_router, wg, wu, wd):
    logits = (h @ w_router).astype(jnp.float32)
    top_val, top_idx = lax.top_k(logits, TOP_K)
    top_w = jax.nn.softmax(top_val, axis=-1)
    gates = jnp.sum(jax.nn.one_hot(top_idx, N_EXPERTS, dtype=jnp.float32) * top_w[..., None], axis=-2).astype(h.dtype)
    out = jnp.zeros_like(h)
    for e in range(N_EXPERTS):
        out = out + gates[..., e:e + 1] * swiglu(h, wg[e], wu[e], wd[e])
    return out


def mix_ab(h, w_in, v_gain, w_sp, b_sp, w_out):
    bsz, length, _ = h.shape
    z = h @ w_in
    u = jax.nn.gelu(z[..., :A_WIDTH])
    v = jax.nn.gelu(z[..., A_WIDTH:2 * A_WIDTH])
    f = z[..., 2 * A_WIDTH:]
    v = rmsnorm(v.reshape(bsz, length, A_GROUPS, A_GROUP_DIM), v_gain.reshape(A_GROUPS, A_GROUP_DIM))
    v = v.reshape(bsz, length // CHUNK, CHUNK, A_GROUPS, A_GROUP_DIM)
    s = jnp.einsum('gpq,bnqgd->bnpgd', w_sp, v) + b_sp.T[None, None, :, :, None]
    a_out = u * s.reshape(bsz, length, A_WIDTH)
    fg = f.reshape(bsz, length, B_GROUPS, B_GROUP_DIM).astype(jnp.float32)
    b_out = jnp.fft.fft2(fg, axes=(1, 3), norm='ortho').real.astype(h.dtype).reshape(bsz, length, B_WIDTH)
    return jnp.concatenate([a_out, b_out], axis=-1) @ w_out


def gqa_attend(q, k, v):
    s = jnp.einsum('bqkgd,bskd->bkgqs', q, k).astype(jnp.float32) * (HEAD_DIM ** -0.5)
    p = jax.nn.softmax(s, axis=-1).astype(v.dtype)
    return jnp.einsum('bkgqs,bskd->bqkgd', p, v)


def blocked_gqa(q, k, v):
    bsz, length = q.shape[:2]
    nb = length // Q_BLOCK
    qb = q.reshape(bsz, nb, Q_BLOCK, *q.shape[2:]).swapaxes(0, 1)
    ob = lax.map(lambda qi: gqa_attend(qi, k, v), qb)
    return ob.swapaxes(0, 1).reshape(q.shape)


def neighbourhood_attend(q, k, v, k_ctx, v_ctx, rpb):
    bsz, length, n_h, dh = q.shape
    rows = length // GRID_W
    wh = min(NA_WIN_H, rows)
    ww = NA_WIN_W
    qg = q.reshape(bsz, rows, GRID_W, n_h, dh)
    kg = k.reshape(bsz, rows, GRID_W, n_h, dh)
    vg = v.reshape(bsz, rows, GRID_W, n_h, dh)
    cols = jnp.arange(GRID_W, dtype=jnp.int32)
    col_start = jnp.clip(cols - ww // 2, 0, GRID_W - ww)
    col_idx = col_start[:, None] + jnp.arange(ww, dtype=jnp.int32)[None, :]
    dc = col_idx - cols[:, None] + (NA_WIN_W - 1)
    scale = dh ** -0.5

    def one_row(r):
        r0 = jnp.clip(r - wh // 2, 0, rows - wh)
        k_nb = lax.dynamic_slice_in_dim(kg, r0, wh, axis=1)[:, :, col_idx]
        v_nb = lax.dynamic_slice_in_dim(vg, r0, wh, axis=1)[:, :, col_idx]
        q_r = lax.dynamic_index_in_dim(qg, r, axis=1, keepdims=False)
        dr = r0 + jnp.arange(wh, dtype=jnp.int32) - r + (NA_WIN_H - 1)
        bias = rpb[:, dr[:, None, None], dc[None, :, :]].transpose(0, 2, 1, 3)
        s_nb = jnp.einsum('bjhd,bajkhd->bhjak', q_r, k_nb).astype(jnp.float32) * scale + bias[None].astype(jnp.float32)
        s_nb = s_nb.reshape(bsz, n_h, GRID_W, wh * ww)
        s_ctx = jnp.einsum('bjhd,bchd->bhjc', q_r, k_ctx).astype(jnp.float32) * scale
        p = jax.nn.softmax(jnp.concatenate([s_nb, s_ctx], axis=-1), axis=-1).astype(v.dtype)
        p_nb = p[..., :wh * ww].reshape(bsz, n_h, GRID_W, wh, ww)
        p_ctx = p[..., wh * ww:]
        return jnp.einsum('bhjak,bajkhd->bjhd', p_nb, v_nb) + jnp.einsum('bhjc,bchd->bjhd', p_ctx, v_ctx)

    out = lax.map(one_row, jnp.arange(rows, dtype=jnp.int32))
    return out.transpose(1, 0, 2, 3, 4).reshape(bsz, length, n_h * dh)


def project_cd(h, w_in, q_gain, k_gain):
    bsz, length = h.shape[:2]
    z = h @ w_in
    o1 = C_Q_W
    o2 = o1 + C_KV_W
    o3 = o2 + C_KV_W
    o4 = o3 + D_W
    o5 = o4 + D_W
    cq = rmsnorm(z[..., :o1].reshape(bsz, length, C_HEADS, HEAD_DIM), q_gain)
    ck = rmsnorm(z[..., o1:o2].reshape(bsz, length, C_KV_HEADS, HEAD_DIM), k_gain)
    cv = z[..., o2:o3].reshape(bsz, length, C_KV_HEADS, HEAD_DIM)
    dq = z[..., o3:o4].reshape(bsz, length, D_HEADS, HEAD_DIM)
    dk = z[..., o4:o5].reshape(bsz, length, D_HEADS, HEAD_DIM)
    dv = z[..., o5:].reshape(bsz, length, D_HEADS, HEAD_DIM)
    return cq, ck, cv, dq, dk, dv


def mix_cd(n, nc, w_in, q_gain, k_gain, rpb, w_out, cos, sin, with_ctx_out):
    bsz, length = n.shape[:2]
    cq, ck, cv, dq, dk, dv = project_cd(n, w_in, q_gain, k_gain)
    cq_c, ck_c, cv_c, dq_c, dk_c, dv_c = project_cd(nc, w_in, q_gain, k_gain)
    cq = apply_rope(cq, cos, sin)
    ck = apply_rope(ck, cos, sin)
    k_all = jnp.concatenate([ck_c, ck], axis=1)
    v_all = jnp.concatenate([cv_c, cv], axis=1)
    o_c = blocked_gqa(cq.reshape(bsz, length, C_KV_HEADS, C_GROUP, HEAD_DIM), k_all, v_all).reshape(bsz, length, C_Q_W)
    o_d = neighbourhood_attend(dq, dk, dv, dk_c, dv_c, rpb)
    y = jnp.concatenate([o_c, o_d], axis=-1) @ w_out
    if not with_ctx_out:
        return y, None
    cl = nc.shape[1]
    oc_c = gqa_attend(cq_c.reshape(bsz, cl, C_KV_HEADS, C_GROUP, HEAD_DIM), ck_c, cv_c).reshape(bsz, cl, C_Q_W)
    od_c = gqa_attend(dq_c[:, :, :, None, :], dk_c, dv_c).reshape(bsz, cl, D_W)
    yc = jnp.concatenate([oc_c, od_c], axis=-1) @ w_out
    return y, yc


def setup_inputs(seed: int = 0) -> dict:
    key = jax.random.key(seed)
    ks = iter(jax.random.split(key, 32))
    n_even = (DEPTH + 1) // 2
    n_odd = DEPTH // 2
    d = D_MODEL

    def nrm(shape, scale):
        return jax.random.normal(next(ks), shape, jnp.float32) * scale

    return {
        'x': nrm((BATCH, SEQ, d), 1.0),
        'c': nrm((BATCH, d), 1.0),
        'ctx': nrm((BATCH, CTX_LEN, d), 1.0),
        'c_ctx': nrm((d,), 1.0),
        'w_ada': nrm((DEPTH, d, 6 * d), 0.5 * d ** -0.5),
        'b_ada': nrm((DEPTH, 6 * d), 0.02),
        'w_in_ab': nrm((n_even, d, AB_IN), d ** -0.5),
        'v_gain': 1.0 + nrm((n_even, A_WIDTH), 0.02),
        'w_spatial': nrm((n_even, A_GROUPS, CHUNK, CHUNK), CHUNK ** -0.5),
        'b_spatial': 1.0 + nrm((n_even, A_GROUPS, CHUNK), 0.02),
        'w_out_ab': nrm((n_even, A_WIDTH + B_WIDTH, d), (A_WIDTH + B_WIDTH) ** -0.5),
        'w_gate_dense': nrm((n_even, d, FF_DENSE), d ** -0.5),
        'w_up_dense': nrm((n_even, d, FF_DENSE), d ** -0.5),
        'w_down_dense': nrm((n_even, FF_DENSE, d), FF_DENSE ** -0.5),
        'w_in_cd': nrm((n_odd, d, CD_IN), d ** -0.5),
        'q_gain': 1.0 + nrm((n_odd, HEAD_DIM), 0.02),
        'k_gain': 1.0 + nrm((n_odd, HEAD_DIM), 0.02),
        'rel_bias': nrm((n_odd, D_HEADS, 2 * NA_WIN_H - 1, 2 * NA_WIN_W - 1), 0.1),
        'w_out_cd': nrm((n_odd, C_Q_W + D_W, d), (C_Q_W + D_W) ** -0.5),
        'w_router': nrm((n_odd, d, N_EXPERTS), d ** -0.5),
        'w_gate_moe': nrm((n_odd, N_EXPERTS, d, FF_EXPERT), d ** -0.5),
        'w_up_moe': nrm((n_odd, N_EXPERTS, d, FF_EXPERT), d ** -0.5),
        'w_down_moe': nrm((n_odd, N_EXPERTS, FF_EXPERT, d), FF_EXPERT ** -0.5),
        'final_gain': 1.0 + nrm((d,), 0.02),
    }


def reference(x, c, ctx, c_ctx, w_ada, b_ada, w_in_ab, v_gain, w_spatial, b_spatial, w_out_ab,
              w_gate_dense, w_up_dense, w_down_dense, w_in_cd, q_gain, k_gain, rel_bias, w_out_cd,
              w_router, w_gate_moe, w_up_moe, w_down_moe, final_gain):
    cos, sin = axial_rope_tables(x.shape[1])
    h, hc = x, ctx
    for i in range(DEPTH):
        last = i == DEPTH - 1
        odd = i % 2 == 1
        j = i // 2
        sh1, sc1, g1, sh2, sc2, g2 = [t[:, None, :] for t in adaln(c, w_ada[i], b_ada[i])]
        if odd or not last:
            csh1, csc1, cg1, csh2, csc2, cg2 = adaln(c_ctx, w_ada[i], b_ada[i])
        n = modulate(h, sh1, sc1)
        if not odd:
            h = h + g1 * mix_ab(n, w_in_ab[j], v_gain[j], w_spatial[j], b_spatial[j], w_out_ab[j])
            if not last:
                hc = hc + cg1 * mix_ab(modulate(hc, csh1, csc1), w_in_ab[j], v_gain[j], w_spatial[j], b_spatial[j], w_out_ab[j])
        else:
            y, yc = mix_cd(n, modulate(hc, csh1, csc1), w_in_cd[j], q_gain[j], k_gain[j], rel_bias[j], w_out_cd[j], cos, sin, not last)
            h = h + g1 * y
            if not last:
                hc = hc + cg1 * yc

        def channel_mixer(t):
            if odd:
                return moe_swiglu(t, w_router[j], w_gate_moe[j], w_up_moe[j], w_down_moe[j])
            return swiglu(t, w_gate_dense[j], w_up_dense[j], w_down_dense[j])

        h = h + g2 * channel_mixer(modulate(h, sh2, sc2))
        if not last:
            hc = hc + cg2 * channel_mixer(modulate(hc, csh2, csc2))
    return rmsnorm(h, final_gain)
```

```python
import functools
import math

import numpy as np
import jax
import jax.numpy as jnp
from jax import lax
from jax.experimental import pallas as pl
from jax.experimental.pallas import tpu as pltpu

F32 = jnp.float32
BF16 = jnp.bfloat16

GRID_W = 64
EPS = 1e-6
CHUNK = 128
A_GROUPS = 8
B_GROUPS = 4
HEAD_DIM = 64
C_HEADS = 8
C_KV_HEADS = 2
C_GROUP = C_HEADS // C_KV_HEADS
D_HEADS = 8
NA_WIN_H = 8
NA_WIN_W = 16
ROPE_THETA = 10000.0
N_EXPERTS = 8
TOP_K = 2

LANES = 128
NA_QROWS = 4
NA_KROWS = NA_QROWS + NA_WIN_H
MASK_VALUE = -1e30
VMEM_LIMIT = 56 * 1024 * 1024


def _cparams(*sem):
    return pltpu.CompilerParams(dimension_semantics=sem, vmem_limit_bytes=VMEM_LIMIT)


def _modulate(h, shift, scale):
    ms = jnp.mean(h * h, axis=-1, keepdims=True)
    return h * lax.rsqrt(ms + EPS) * (1.0 + scale) + shift


def _dot(a, b):
    return jnp.dot(a, b, preferred_element_type=F32)


def _dot_nt(a, b):
    return lax.dot_general(a, b, (((1,), (1,)), ((), ())), preferred_element_type=F32)


def _row_tile(length, target):
    t = min(length, target)
    assert length % t == 0
    return t


def _adaln_kernel(c_ref, w_ref, b_ref, o_ref):
    s = jax.nn.silu(c_ref[...]).astype(BF16)
    o_ref[...] = _dot(s, w_ref[...].astype(BF16)) + b_ref[...]


def _adaln(cond, w_ada, b_ada):
    depth, d, n = w_ada.shape
    r = cond.shape[0]
    tn = 512
    return pl.pallas_call(
        _adaln_kernel,
        grid=(depth, n // tn),
        in_specs=[
            pl.BlockSpec((r, d), lambda i, j: (0, 0)),
            pl.BlockSpec((None, d, tn), lambda i, j: (i, 0, j)),
            pl.BlockSpec((None, 1, tn), lambda i, j: (i, 0, j)),
        ],
        out_specs=pl.BlockSpec((None, r, tn), lambda i, j: (i, 0, j)),
        out_shape=jax.ShapeDtypeStruct((depth, r, n), F32),
        compiler_params=_cparams("parallel", "parallel"),
    )(cond, w_ada, b_ada.reshape(depth, 1, n))


def _ab_in_kernel(h_ref, mod_ref, w_ref, vg_ref, gm_ref, wsp_ref, bsp_ref, cm_ref, sm_ref,
                  a_ref, fc_ref, fs_ref, *, aw):
    tm = h_ref.shape[0]
    n = _modulate(h_ref[...], mod_ref[0:1, :], mod_ref[1:2, :])
    z = _dot(n.astype(BF16), w_ref[...])
    u = jax.nn.gelu(z[:, :aw])
    v = jax.nn.gelu(z[:, aw:2 * aw])
    fb = z[:, 2 * aw:].astype(BF16)
    ms = _dot((v * v).astype(BF16), gm_ref[...])
    vn = v * lax.rsqrt(ms + EPS) * vg_ref[...]
    group_dim = aw // A_GROUPS
    lane_group = lax.broadcasted_iota(jnp.int32, (CHUNK, aw), 1) // group_dim
    for c in range(tm // CHUNK):
        rows = slice(c * CHUNK, (c + 1) * CHUNK)
        vc = vn[rows, :]
        stack = jnp.concatenate(
            [jnp.where(lane_group == g, vc, 0.0).astype(BF16) for g in range(A_GROUPS)], axis=0)
        s = _dot(wsp_ref[...], stack) + bsp_ref[...]
        a_ref[rows, :] = (u[rows, :] * s).astype(BF16)
    bw = fb.shape[1] // B_GROUPS
    for g in range(B_GROUPS):
        cols = slice(g * bw, (g + 1) * bw)
        fc_ref[:, cols] = _dot(fb[:, cols], cm_ref[...]).astype(BF16)
        fs_ref[:, cols] = _dot(fb[:, cols], sm_ref[...]).astype(BF16)


def _seq_dft_kernel(c_ref, s_ref, fc_ref, fs_ref, o_ref):
    o_ref[...] = (_dot(c_ref[...], fc_ref[...]) - _dot(s_ref[...], fs_ref[...])).astype(BF16)


def _proj_residual_kernel(a_ref, b_ref, w_ref, h_ref, mod_ref, o_ref, *, gate_row):
    ka = a_ref.shape[1]
    y = _dot(a_ref[...], w_ref[:ka, :]) + _dot(b_ref[...], w_ref[ka:, :])
    o_ref[...] = h_ref[...] + mod_ref[gate_row:gate_row + 1, :] * y


def _proj_residual(a, b, w, h, mod, gate_row):
    bsz, length, d = h.shape
    tm = _row_tile(length, 1024)
    ka, kb = a.shape[2], b.shape[2]
    return pl.pallas_call(
        functools.partial(_proj_residual_kernel, gate_row=gate_row),
        grid=(bsz, length // tm),
        in_specs=[
            pl.BlockSpec((None, tm, ka), lambda i, j: (i, j, 0)),
            pl.BlockSpec((None, tm, kb), lambda i, j: (i, j, 0)),
            pl.BlockSpec((ka + kb, d), lambda i, j: (0, 0)),
            pl.BlockSpec((None, tm, d), lambda i, j: (i, j, 0)),
            pl.BlockSpec((None, 8, d), lambda i, j: (i, 0, 0)),
        ],
        out_specs=pl.BlockSpec((None, tm, d), lambda i, j: (i, j, 0)),
        out_shape=jax.ShapeDtypeStruct((bsz, length, d), F32),
        compiler_params=_cparams("parallel", "parallel"),
    )(a, b, w, h, mod)


def _dft_tables(n, scale):
    k = jnp.arange(n, dtype=jnp.int32)
    ang = ((k[:, None] * k[None, :]) % n).astype(F32) * (2.0 * math.pi / n)
    return (jnp.cos(ang) * scale).astype(BF16), (jnp.sin(ang) * scale).astype(BF16)


def _mix_ab(h, mod, p):
    bsz, length, d = h.shape
    aw = p["v_gain"].shape[1]
    bw_total = p["w_in"].shape[1] - 2 * aw
    tm = _row_tile(length, 512)
    const = lambda i, j: (0, 0)
    a_out, fc, fs = pl.pallas_call(
        functools.partial(_ab_in_kernel, aw=aw),
        grid=(bsz, length // tm),
        in_specs=[
            pl.BlockSpec((None, tm, d), lambda i, j: (i, j, 0)),
            pl.BlockSpec((None, 8, d), lambda i, j: (i, 0, 0)),
            pl.BlockSpec(p["w_in"].shape, const),
            pl.BlockSpec(p["v_gain"].shape, const),
            pl.BlockSpec(p["gmean"].shape, const),
            pl.BlockSpec(p["w_sp"].shape, const),
            pl.BlockSpec(p["b_sp"].shape, const),
            pl.BlockSpec(p["cmat"].shape, const),
            pl.BlockSpec(p["smat"].shape, const),
        ],
        out_specs=[
            pl.BlockSpec((None, tm, aw), lambda i, j: (i, j, 0)),
            pl.BlockSpec((None, tm, bw_total), lambda i, j: (i, j, 0)),
            pl.BlockSpec((None, tm, bw_total), lambda i, j: (i, j, 0)),
        ],
        out_shape=[
            jax.ShapeDtypeStruct((bsz, length, aw), BF16),
            jax.ShapeDtypeStruct((bsz, length, bw_total), BF16),
            jax.ShapeDtypeStruct((bsz, length, bw_total), BF16),
        ],
        compiler_params=_cparams("parallel", "parallel"),
    )(h, mod, p["w_in"], p["v_gain"], p["gmean"], p["w_sp"], p["b_sp"], p["cmat"], p["smat"])

    cl, sl = _dft_tables(length, length ** -0.5)
    tk = _row_tile(length, 512)
    b_out = pl.pallas_call(
        _seq_dft_kernel,
        grid=(length // tk, bsz),
        in_specs=[
            pl.BlockSpec((tk, length), lambda k, b: (k, 0)),
            pl.BlockSpec((tk, length), lambda k, b: (k, 0)),
            pl.BlockSpec((None, length, bw_total), lambda k, b: (b, 0, 0)),
            pl.BlockSpec((None, length, bw_total), lambda k, b: (b, 0, 0)),
        ],
        out_specs=pl.BlockSpec((None, tk, bw_total), lambda k, b: (b, k, 0)),
        out_shape=jax.ShapeDtypeStruct((bsz, length, bw_total), BF16),
        compiler_params=_cparams("parallel", "parallel"),
    )(cl, sl, fc, fs)
    return _proj_residual(a_out, b_out, p["w_out"], h, mod, 2)


def _dense_ffn_kernel(h_ref, mod_ref, wg_ref, wu_ref, wd_ref, o_ref, n_sc, acc_sc):
    f = pl.program_id(2)

    @pl.when(f == 0)
    def _():
        n_sc[...] = _modulate(h_ref[...], mod_ref[3:4, :], mod_ref[4:5, :]).astype(BF16)
        acc_sc[...] = jnp.zeros_like(acc_sc)

    n = n_sc[...]
    a = jax.nn.silu(_dot(n, wg_ref[...])) * _dot(n, wu_ref[...])
    acc_sc[...] += _dot(a.astype(BF16), wd_ref[...])

    @pl.when(f == pl.num_programs(2) - 1)
    def _():
        o_ref[...] = h_ref[...] + mod_ref[5:6, :] * acc_sc[...]


def _ff_tile(ff, target):
    best = None
    for t in range(LANES, min(ff, target) + 1, LANES):
        if ff % t == 0:
            best = t
    assert best is not None
    return best


def _dense_ffn(h, mod, wg, wu, wd):
    bsz, length, d = h.shape
    ff = wg.shape[1]
    tm = _row_tile(length, 1024)
    tf = _ff_tile(ff, 1408)
    return pl.pallas_call(
        _dense_ffn_kernel,
        grid=(bsz, length // tm, ff // tf),
        in_specs=[
            pl.BlockSpec((None, tm, d), lambda i, j, f: (i, j, 0)),
            pl.BlockSpec((None, 8, d), lambda i, j, f: (i, 0, 0)),
            pl.BlockSpec((d, tf), lambda i, j, f: (0, f)),
            pl.BlockSpec((d, tf), lambda i, j, f: (0, f)),
            pl.BlockSpec((tf, d), lambda i, j, f: (f, 0)),
        ],
        out_specs=pl.BlockSpec((None, tm, d), lambda i, j, f: (i, j, 0)),
        out_shape=jax.ShapeDtypeStruct((bsz, length, d), F32),
        scratch_shapes=[pltpu.VMEM((tm, d), BF16), pltpu.VMEM((tm, d), F32)],
        compiler_params=_cparams("parallel", "parallel", "arbitrary"),
    )(h, mod, wg, wu, wd)


def _pair_swap(x):
    width = x.shape[-1]
    lane = lax.broadcasted_iota(jnp.int32, x.shape, x.ndim - 1)
    nxt = pltpu.roll(x, width - 1, x.ndim - 1)
    prv = pltpu.roll(x, 1, x.ndim - 1)
    return jnp.where(lane % 2 == 0, nxt, prv)


def _rope(x, cos, sin):
    reps = x.shape[1] // LANES
    cos = jnp.concatenate([cos] * reps, axis=1) if reps > 1 else cos
    sin = jnp.concatenate([sin] * reps, axis=1) if reps > 1 else sin
    return x * cos + _pair_swap(x) * sin


def _cd_in_kernel(h_ref, mod_ref, w_ref, gm_ref, qg_ref, kg_ref, cos_ref, sin_ref,
                  cq_ref, ck_ref, cv_ref, dq_ref, dk_ref, dv_ref, *, rope):
    n = _modulate(h_ref[...], mod_ref[0:1, :], mod_ref[1:2, :])
    z = _dot(n.astype(BF16), w_ref[...])
    qw, kw, dw = cq_ref.shape[1], ck_ref.shape[1], dq_ref.shape[1]
    o1, o2, o3, o4, o5 = qw, qw + kw, qw + 2 * kw, qw + 2 * kw + dw, qw + 2 * kw + 2 * dw
    scale = HEAD_DIM ** -0.5

    def head_norm(t, gain):
        width = t.shape[1]
        ms = _dot((t * t).astype(BF16), gm_ref[:width, :width])
        return t * lax.rsqrt(ms + EPS) * gain

    cq = head_norm(z[:, :o1], qg_ref[...])
    ck = head_norm(z[:, o1:o2], kg_ref[...])
    if rope:
        cq = _rope(cq, cos_ref[...], sin_ref[...])
        ck = _rope(ck, cos_ref[...], sin_ref[...])
    cq_ref[...] = (cq * scale).astype(BF16)
    ck_ref[...] = ck.astype(BF16)
    cv_ref[...] = z[:, o2:o3].astype(BF16)
    dq_ref[...] = (z[:, o3:o4] * scale).astype(BF16)
    dk_ref[...] = z[:, o4:o5].astype(BF16)
    dv_ref[...] = z[:, o5:].astype(BF16)


def _project_cd(h, mod, p, rope):
    bsz, length, d = h.shape
    qw, kw, dw = C_HEADS * HEAD_DIM, C_KV_HEADS * HEAD_DIM, D_HEADS * HEAD_DIM
    tm = _row_tile(length, 512)
    const = lambda i, j: (0, 0)
    tok = lambda w: pl.BlockSpec((None, tm, w), lambda i, j: (i, j, 0))
    widths = (qw, kw, kw, dw, dw, dw)
    return pl.pallas_call(
        functools.partial(_cd_in_kernel, rope=rope),
        grid=(bsz, length // tm),
        in_specs=[
            tok(d),
            pl.BlockSpec((None, 8, d), lambda i, j: (i, 0, 0)),
            pl.BlockSpec(p["w_in"].shape, const),
            pl.BlockSpec(p["gmean"].shape, const),
            pl.BlockSpec(p["q_gain"].shape, const),
            pl.BlockSpec(p["k_gain"].shape, const),
            pl.BlockSpec((tm, LANES), lambda i, j: (j, 0)),
            pl.BlockSpec((tm, LANES), lambda i, j: (j, 0)),
        ],
        out_specs=[tok(w) for w in widths],
        out_shape=[jax.ShapeDtypeStruct((bsz, length, w), BF16) for w in widths],
        compiler_params=_cparams("parallel", "parallel"),
    )(h, mod, p["w_in"], p["gmean"], p["q_gain"], p["k_gain"], p["cos"], p["sin"])


def _gqa_kernel(q_ref, k_ref, v_ref, o_ref, m_sc, l_sc, acc_sc, *, tk):
    tq = q_ref.shape[0]
    n_chunks = k_ref.shape[0] // tk
    for kvh in range(C_KV_HEADS):
        heads = [kvh * C_GROUP + g for g in range(C_GROUP)]
        q4 = jnp.concatenate([q_ref[:, h * HEAD_DIM:(h + 1) * HEAD_DIM] for h in heads], axis=0)
        cols = slice(kvh * HEAD_DIM, (kvh + 1) * HEAD_DIM)
        m_sc[...] = jnp.full_like(m_sc, -jnp.inf)
        l_sc[...] = jnp.zeros_like(l_sc)
        acc_sc[...] = jnp.zeros_like(acc_sc)

        def body(j, carry):
            start = pl.multiple_of(j * tk, tk)
            k = k_ref[pl.ds(start, tk), cols]
            v = v_ref[pl.ds(start, tk), cols]
            s = _dot_nt(q4, k)
            m_prev = m_sc[...]
            m_new = jnp.maximum(m_prev, jnp.max(s, axis=-1, keepdims=True))
            alpha = jnp.exp(m_prev - m_new)
            p = jnp.exp(s - m_new)
            l_sc[...] = alpha * l_sc[...] + jnp.sum(p, axis=-1, keepdims=True)
            acc_sc[...] = alpha * acc_sc[...] + _dot(p.astype(BF16), v)
            m_sc[...] = m_new
            return carry

        lax.fori_loop(0, n_chunks, body, 0)
        o = acc_sc[...] / l_sc[...]
        for g, h in enumerate(heads):
            o_ref[:, h * HEAD_DIM:(h + 1) * HEAD_DIM] = o[g * tq:(g + 1) * tq, :].astype(BF16)


def _gqa(q, k_all, v_all):
    bsz, length, qw = q.shape
    lk, kw = k_all.shape[1], k_all.shape[2]
    tq = _row_tile(length, 256)
    tk = 256
    assert lk % tk == 0
    rows = C_GROUP * tq
    return pl.pallas_call(
        functools.partial(_gqa_kernel, tk=tk),
        grid=(bsz, length // tq),
        in_specs=[
            pl.BlockSpec((None, tq, qw), lambda i, j: (i, j, 0)),
            pl.BlockSpec((None, lk, kw), lambda i, j: (i, 0, 0)),
            pl.BlockSpec((None, lk, kw), lambda i, j: (i, 0, 0)),
        ],
        out_specs=pl.BlockSpec((None, tq, qw), lambda i, j: (i, j, 0)),
        out_shape=jax.ShapeDtypeStruct((bsz, length, qw), BF16),
        scratch_shapes=[pltpu.VMEM((rows, 1), F32), pltpu.VMEM((rows, 1), F32),
                        pltpu.VMEM((rows, HEAD_DIM), F32)],
        compiler_params=_cparams("parallel", "parallel"),
    )(q, k_all, v_all)


def _na_block_start(qb, rows):
    return jnp.clip(qb * NA_QROWS - NA_WIN_H // 2, 0, rows - NA_KROWS)


def _na_bias_tables(rel_bias, rows):
    nblk = rows // NA_QROWS
    i = np.arange(NA_QROWS)[:, None, None, None]
    j = np.arange(GRID_W)[None, :, None, None]
    a = np.arange(NA_KROWS)[None, None, :, None]
    kc = np.arange(GRID_W)[None, None, None, :]
    col_start = np.clip(j - NA_WIN_W // 2, 0, GRID_W - NA_WIN_W)
    valid_col = (kc >= col_start) & (kc < col_start + NA_WIN_W)
    dc = kc - j + (NA_WIN_W - 1)
    tables = []
    for qb in (0, 1, nblk - 1):
        r = qb * NA_QROWS + i
        r0 = np.clip(r - NA_WIN_H // 2, 0, rows - NA_WIN_H)
        kr = int(np.clip(qb * NA_QROWS - NA_WIN_H // 2, 0, rows - NA_KROWS)) + a
        valid = (kr >= r0) & (kr < r0 + NA_WIN_H) & valid_col
        dr = kr - r + (NA_WIN_H - 1)
        shape = (NA_QROWS, GRID_W, NA_KROWS, GRID_W)
        valid = np.broadcast_to(valid, shape).reshape(NA_QROWS * GRID_W, NA_KROWS * GRID_W)
        dr = np.clip(np.broadcast_to(dr, shape), 0, 2 * NA_WIN_H - 2).reshape(valid.shape)
        dcb = np.clip(np.broadcast_to(dc, shape), 0, 2 * NA_WIN_W - 2).reshape(valid.shape)
        tables.append(jnp.where(valid[None], rel_bias[:, dr, dcb].astype(F32), MASK_VALUE))
    return jnp.concatenate(tables, axis=0)


def _na_kernel(q_ref, k_ref, v_ref, kc_ref, vc_ref, bias_ref, o_ref, *, rows):
    qb = pl.program_id(1)
    nk = NA_KROWS * GRID_W
    start = pl.multiple_of(_na_block_start(qb, rows) * GRID_W, GRID_W)
    k_nb = k_ref[pl.ds(start, nk), :]
    v_nb = v_ref[pl.ds(start, nk), :]
    for h in range(D_HEADS):
        cols = slice(h * HEAD_DIM, (h + 1) * HEAD_DIM)
        q = q_ref[:, cols]
        s_nb = _dot_nt(q, k_nb[:, cols]) + bias_ref[h]
        s_c = _dot_nt(q, kc_ref[:, cols])
        m = jnp.maximum(jnp.max(s_nb, axis=-1, keepdims=True), jnp.max(s_c, axis=-1, keepdims=True))
        p_nb = jnp.exp(s_nb - m)
        p_c = jnp.exp(s_c - m)
        l = jnp.sum(p_nb, axis=-1, keepdims=True) + jnp.sum(p_c, axis=-1, keepdims=True)
        o = _dot(p_nb.astype(BF16), v_nb[:, cols]) + _dot(p_c.astype(BF16), vc_ref[:, cols])
        o_ref[:, cols] = (o / l).astype(BF16)


def _neighbourhood(dq, dk, dv, dk_c, dv_c, bias):
    bsz, length, w = dq.shape
    rows = length // GRID_W
    assert rows % NA_QROWS == 0 and rows >= NA_KROWS
    nblk = rows // NA_QROWS
    tq = NA_QROWS * GRID_W
    cl = dk_c.shape[1]

    def bias_class(i, j):
        cls = jnp.where(j == 0, 0, jnp.where(j == nblk - 1, 2, 1))
        return (cls, 0, 0)

    return pl.pallas_call(
        functools.partial(_na_kernel, rows=rows),
        grid=(bsz, nblk),
        in_specs=[
            pl.BlockSpec((None, tq, w), lambda i, j: (i, j, 0)),
            pl.BlockSpec((None, length, w), lambda i, j: (i, 0, 0)),
            pl.BlockSpec((None, length, w), lambda i, j: (i, 0, 0)),
            pl.BlockSpec((None, cl, w), lambda i, j: (i, 0, 0)),
            pl.BlockSpec((None, cl, w), lambda i, j: (i, 0, 0)),
            pl.BlockSpec((D_HEADS, tq, NA_KROWS * GRID_W), bias_class),
        ],
        out_specs=pl.BlockSpec((None, tq, w), lambda i, j: (i, j, 0)),
        out_shape=jax.ShapeDtypeStruct((bsz, length, w), BF16),
        compiler_params=_cparams("parallel", "arbitrary"),
    )(dq, dk, dv, dk_c, dv_c, bias)


def _router_kernel(h_ref, mod_ref, wr_ref, n_ref, g_ref):
    n = _modulate(h_ref[...], mod_ref[3:4, :], mod_ref[4:5, :])
    n_hi = n.astype(BF16)
    n_lo = (n - n_hi.astype(F32)).astype(BF16)
    w = wr_ref[...]
    w_hi = w.astype(BF16)
    w_lo = (w - w_hi.astype(F32)).astype(BF16)
    logits = _dot(n_hi, w_hi) + (_dot(n_lo, w_hi) + _dot(n_hi, w_lo))
    lane = lax.broadcasted_iota(jnp.int32, logits.shape, 1)
    logits = jnp.where(lane < N_EXPERTS, logits, -jnp.inf)
    m1 = jnp.max(logits, axis=-1, keepdims=True)
    i1 = jnp.min(jnp.where(logits == m1, lane, LANES), axis=-1, keepdims=True)
    rest = jnp.where(lane == i1, -jnp.inf, logits)
    m2 = jnp.max(rest, axis=-1, keepdims=True)
    i2 = jnp.min(jnp.where(rest == m2, lane, LANES), axis=-1, keepdims=True)
    e2 = jnp.exp(m2 - m1)
    w1 = 1.0 / (1.0 + e2)
    w2 = e2 / (1.0 + e2)
    g_ref[...] = jnp.where(lane == i1, w1, 0.0) + jnp.where(lane == i2, w2, 0.0)
    n_ref[...] = n_hi


def _router(h, mod, w_router_padded):
    bsz, length, d = h.shape
    tm = _row_tile(length, 1024)
    return pl.pallas_call(
        _router_kernel,
        grid=(bsz, length // tm),
        in_specs=[
            pl.BlockSpec((None, tm, d), lambda i, j: (i, j, 0)),
            pl.BlockSpec((None, 8, d), lambda i, j: (i, 0, 0)),
            pl.BlockSpec((d, LANES), lambda i, j: (0, 0)),
        ],
        out_specs=[
            pl.BlockSpec((None, tm, d), lambda i, j: (i, j, 0)),
            pl.BlockSpec((None, tm, LANES), lambda i, j: (i, j, 0)),
        ],
        out_shape=[
            jax.ShapeDtypeStruct((bsz, length, d), BF16),
            jax.ShapeDtypeStruct((bsz, length, LANES), F32),
        ],
        compiler_params=_cparams("parallel", "parallel"),
    )(h, mod, w_router_padded)


def _moe_dense_kernel(n_ref, g_ref, h_ref, mod_ref, fg_ref, wg_ref, wu_ref, wd_ref, o_ref, acc_sc):
    e = pl.program_id(2)
    f = pl.program_id(3)

    @pl.when((e == 0) & (f == 0))
    def _():
        acc_sc[...] = jnp.zeros_like(acc_sc)

    n = n_ref[...]
    lane = lax.broadcasted_iota(jnp.int32, g_ref.shape, 1)
    gate = jnp.sum(jnp.where(lane == e, g_ref[...], 0.0), axis=-1, keepdims=True)
    a = jax.nn.silu(_dot(n, wg_ref[...])) * _dot(n, wu_ref[...])
    acc_sc[...] += gate * _dot(a.astype(BF16), wd_ref[...])

    @pl.when((e == pl.num_programs(2) - 1) & (f == pl.num_programs(3) - 1))
    def _():
        y = h_ref[...] + mod_ref[5:6, :] * acc_sc[...]
        ms = jnp.mean(y * y, axis=-1, keepdims=True)
        o_ref[...] = y * lax.rsqrt(ms + EPS) * fg_ref[...]


def _moe_dense_final(n2, gates, h, mod, final_gain, wg, wu, wd):
    bsz, length, d = h.shape
    n_exp, _, ff = wg.shape
    tm = _row_tile(length, 1024)
    tf = _ff_tile(ff, 512)
    tok = lambda w: pl.BlockSpec((None, tm, w), lambda i, j, e, f: (i, j, 0))
    return pl.pallas_call(
        _moe_dense_kernel,
        grid=(bsz, length // tm, n_exp, ff // tf),
        in_specs=[
            tok(d), tok(LANES), tok(d),
            pl.BlockSpec((None, 8, d), lambda i, j, e, f: (i, 0, 0)),
            pl.BlockSpec((1, d), lambda i, j, e, f: (0, 0)),
            pl.BlockSpec((None, d, tf), lambda i, j, e, f: (e, 0, f)),
            pl.BlockSpec((None, d, tf), lambda i, j, e, f: (e, 0, f)),
            pl.BlockSpec((None, tf, d), lambda i, j, e, f: (e, f, 0)),
        ],
        out_specs=tok(d),
        out_shape=jax.ShapeDtypeStruct((bsz, length, d), F32),
        scratch_shapes=[pltpu.VMEM((tm, d), F32)],
        compiler_params=_cparams("parallel", "parallel", "arbitrary", "arbitrary"),
    )(n2, gates, h, mod, final_gain, wg, wu, wd)


def _group_mean_matrix(width, group):
    return jnp.asarray(np.kron(np.eye(width // group), np.full((group, group), 1.0 / group)), BF16)


def _rope_tables(length):
    t = jnp.arange(length, dtype=jnp.int32)
    row = (t // GRID_W).astype(F32)
    col = (t % GRID_W).astype(F32)
    n_axis = HEAD_DIM // 4
    inv_freq = ROPE_THETA ** (-jnp.arange(n_axis, dtype=F32) / n_axis)
    ang = jnp.concatenate([row[:, None] * inv_freq, col[:, None] * inv_freq], axis=-1)
    cos = jnp.repeat(jnp.cos(ang), 2, axis=-1)
    sin = jnp.repeat(jnp.sin(ang), 2, axis=-1)
    sign = jnp.tile(jnp.asarray([-1.0, 1.0], F32), HEAD_DIM // 2)
    reps = LANES // HEAD_DIM
    return jnp.tile(cos, (1, reps)), jnp.tile(sin * sign, (1, reps))


def kernel(x, c, ctx, c_ctx, w_ada, b_ada, w_in_ab, v_gain, w_spatial, b_spatial, w_out_ab,
           w_gate_dense, w_up_dense, w_down_dense, w_in_cd, q_gain, k_gain, rel_bias, w_out_cd,
           w_router, w_gate_moe, w_up_moe, w_down_moe, final_gain):
    bsz, length, d = x.shape
    depth = w_ada.shape[0]
    assert depth == 2 and bsz <= 8
    aw = v_gain.shape[1]
    group_dim = aw // A_GROUPS
    bgd = (w_in_ab.shape[2] - 2 * aw) // B_GROUPS

    cond = jnp.zeros((16, d), F32).at[:bsz].set(c).at[8].set(c_ctx)
    ada = _adaln(cond, w_ada, b_ada).reshape(depth, 16, 6, d)
    pad = jnp.zeros((depth, 16, 2, d), F32)
    ada = jnp.concatenate([ada, pad], axis=2)
    mod_lat = [ada[i, :bsz] for i in range(depth)]
    mod_ctx = [jnp.broadcast_to(ada[i, 8], (bsz, 8, d)) for i in range(depth)]

    cmat, smat = _dft_tables(bgd, bgd ** -0.5)
    p_ab = dict(
        w_in=w_in_ab[0].astype(BF16),
        v_gain=v_gain[0].reshape(1, aw),
        gmean=_group_mean_matrix(aw, group_dim),
        w_sp=w_spatial[0].transpose(1, 0, 2).reshape(CHUNK, A_GROUPS * CHUNK).astype(BF16),
        b_sp=jnp.repeat(b_spatial[0].T, group_dim, axis=1),
        cmat=cmat, smat=smat,
        w_out=w_out_ab[0].astype(BF16),
    )
    wg0, wu0, wd0 = (w.astype(BF16) for w in (w_gate_dense[0], w_up_dense[0], w_down_dense[0]))
    h = _mix_ab(x, mod_lat[0], p_ab)
    hc = _mix_ab(ctx, mod_ctx[0], p_ab)
    h = _dense_ffn(h, mod_lat[0], wg0, wu0, wd0)
    hc = _dense_ffn(hc, mod_ctx[0], wg0, wu0, wd0)

    cos, sin = _rope_tables(length)
    qw = C_HEADS * HEAD_DIM
    p_cd = dict(
        w_in=w_in_cd[0].astype(BF16),
        gmean=_group_mean_matrix(qw, HEAD_DIM),
        q_gain=jnp.tile(q_gain[0], C_HEADS).reshape(1, qw),
        k_gain=jnp.tile(k_gain[0], C_KV_HEADS).reshape(1, C_KV_HEADS * HEAD_DIM),
        cos=cos, sin=sin,
    )
    cq, ck, cv, dq, dk, dv = _project_cd(h, mod_lat[1], p_cd, rope=True)
    p_cd_ctx = dict(p_cd, cos=cos[:hc.shape[1]], sin=sin[:hc.shape[1]])
    _, ck_c, cv_c, _, dk_c, dv_c = _project_cd(hc, mod_ctx[1], p_cd_ctx, rope=False)
    o_c = _gqa(cq, jnp.concatenate([ck_c, ck], axis=1), jnp.concatenate([cv_c, cv], axis=1))
    o_d = _neighbourhood(dq, dk, dv, dk_c, dv_c, _na_bias_tables(rel_bias[0], length // GRID_W))
    h = _proj_residual(o_c, o_d, w_out_cd[0].astype(BF16), h, mod_lat[1], 2)

    wr = jnp.zeros((d, LANES), F32).at[:, :N_EXPERTS].set(w_router[0])
    n2, gates = _router(h, mod_lat[1], wr)
    return _moe_dense_final(n2, gates, h, mod_lat[1], final_gain.reshape(1, d),
                            w_gate_moe[0].astype(BF16), w_up_moe[0].astype(BF16),
                            w_down_moe[0].astype(BF16))
```

```python
import functools
import math

import numpy as np
import jax
import jax.numpy as jnp
from jax import lax
from jax.experimental import pallas as pl
from jax.experimental.pallas import tpu as pltpu

F32 = jnp.float32
BF16 = jnp.bfloat16

GRID_W = 64
EPS = 1e-6
CHUNK = 128
A_GROUPS = 8
B_GROUPS = 4
HEAD_DIM = 64
C_HEADS = 8
C_KV_HEADS = 2
C_GROUP = C_HEADS // C_KV_HEADS
D_HEADS = 8
NA_WIN_H = 8
NA_WIN_W = 16
ROPE_THETA = 10000.0
N_EXPERTS = 8
TOP_K = 2

LANES = 128
NA_QROWS = 4
NA_KROWS = NA_QROWS + NA_WIN_H
MASK_VALUE = -1e30
LOG2E = 1.4426950408889634
VMEM_LIMIT = 56 * 1024 * 1024

MOE_TOKENS = 2048
MOE_BLOCK = 256
MOE_ROWS = TOP_K * MOE_TOKENS + N_EXPERTS * MOE_BLOCK
MOE_DISPATCH_ROWS = 768
MOE_COMBINE_TOKENS = 256
MOE_FF_CHUNK = 512


def _cparams(*sem):
    return pltpu.CompilerParams(dimension_semantics=sem, vmem_limit_bytes=VMEM_LIMIT)


def _modulate(h, shift, scale):
    ms = jnp.mean(h * h, axis=-1, keepdims=True)
    return h * lax.rsqrt(ms + EPS) * (1.0 + scale) + shift


def _dot(a, b):
    return jnp.dot(a, b, preferred_element_type=F32)


def _dot_nt(a, b):
    return lax.dot_general(a, b, (((1,), (1,)), ((), ())), preferred_element_type=F32)


def _row_tile(length, target):
    t = min(length, target)
    assert length % t == 0
    return t


def _adaln_kernel(c_ref, w_ref, b_ref, o_ref):
    s = jax.nn.silu(c_ref[...]).astype(BF16)
    o_ref[...] = _dot(s, w_ref[...].astype(BF16)) + b_ref[...]


def _adaln(cond, w_ada, b_ada):
    depth, d, n = w_ada.shape
    r = cond.shape[0]
    tn = 512
    return pl.pallas_call(
        _adaln_kernel,
        grid=(depth, n // tn),
        in_specs=[
            pl.BlockSpec((r, d), lambda i, j: (0, 0)),
            pl.BlockSpec((None, d, tn), lambda i, j: (i, 0, j)),
            pl.BlockSpec((None, 1, tn), lambda i, j: (i, 0, j)),
        ],
        out_specs=pl.BlockSpec((None, r, tn), lambda i, j: (i, 0, j)),
        out_shape=jax.ShapeDtypeStruct((depth, r, n), F32),
        compiler_params=_cparams("parallel", "parallel"),
    )(cond, w_ada, b_ada.reshape(depth, 1, n))


def _ab_in_kernel(h_ref, mod_ref, w_ref, vg_ref, gm_ref, wsp_ref, bsp_ref, cm_ref, sm_ref,
                  a_ref, fc_ref, fs_ref, *, aw):
    tm = h_ref.shape[0]
    n = _modulate(h_ref[...], mod_ref[0:1, :], mod_ref[1:2, :])
    z = _dot(n.astype(BF16), w_ref[...])
    u = jax.nn.gelu(z[:, :aw])
    v = jax.nn.gelu(z[:, aw:2 * aw])
    fb = z[:, 2 * aw:].astype(BF16)
    ms = _dot((v * v).astype(BF16), gm_ref[...])
    vn = v * lax.rsqrt(ms + EPS) * vg_ref[...]
    group_dim = aw // A_GROUPS
    lane_group = lax.broadcasted_iota(jnp.int32, (CHUNK, aw), 1) // group_dim
    for c in range(tm // CHUNK):
        rows = slice(c * CHUNK, (c + 1) * CHUNK)
        vc = vn[rows, :]
        stack = jnp.concatenate(
            [jnp.where(lane_group == g, vc, 0.0).astype(BF16) for g in range(A_GROUPS)], axis=0)
        s = _dot(wsp_ref[...], stack) + bsp_ref[...]
        a_ref[rows, :] = (u[rows, :] * s).astype(BF16)
    bw = fb.shape[1] // B_GROUPS
    for g in range(B_GROUPS):
        cols = slice(g * bw, (g + 1) * bw)
        fc_ref[:, cols] = _dot(fb[:, cols], cm_ref[...]).astype(BF16)
        fs_ref[:, cols] = _dot(fb[:, cols], sm_ref[...]).astype(BF16)


def _seq_dft_kernel(c_ref, s_ref, fc_ref, fs_ref, o_ref):
    o_ref[...] = (_dot(c_ref[...], fc_ref[...]) - _dot(s_ref[...], fs_ref[...])).astype(BF16)


def _proj_residual_kernel(a_ref, b_ref, w_ref, h_ref, mod_ref, o_ref, *, gate_row):
    ka = a_ref.shape[1]
    y = _dot(a_ref[...], w_ref[:ka, :]) + _dot(b_ref[...], w_ref[ka:, :])
    o_ref[...] = h_ref[...] + mod_ref[gate_row:gate_row + 1, :] * y


def _proj_residual(a, b, w, h, mod, gate_row):
    bsz, length, d = h.shape
    tm = _row_tile(length, 1024)
    ka, kb = a.shape[2], b.shape[2]
    return pl.pallas_call(
        functools.partial(_proj_residual_kernel, gate_row=gate_row),
        grid=(bsz, length // tm),
        in_specs=[
            pl.BlockSpec((None, tm, ka), lambda i, j: (i, j, 0)),
            pl.BlockSpec((None, tm, kb), lambda i, j: (i, j, 0)),
            pl.BlockSpec((ka + kb, d), lambda i, j: (0, 0)),
            pl.BlockSpec((None, tm, d), lambda i, j: (i, j, 0)),
            pl.BlockSpec((None, 8, d), lambda i, j: (i, 0, 0)),
        ],
        out_specs=pl.BlockSpec((None, tm, d), lambda i, j: (i, j, 0)),
        out_shape=jax.ShapeDtypeStruct((bsz, length, d), F32),
        compiler_params=_cparams("parallel", "parallel"),
    )(a, b, w, h, mod)


def _dft_tables(n, scale):
    k = jnp.arange(n, dtype=jnp.int32)
    ang = ((k[:, None] * k[None, :]) % n).astype(F32) * (2.0 * math.pi / n)
    return (jnp.cos(ang) * scale).astype(BF16), (jnp.sin(ang) * scale).astype(BF16)


def _mix_ab(h, mod, p):
    bsz, length, d = h.shape
    aw = p["v_gain"].shape[1]
    bw_total = p["w_in"].shape[1] - 2 * aw
    tm = _row_tile(length, 512)
    const = lambda i, j: (0, 0)
    a_out, fc, fs = pl.pallas_call(
        functools.partial(_ab_in_kernel, aw=aw),
        grid=(bsz, length // tm),
        in_specs=[
            pl.BlockSpec((None, tm, d), lambda i, j: (i, j, 0)),
            pl.BlockSpec((None, 8, d), lambda i, j: (i, 0, 0)),
            pl.BlockSpec(p["w_in"].shape, const),
            pl.BlockSpec(p["v_gain"].shape, const),
            pl.BlockSpec(p["gmean"].shape, const),
            pl.BlockSpec(p["w_sp"].shape, const),
            pl.BlockSpec(p["b_sp"].shape, const),
            pl.BlockSpec(p["cmat"].shape, const),
            pl.BlockSpec(p["smat"].shape, const),
        ],
        out_specs=[
            pl.BlockSpec((None, tm, aw), lambda i, j: (i, j, 0)),
            pl.BlockSpec((None, tm, bw_total), lambda i, j: (i, j, 0)),
            pl.BlockSpec((None, tm, bw_total), lambda i, j: (i, j, 0)),
        ],
        out_shape=[
            jax.ShapeDtypeStruct((bsz, length, aw), BF16),
            jax.ShapeDtypeStruct((bsz, length, bw_total), BF16),
            jax.ShapeDtypeStruct((bsz, length, bw_total), BF16),
        ],
        compiler_params=_cparams("parallel", "parallel"),
    )(h, mod, p["w_in"], p["v_gain"], p["gmean"], p["w_sp"], p["b_sp"], p["cmat"], p["smat"])

    cl, sl = _dft_tables(length, length ** -0.5)
    tk = _row_tile(length, 512)
    b_out = pl.pallas_call(
        _seq_dft_kernel,
        grid=(length // tk, bsz),
        in_specs=[
            pl.BlockSpec((tk, length), lambda k, b: (k, 0)),
            pl.BlockSpec((tk, length), lambda k, b: (k, 0)),
            pl.BlockSpec((None, length, bw_total), lambda k, b: (b, 0, 0)),
            pl.BlockSpec((None, length, bw_total), lambda k, b: (b, 0, 0)),
        ],
        out_specs=pl.BlockSpec((None, tk, bw_total), lambda k, b: (b, k, 0)),
        out_shape=jax.ShapeDtypeStruct((bsz, length, bw_total), BF16),
        compiler_params=_cparams("parallel", "parallel"),
    )(cl, sl, fc, fs)
    return _proj_residual(a_out, b_out, p["w_out"], h, mod, 2)


def _dense_ffn_kernel(h_ref, mod_ref, wg_ref, wu_ref, wd_ref, o_ref, n_sc, acc_sc):
    f = pl.program_id(2)

    @pl.when(f == 0)
    def _():
        n_sc[...] = _modulate(h_ref[...], mod_ref[3:4, :], mod_ref[4:5, :]).astype(BF16)
        acc_sc[...] = jnp.zeros_like(acc_sc)

    n = n_sc[...]
    a = jax.nn.silu(_dot(n, wg_ref[...])) * _dot(n, wu_ref[...])
    acc_sc[...] += _dot(a.astype(BF16), wd_ref[...])

    @pl.when(f == pl.num_programs(2) - 1)
    def _():
        o_ref[...] = h_ref[...] + mod_ref[5:6, :] * acc_sc[...]


def _divisor_tile(n, target, unit=LANES):
    best = None
    for t in range(unit, min(n, target) + 1, unit):
        if n % t == 0:
            best = t
    assert best is not None
    return best


def _dense_ffn(h, mod, wg, wu, wd):
    bsz, length, d = h.shape
    ff = wg.shape[1]
    tm = _row_tile(length, 1024)
    tf = _divisor_tile(ff, 1408)
    return pl.pallas_call(
        _dense_ffn_kernel,
        grid=(bsz, length // tm, ff // tf),
        in_specs=[
            pl.BlockSpec((None, tm, d), lambda i, j, f: (i, j, 0)),
            pl.BlockSpec((None, 8, d), lambda i, j, f: (i, 0, 0)),
            pl.BlockSpec((d, tf), lambda i, j, f: (0, f)),
            pl.BlockSpec((d, tf), lambda i, j, f: (0, f)),
            pl.BlockSpec((tf, d), lambda i, j, f: (f, 0)),
        ],
        out_specs=pl.BlockSpec((None, tm, d), lambda i, j, f: (i, j, 0)),
        out_shape=jax.ShapeDtypeStruct((bsz, length, d), F32),
        scratch_shapes=[pltpu.VMEM((tm, d), BF16), pltpu.VMEM((tm, d), F32)],
        compiler_params=_cparams("parallel", "parallel", "arbitrary"),
    )(h, mod, wg, wu, wd)


def _pair_swap(x):
    width = x.shape[-1]
    lane = lax.broadcasted_iota(jnp.int32, x.shape, x.ndim - 1)
    nxt = pltpu.roll(x, width - 1, x.ndim - 1)
    prv = pltpu.roll(x, 1, x.ndim - 1)
    return jnp.where(lane % 2 == 0, nxt, prv)


def _rope(x, cos, sin):
    reps = x.shape[1] // LANES
    cos = jnp.concatenate([cos] * reps, axis=1) if reps > 1 else cos
    sin = jnp.concatenate([sin] * reps, axis=1) if reps > 1 else sin
    return x * cos + _pair_swap(x) * sin


def _cd_in_kernel(h_ref, mod_ref, w_ref, gm_ref, qg_ref, kg_ref, cos_ref, sin_ref,
                  cq_ref, ck_ref, cv_ref, dq_ref, dk_ref, dv_ref, *, rope):
    n = _modulate(h_ref[...], mod_ref[0:1, :], mod_ref[1:2, :])
    z = _dot(n.astype(BF16), w_ref[...])
    qw, kw, dw = cq_ref.shape[1], ck_ref.shape[1], dq_ref.shape[1]
    o1, o2, o3, o4, o5 = qw, qw + kw, qw + 2 * kw, qw + 2 * kw + dw, qw + 2 * kw + 2 * dw
    scale = HEAD_DIM ** -0.5

    zq = z[:, :o1]
    blocks = []
    for h in range(C_HEADS):
        t = zq[:, h * LANES:(h + 1) * LANES]
        ms = jnp.sum(t * t, axis=-1, keepdims=True) * (1.0 / HEAD_DIM)
        blocks.append(t * lax.rsqrt(ms + EPS))
    cq = jnp.concatenate(blocks, axis=1) * qg_ref[...]
    zk = z[:, o1:o2]
    ck = zk * lax.rsqrt(_dot((zk * zk).astype(BF16), gm_ref[...]) + EPS) * kg_ref[...]
    if rope:
        cq = _rope(cq, cos_ref[...], sin_ref[...])
        ck = _rope(ck, cos_ref[...], sin_ref[...])
    cq_ref[...] = (cq * (scale * LOG2E)).astype(BF16)
    ck_ref[...] = ck.astype(BF16)
    cv_ref[...] = z[:, o2:o3].astype(BF16)
    dq_ref[...] = (z[:, o3:o4] * scale).astype(BF16)
    dk_ref[...] = z[:, o4:o5].astype(BF16)
    dv_ref[...] = z[:, o5:].astype(BF16)


def _project_cd(h, mod, p, rope):
    bsz, length, d = h.shape
    qw, kw, dw = C_HEADS * LANES, C_KV_HEADS * HEAD_DIM, D_HEADS * HEAD_DIM
    tm = _row_tile(length, 512)
    const = lambda i, j: (0, 0)
    tok = lambda w: pl.BlockSpec((None, tm, w), lambda i, j: (i, j, 0))
    widths = (qw, kw, kw, dw, dw, dw)
    return pl.pallas_call(
        functools.partial(_cd_in_kernel, rope=rope),
        grid=(bsz, length // tm),
        in_specs=[
            tok(d),
            pl.BlockSpec((None, 8, d), lambda i, j: (i, 0, 0)),
            pl.BlockSpec(p["w_in"].shape, const),
            pl.BlockSpec(p["gmean"].shape, const),
            pl.BlockSpec(p["q_gain"].shape, const),
            pl.BlockSpec(p["k_gain"].shape, const),
            pl.BlockSpec((tm, LANES), lambda i, j: (j, 0)),
            pl.BlockSpec((tm, LANES), lambda i, j: (j, 0)),
        ],
        out_specs=[tok(w) for w in widths],
        out_shape=[jax.ShapeDtypeStruct((bsz, length, w), BF16) for w in widths],
        compiler_params=_cparams("parallel", "parallel"),
    )(h, mod, p["w_in"], p["gmean"], p["q_gain"], p["k_gain"], p["cos"], p["sin"])


def _gqa_kernel(q_ref, kt_ref, v_ref, o_ref):
    tq = q_ref.shape[0]
    lane = lax.broadcasted_iota(jnp.int32, (tq, LANES), 1)
    for kvh in range(C_KV_HEADS):
        heads = [kvh * C_GROUP + g for g in range(C_GROUP)]
        q4 = jnp.concatenate([q_ref[:, h * LANES:(h + 1) * LANES] for h in heads], axis=0)
        s = _dot(q4, kt_ref[...])
        p = jnp.exp2(s - jnp.max(s, axis=-1, keepdims=True)).astype(BF16)
        o = _dot(p, v_ref[...])
        on = o[:, :LANES] / o[:, LANES:LANES + 1]
        for pair in range(C_GROUP // 2):
            a = on[(2 * pair) * tq:(2 * pair + 1) * tq, :]
            b = on[(2 * pair + 1) * tq:(2 * pair + 2) * tq, :]
            if kvh == 0:
                blk = jnp.where(lane < HEAD_DIM, a, pltpu.roll(b, HEAD_DIM, 1))
            else:
                blk = jnp.where(lane < HEAD_DIM, pltpu.roll(a, HEAD_DIM, 1), b)
            blk_idx = kvh * (C_GROUP // 2) + pair
            o_ref[:, blk_idx * LANES:(blk_idx + 1) * LANES] = blk.astype(BF16)


def _gqa(q_ext, k_all, v_all):
    bsz, length, qw = q_ext.shape
    lk = k_all.shape[1]
    assert C_KV_HEADS * HEAD_DIM == LANES and C_KV_HEADS == 2
    kt = jnp.swapaxes(k_all, 1, 2)
    v_ext = jnp.concatenate([v_all, jnp.ones_like(v_all)], axis=2)
    tq = _row_tile(length, 64)
    ow = C_HEADS * HEAD_DIM
    return pl.pallas_call(
        _gqa_kernel,
        grid=(bsz, length // tq),
        in_specs=[
            pl.BlockSpec((None, tq, qw), lambda i, j: (i, j, 0)),
            pl.BlockSpec((None, LANES, lk), lambda i, j: (i, 0, 0)),
            pl.BlockSpec((None, lk, 2 * LANES), lambda i, j: (i, 0, 0)),
        ],
        out_specs=pl.BlockSpec((None, tq, ow), lambda i, j: (i, j, 0)),
        out_shape=jax.ShapeDtypeStruct((bsz, length, ow), BF16),
        compiler_params=_cparams("parallel", "parallel"),
    )(q_ext, kt, v_ext)


def _na_block_start(qb, rows):
    return jnp.clip(qb * NA_QROWS - NA_WIN_H // 2, 0, rows - NA_KROWS)


def _na_bias_tables(rel_bias, rows):
    nblk = rows // NA_QROWS
    n_dr, n_dc = 2 * NA_WIN_H - 1, 2 * NA_WIN_W - 1
    i = np.arange(NA_QROWS)[:, None]
    a = np.arange(NA_KROWS)[None, :]
    j = np.arange(GRID_W)[:, None]
    kc = np.arange(GRID_W)[None, :]
    col_start = np.clip(j - NA_WIN_W // 2, 0, GRID_W - NA_WIN_W)
    valid_col = (kc >= col_start) & (kc < col_start + NA_WIN_W)
    dc = np.clip(kc - j + (NA_WIN_W - 1), 0, n_dc - 1)
    onehot_c = (dc[:, :, None] == np.arange(n_dc)).astype(np.float32)
    onehot_r, valid = [], []
    for qb in (0, 1, nblk - 1):
        r = qb * NA_QROWS + i
        r0 = np.clip(r - NA_WIN_H // 2, 0, rows - NA_WIN_H)
        kr = int(np.clip(qb * NA_QROWS - NA_WIN_H // 2, 0, rows - NA_KROWS)) + a
        valid_row = (kr >= r0) & (kr < r0 + NA_WIN_H)
        dr = np.clip(kr - r + (NA_WIN_H - 1), 0, n_dr - 1)
        onehot_r.append((dr[:, :, None] == np.arange(n_dr)).astype(np.float32))
        valid.append(valid_row[:, None, :, None] & valid_col[None, :, None, :])
    onehot_r = jnp.asarray(np.stack(onehot_r))
    valid = np.stack(valid)
    hp = lax.Precision.HIGHEST
    by_row = jnp.einsum("hrc,ziar->zhiac", rel_bias.astype(F32), onehot_r, precision=hp)
    table = jnp.einsum("zhiac,jkc->zhijak", by_row, jnp.asarray(onehot_c), precision=hp)
    table = jnp.where(valid[:, None], table, MASK_VALUE)
    return table.reshape(3 * rel_bias.shape[0], NA_QROWS * GRID_W, NA_KROWS * GRID_W)


def _na_kernel(q_ref, k_ref, v_ref, kc_ref, vc_ref, bias_ref, o_ref, *, rows):
    qb = pl.program_id(1)
    nk = NA_KROWS * GRID_W
    start = pl.multiple_of(_na_block_start(qb, rows) * GRID_W, GRID_W)
    k_nb = k_ref[pl.ds(start, nk), :]
    v_nb = v_ref[pl.ds(start, nk), :]
    for h in range(D_HEADS):
        cols = slice(h * HEAD_DIM, (h + 1) * HEAD_DIM)
        q = q_ref[:, cols]
        s_nb = _dot_nt(q, k_nb[:, cols]) + bias_ref[h]
        s_c = _dot_nt(q, kc_ref[:, cols])
        m = jnp.maximum(jnp.max(s_nb, axis=-1, keepdims=True), jnp.max(s_c, axis=-1, keepdims=True))
        p_nb = jnp.exp(s_nb - m)
        p_c = jnp.exp(s_c - m)
        l = jnp.sum(p_nb, axis=-1, keepdims=True) + jnp.sum(p_c, axis=-1, keepdims=True)
        o = _dot(p_nb.astype(BF16), v_nb[:, cols]) + _dot(p_c.astype(BF16), vc_ref[:, cols])
        o_ref[:, cols] = (o / l).astype(BF16)


def _neighbourhood(dq, dk, dv, dk_c, dv_c, bias):
    bsz, length, w = dq.shape
    rows = length // GRID_W
    assert rows % NA_QROWS == 0 and rows >= NA_KROWS
    nblk = rows // NA_QROWS
    tq = NA_QROWS * GRID_W
    cl = dk_c.shape[1]

    def bias_class(i, j):
        cls = jnp.where(j == 0, 0, jnp.where(j == nblk - 1, 2, 1))
        return (cls, 0, 0)

    return pl.pallas_call(
        functools.partial(_na_kernel, rows=rows),
        grid=(bsz, nblk),
        in_specs=[
            pl.BlockSpec((None, tq, w), lambda i, j: (i, j, 0)),
            pl.BlockSpec((None, length, w), lambda i, j: (i, 0, 0)),
            pl.BlockSpec((None, length, w), lambda i, j: (i, 0, 0)),
            pl.BlockSpec((None, cl, w), lambda i, j: (i, 0, 0)),
            pl.BlockSpec((None, cl, w), lambda i, j: (i, 0, 0)),
            pl.BlockSpec((D_HEADS, tq, NA_KROWS * GRID_W), bias_class),
        ],
        out_specs=pl.BlockSpec((None, tq, w), lambda i, j: (i, j, 0)),
        out_shape=jax.ShapeDtypeStruct((bsz, length, w), BF16),
        compiler_params=_cparams("parallel", "arbitrary"),
    )(dq, dk, dv, dk_c, dv_c, bias)


def _router_kernel(h_ref, mod_ref, wr_ref, n_ref, sel_ref):
    n = _modulate(h_ref[...], mod_ref[3:4, :], mod_ref[4:5, :])
    n_hi = n.astype(BF16)
    n_lo = (n - n_hi.astype(F32)).astype(BF16)
    w = wr_ref[...]
    w_hi = w.astype(BF16)
    w_lo = (w - w_hi.astype(F32)).astype(BF16)
    logits = _dot(n_hi, w_hi) + (_dot(n_lo, w_hi) + _dot(n_hi, w_lo))
    lane = lax.broadcasted_iota(jnp.int32, logits.shape, 1)
    logits = jnp.where(lane < N_EXPERTS, logits, -jnp.inf)
    m1 = jnp.max(logits, axis=-1, keepdims=True)
    i1 = jnp.min(jnp.where(logits == m1, lane, LANES), axis=-1, keepdims=True)
    rest = jnp.where(lane == i1, -jnp.inf, logits)
    m2 = jnp.max(rest, axis=-1, keepdims=True)
    i2 = jnp.min(jnp.where(rest == m2, lane, LANES), axis=-1, keepdims=True)
    e2 = jnp.exp(m2 - m1)
    w1 = 1.0 / (1.0 + e2)
    w2 = e2 / (1.0 + e2)
    sel = jnp.where(lane == 0, i1.astype(F32), jnp.where(lane == 1, i2.astype(F32),
                    jnp.where(lane == 2, w1, jnp.where(lane == 3, w2, 0.0))))
    sel_ref[...] = sel
    n_ref[...] = n_hi


def _router(h, mod, w_router_padded):
    bsz, length, d = h.shape
    tm = _row_tile(length, 1024)
    return pl.pallas_call(
        _router_kernel,
        grid=(bsz, length // tm),
        in_specs=[
            pl.BlockSpec((None, tm, d), lambda i, j: (i, j, 0)),
            pl.BlockSpec((None, 8, d), lambda i, j: (i, 0, 0)),
            pl.BlockSpec((d, LANES), lambda i, j: (0, 0)),
        ],
        out_specs=[
            pl.BlockSpec((None, tm, d), lambda i, j: (i, j, 0)),
            pl.BlockSpec((None, tm, LANES), lambda i, j: (i, j, 0)),
        ],
        out_shape=[
            jax.ShapeDtypeStruct((bsz, length, d), BF16),
            jax.ShapeDtypeStruct((bsz, length, LANES), F32),
        ],
        compiler_params=_cparams("parallel", "parallel"),
    )(h, mod, w_router_padded)


def _moe_dispatch_kernel(pos_ref, w_ref, n_ref, xs_ref, gs_ref):
    nrow = xs_ref.shape[0]
    base = pl.program_id(1) * nrow
    row = base + lax.broadcasted_iota(jnp.int32, (nrow, 1), 0)
    eq0 = pos_ref[0:1, :] == row
    eq1 = pos_ref[1:2, :] == row
    perm = jnp.where(eq0, 1.0, jnp.where(eq1, 1.0, 0.0)).astype(BF16)
    xs_ref[...] = _dot(perm, n_ref[...]).astype(BF16)
    gate = jnp.where(eq0, w_ref[0:1, :], jnp.where(eq1, w_ref[1:2, :], 0.0))
    gs_ref[...] = jnp.sum(gate, axis=-1, keepdims=True)


def _moe_expert_kernel(blk_ref, exp_ref, nused_ref, xs_ref, gs_ref, wg_ref, wu_ref, wd_ref, ys_ref):
    i = pl.program_id(0)

    @pl.when(i < nused_ref[0])
    def _():
        x = xs_ref[...]
        ff = wg_ref.shape[1]
        y = jnp.zeros((x.shape[0], wd_ref.shape[1]), F32)
        for f in range(0, ff, MOE_FF_CHUNK):
            cols = slice(f, min(f + MOE_FF_CHUNK, ff))
            a = jax.nn.silu(_dot(x, wg_ref[:, cols])) * _dot(x, wu_ref[:, cols])
            y = y + _dot(a.astype(BF16), wd_ref[cols, :])
        ys_ref[...] = (y * gs_ref[...]).astype(BF16)

    @pl.when(i >= nused_ref[0])
    def _():
        ys_ref[...] = jnp.zeros_like(ys_ref)


def _moe_combine_kernel(pos_ref, ys_ref, h_ref, mod_ref, fg_ref, o_ref):
    ntok, nrow = pos_ref.shape[0], ys_ref.shape[0]
    row = lax.broadcasted_iota(jnp.int32, (ntok, nrow), 1)
    perm = jnp.where(pos_ref[:, 0:1] == row, 1.0, jnp.where(pos_ref[:, 1:2] == row, 1.0, 0.0)).astype(BF16)
    y = h_ref[...] + mod_ref[5:6, :] * _dot(perm, ys_ref[...])
    ms = jnp.mean(y * y, axis=-1, keepdims=True)
    o_ref[...] = y * lax.rsqrt(ms + EPS) * fg_ref[...]


def _moe_plan(sel, n_tiles, tile_tokens, rows_per_tile):
    blocks_per_tile = rows_per_tile // MOE_BLOCK
    experts = sel[:, :TOP_K].astype(jnp.int32).reshape(n_tiles, tile_tokens * TOP_K)
    weights = sel[:, TOP_K:2 * TOP_K].reshape(n_tiles, tile_tokens, TOP_K)
    onehot = (experts[:, :, None] == jnp.arange(N_EXPERTS, dtype=jnp.int32)).astype(jnp.int32)
    csum = jnp.cumsum(onehot, axis=1)
    rank = jnp.sum((csum - onehot) * onehot, axis=-1)
    count = csum[:, -1, :]
    nblk = (count + MOE_BLOCK - 1) // MOE_BLOCK
    blk_end = jnp.cumsum(nblk, axis=1)
    blk_off = blk_end - nblk
    pos = jnp.sum(onehot * blk_off[:, None, :], axis=-1) * MOE_BLOCK + rank
    pos = pos.reshape(n_tiles, tile_tokens, TOP_K)

    b = jnp.arange(blocks_per_tile, dtype=jnp.int32)
    blk_expert = jnp.sum((blk_end[:, None, :] <= b[None, :, None]).astype(jnp.int32), axis=-1)
    flat_expert = blk_expert.reshape(-1)
    n_blocks = flat_expert.shape[0]
    order = jnp.argsort(flat_expert * n_blocks + jnp.arange(n_blocks, dtype=jnp.int32)).astype(jnp.int32)
    n_used = jnp.sum((flat_expert < N_EXPERTS).astype(jnp.int32))
    last_used = order[n_used - 1]
    blk_exp = jnp.minimum(flat_expert[order], flat_expert[last_used])
    return pos, weights, order, blk_exp, n_used.reshape(1)


def _moe_final(n2, sel, h, mod, final_gain, wg, wu, wd):
    bsz, length, d = h.shape
    n_exp, _, ff = wg.shape
    tokens = bsz * length
    tt = _row_tile(length, MOE_TOKENS)
    n_tiles = tokens // tt
    rows = TOP_K * tt + N_EXPERTS * MOE_BLOCK
    drows = _divisor_tile(rows, MOE_DISPATCH_ROWS, MOE_BLOCK)
    ctok = _row_tile(tt, MOE_COMBINE_TOKENS)
    pos, weights, blk_ids, blk_exp, n_used = _moe_plan(sel.reshape(tokens, LANES), n_tiles, tt, rows)

    pos_rows = jnp.swapaxes(pos, 1, 2)
    w_rows = jnp.swapaxes(weights, 1, 2)
    xs, gs = pl.pallas_call(
        _moe_dispatch_kernel,
        grid=(n_tiles, rows // drows),
        in_specs=[
            pl.BlockSpec((None, TOP_K, tt), lambda t, r: (t, 0, 0)),
            pl.BlockSpec((None, TOP_K, tt), lambda t, r: (t, 0, 0)),
            pl.BlockSpec((tt, d), lambda t, r: (t, 0)),
        ],
        out_specs=[
            pl.BlockSpec((None, drows, d), lambda t, r: (t, r, 0)),
            pl.BlockSpec((None, drows, 1), lambda t, r: (t, r, 0)),
        ],
        out_shape=[
            jax.ShapeDtypeStruct((n_tiles, rows, d), BF16),
            jax.ShapeDtypeStruct((n_tiles, rows, 1), F32),
        ],
        compiler_params=_cparams("parallel", "parallel"),
    )(pos_rows, w_rows, n2.reshape(tokens, d))

    n_blocks = n_tiles * (rows // MOE_BLOCK)
    resident = pl.Buffered(1)
    ys = pl.pallas_call(
        _moe_expert_kernel,
        grid_spec=pltpu.PrefetchScalarGridSpec(
            num_scalar_prefetch=3,
            grid=(n_blocks,),
            in_specs=[
                pl.BlockSpec((MOE_BLOCK, d), lambda i, blk, ex, nu: (blk[i], 0)),
                pl.BlockSpec((MOE_BLOCK, 1), lambda i, blk, ex, nu: (blk[i], 0)),
                pl.BlockSpec((None, d, ff), lambda i, blk, ex, nu: (ex[i], 0, 0), pipeline_mode=resident),
                pl.BlockSpec((None, d, ff), lambda i, blk, ex, nu: (ex[i], 0, 0), pipeline_mode=resident),
                pl.BlockSpec((None, ff, d), lambda i, blk, ex, nu: (ex[i], 0, 0), pipeline_mode=resident),
            ],
            out_specs=pl.BlockSpec((MOE_BLOCK, d), lambda i, blk, ex, nu: (blk[i], 0)),
        ),
        out_shape=jax.ShapeDtypeStruct((n_tiles * rows, d), BF16),
        compiler_params=_cparams("arbitrary"),
    )(blk_ids, blk_exp, n_used, xs.reshape(n_tiles * rows, d), gs.reshape(n_tiles * rows, 1), wg, wu, wd)

    tiles_per_seq = length // tt
    out = pl.pallas_call(
        _moe_combine_kernel,
        grid=(n_tiles, tt // ctok),
        in_specs=[
            pl.BlockSpec((None, ctok, TOP_K), lambda t, c: (t, c, 0)),
            pl.BlockSpec((None, rows, d), lambda t, c: (t, 0, 0)),
            pl.BlockSpec((None, ctok, d), lambda t, c: (t, c, 0)),
            pl.BlockSpec((None, 8, d), lambda t, c: (t // tiles_per_seq, 0, 0)),
            pl.BlockSpec((1, d), lambda t, c: (0, 0)),
        ],
        out_specs=pl.BlockSpec((None, ctok, d), lambda t, c: (t, c, 0)),
        out_shape=jax.ShapeDtypeStruct((n_tiles, tt, d), F32),
        compiler_params=_cparams("parallel", "parallel"),
    )(pos, ys.reshape(n_tiles, rows, d), h.reshape(n_tiles, tt, d), mod, final_gain)
    return out.reshape(bsz, length, d)


def _group_mean_matrix(width, group):
    return jnp.asarray(np.kron(np.eye(width // group), np.full((group, group), 1.0 / group)), BF16)


def _rope_tables(length):
    t = jnp.arange(length, dtype=jnp.int32)
    row = (t // GRID_W).astype(F32)
    col = (t % GRID_W).astype(F32)
    n_axis = HEAD_DIM // 4
    inv_freq = ROPE_THETA ** (-jnp.arange(n_axis, dtype=F32) / n_axis)
    ang = jnp.concatenate([row[:, None] * inv_freq, col[:, None] * inv_freq], axis=-1)
    cos = jnp.repeat(jnp.cos(ang), 2, axis=-1)
    sin = jnp.repeat(jnp.sin(ang), 2, axis=-1)
    sign = jnp.tile(jnp.asarray([-1.0, 1.0], F32), HEAD_DIM // 2)
    reps = LANES // HEAD_DIM
    return jnp.tile(cos, (1, reps)), jnp.tile(sin * sign, (1, reps))


def _spread_q_heads(w_q):
    d = w_q.shape[0]
    w = w_q.reshape(d, C_HEADS, HEAD_DIM)
    zeros = jnp.zeros_like(w)
    first_kv = (jnp.arange(C_HEADS) // C_GROUP == 0)[None, :, None]
    lo = jnp.where(first_kv, w, zeros)
    hi = jnp.where(first_kv, zeros, w)
    return jnp.concatenate([lo, hi], axis=2).reshape(d, C_HEADS * LANES)


def kernel(x, c, ctx, c_ctx, w_ada, b_ada, w_in_ab, v_gain, w_spatial, b_spatial, w_out_ab,
           w_gate_dense, w_up_dense, w_down_dense, w_in_cd, q_gain, k_gain, rel_bias, w_out_cd,
           w_router, w_gate_moe, w_up_moe, w_down_moe, final_gain):
    bsz, length, d = x.shape
    depth = w_ada.shape[0]
    assert depth == 2 and bsz <= 8
    aw = v_gain.shape[1]
    group_dim = aw // A_GROUPS
    bgd = (w_in_ab.shape[2] - 2 * aw) // B_GROUPS

    cond = jnp.zeros((16, d), F32).at[:bsz].set(c).at[8].set(c_ctx)
    ada = _adaln(cond, w_ada, b_ada).reshape(depth, 16, 6, d)
    pad = jnp.zeros((depth, 16, 2, d), F32)
    ada = jnp.concatenate([ada, pad], axis=2)
    mod_lat = [ada[i, :bsz] for i in range(depth)]
    mod_ctx = [jnp.broadcast_to(ada[i, 8], (bsz, 8, d)) for i in range(depth)]

    cmat, smat = _dft_tables(bgd, bgd ** -0.5)
    p_ab = dict(
        w_in=w_in_ab[0].astype(BF16),
        v_gain=v_gain[0].reshape(1, aw),
        gmean=_group_mean_matrix(aw, group_dim),
        w_sp=w_spatial[0].transpose(1, 0, 2).reshape(CHUNK, A_GROUPS * CHUNK).astype(BF16),
        b_sp=jnp.repeat(b_spatial[0].T, group_dim, axis=1),
        cmat=cmat, smat=smat,
        w_out=w_out_ab[0].astype(BF16),
    )
    wg0, wu0, wd0 = (w.astype(BF16) for w in (w_gate_dense[0], w_up_dense[0], w_down_dense[0]))
    h = _mix_ab(x, mod_lat[0], p_ab)
    hc = _mix_ab(ctx, mod_ctx[0], p_ab)
    h = _dense_ffn(h, mod_lat[0], wg0, wu0, wd0)
    hc = _dense_ffn(hc, mod_ctx[0], wg0, wu0, wd0)

    cos, sin = _rope_tables(length)
    qw = C_HEADS * HEAD_DIM
    kw = C_KV_HEADS * HEAD_DIM
    w_cd = w_in_cd[0]
    p_cd = dict(
        w_in=jnp.concatenate([_spread_q_heads(w_cd[:, :qw]), w_cd[:, qw:]], axis=1).astype(BF16),
        gmean=_group_mean_matrix(kw, HEAD_DIM),
        q_gain=jnp.tile(q_gain[0], C_HEADS * LANES // HEAD_DIM).reshape(1, C_HEADS * LANES),
        k_gain=jnp.tile(k_gain[0], C_KV_HEADS).reshape(1, kw),
        cos=cos, sin=sin,
    )
    cq, ck, cv, dq, dk, dv = _project_cd(h, mod_lat[1], p_cd, rope=True)
    p_cd_ctx = dict(p_cd, cos=cos[:hc.shape[1]], sin=sin[:hc.shape[1]])
    _, ck_c, cv_c, _, dk_c, dv_c = _project_cd(hc, mod_ctx[1], p_cd_ctx, rope=False)
    o_c = _gqa(cq, jnp.concatenate([ck_c, ck], axis=1), jnp.concatenate([cv_c, cv], axis=1))
    o_d = _neighbourhood(dq, dk, dv, dk_c, dv_c, _na_bias_tables(rel_bias[0], length // GRID_W))
    h = _proj_residual(o_c, o_d, w_out_cd[0].astype(BF16), h, mod_lat[1], 2)

    wr = jnp.zeros((d, LANES), F32).at[:, :N_EXPERTS].set(w_router[0])
    n2, sel = _router(h, mod_lat[1], wr)
    return _moe_final(n2, sel, h, mod_lat[1], final_gain.reshape(1, d),
                      w_gate_moe[0].astype(BF16), w_up_moe[0].astype(BF16), w_down_moe[0].astype(BF16))
```

```python
import functools
import math

import numpy as np
import jax
import jax.numpy as jnp
from jax import lax
from jax.experimental import pallas as pl
from jax.experimental.pallas import tpu as pltpu

F32 = jnp.float32
BF16 = jnp.bfloat16

GRID_W = 64
EPS = 1e-6
CHUNK = 128
A_GROUPS = 8
B_GROUPS = 4
HEAD_DIM = 64
C_HEADS = 8
C_KV_HEADS = 2
C_GROUP = C_HEADS // C_KV_HEADS
D_HEADS = 8
NA_WIN_H = 8
NA_WIN_W = 16
ROPE_THETA = 10000.0
N_EXPERTS = 8
TOP_K = 2

LANES = 128
NA_QROWS = 4
NA_KROWS = NA_QROWS + NA_WIN_H
MASK_VALUE = -1e30
LOG2E = 1.4426950408889634
VMEM_LIMIT = 56 * 1024 * 1024

MOE_TOKENS = 2048
MOE_BLOCK = 256
MOE_ROWS = TOP_K * MOE_TOKENS + N_EXPERTS * MOE_BLOCK
MOE_DISPATCH_ROWS = 768
MOE_COMBINE_TOKENS = 256
MOE_FF_CHUNK = 512


def _cparams(*sem):
    return pltpu.CompilerParams(dimension_semantics=sem, vmem_limit_bytes=VMEM_LIMIT)


def _modulate(h, shift, scale):
    ms = jnp.mean(h * h, axis=-1, keepdims=True)
    return h * lax.rsqrt(ms + EPS) * (1.0 + scale) + shift


def _dot(a, b):
    return jnp.dot(a, b, preferred_element_type=F32)


def _dot_nt(a, b):
    return lax.dot_general(a, b, (((1,), (1,)), ((), ())), preferred_element_type=F32)


def _row_tile(length, target):
    t = min(length, target)
    assert length % t == 0
    return t


def _adaln_kernel(c_ref, w_ref, b_ref, o_ref):
    s = jax.nn.silu(c_ref[...]).astype(BF16)
    o_ref[...] = _dot(s, w_ref[...].astype(BF16)) + b_ref[...]


def _adaln(cond, w_ada, b_ada):
    depth, d, n = w_ada.shape
    r = cond.shape[0]
    tn = 512
    return pl.pallas_call(
        _adaln_kernel,
        grid=(depth, n // tn),
        in_specs=[
            pl.BlockSpec((r, d), lambda i, j: (0, 0)),
            pl.BlockSpec((None, d, tn), lambda i, j: (i, 0, j)),
            pl.BlockSpec((None, 1, tn), lambda i, j: (i, 0, j)),
        ],
        out_specs=pl.BlockSpec((None, r, tn), lambda i, j: (i, 0, j)),
        out_shape=jax.ShapeDtypeStruct((depth, r, n), F32),
        compiler_params=_cparams("parallel", "parallel"),
    )(cond, w_ada, b_ada.reshape(depth, 1, n))


def _ab_in_kernel(h_ref, mod_ref, w_ref, vg_ref, gm_ref, wsp_ref, bsp_ref, cm_ref, sm_ref,
                  a_ref, fc_ref, fs_ref, *, aw):
    tm = h_ref.shape[0]
    n = _modulate(h_ref[...], mod_ref[0:1, :], mod_ref[1:2, :])
    z = _dot(n.astype(BF16), w_ref[...])
    u = jax.nn.gelu(z[:, :aw])
    v = jax.nn.gelu(z[:, aw:2 * aw])
    fb = z[:, 2 * aw:].astype(BF16)
    ms = _dot((v * v).astype(BF16), gm_ref[...])
    vn = v * lax.rsqrt(ms + EPS) * vg_ref[...]
    group_dim = aw // A_GROUPS
    lane_group = lax.broadcasted_iota(jnp.int32, (CHUNK, aw), 1) // group_dim
    for c in range(tm // CHUNK):
        rows = slice(c * CHUNK, (c + 1) * CHUNK)
        vc = vn[rows, :]
        stack = jnp.concatenate(
            [jnp.where(lane_group == g, vc, 0.0).astype(BF16) for g in range(A_GROUPS)], axis=0)
        s = _dot(wsp_ref[...], stack) + bsp_ref[...]
        a_ref[rows, :] = (u[rows, :] * s).astype(BF16)
    bw = fb.shape[1] // B_GROUPS
    for g in range(B_GROUPS):
        cols = slice(g * bw, (g + 1) * bw)
        fc_ref[:, cols] = _dot(fb[:, cols], cm_ref[...]).astype(BF16)
        fs_ref[:, cols] = _dot(fb[:, cols], sm_ref[...]).astype(BF16)


def _seq_dft_kernel(c_ref, s_ref, fc_ref, fs_ref, o_ref):
    o_ref[...] = (_dot(c_ref[...], fc_ref[...]) - _dot(s_ref[...], fs_ref[...])).astype(BF16)


def _proj_residual_kernel(a_ref, b_ref, w_ref, h_ref, mod_ref, o_ref, *, gate_row):
    ka = a_ref.shape[1]
    y = _dot(a_ref[...], w_ref[:ka, :]) + _dot(b_ref[...], w_ref[ka:, :])
    o_ref[...] = h_ref[...] + mod_ref[gate_row:gate_row + 1, :] * y


def _proj_residual(a, b, w, h, mod, gate_row):
    bsz, length, d = h.shape
    tm = _row_tile(length, 1024)
    ka, kb = a.shape[2], b.shape[2]
    return pl.pallas_call(
        functools.partial(_proj_residual_kernel, gate_row=gate_row),
        grid=(bsz, length // tm),
        in_specs=[
            pl.BlockSpec((None, tm, ka), lambda i, j: (i, j, 0)),
            pl.BlockSpec((None, tm, kb), lambda i, j: (i, j, 0)),
            pl.BlockSpec((ka + kb, d), lambda i, j: (0, 0)),
            pl.BlockSpec((None, tm, d), lambda i, j: (i, j, 0)),
            pl.BlockSpec((None, 8, d), lambda i, j: (i, 0, 0)),
        ],
        out_specs=pl.BlockSpec((None, tm, d), lambda i, j: (i, j, 0)),
        out_shape=jax.ShapeDtypeStruct((bsz, length, d), F32),
        compiler_params=_cparams("parallel", "parallel"),
    )(a, b, w, h, mod)


def _dft_tables(n, scale):
    k = jnp.arange(n, dtype=jnp.int32)
    ang = ((k[:, None] * k[None, :]) % n).astype(F32) * (2.0 * math.pi / n)
    return (jnp.cos(ang) * scale).astype(BF16), (jnp.sin(ang) * scale).astype(BF16)


def _mix_ab(h, mod, p):
    bsz, length, d = h.shape
    aw = p["v_gain"].shape[1]
    bw_total = p["w_in"].shape[1] - 2 * aw
    tm = _row_tile(length, 512)
    const = lambda i, j: (0, 0)
    a_out, fc, fs = pl.pallas_call(
        functools.partial(_ab_in_kernel, aw=aw),
        grid=(bsz, length // tm),
        in_specs=[
            pl.BlockSpec((None, tm, d), lambda i, j: (i, j, 0)),
            pl.BlockSpec((None, 8, d), lambda i, j: (i, 0, 0)),
            pl.BlockSpec(p["w_in"].shape, const),
            pl.BlockSpec(p["v_gain"].shape, const),
            pl.BlockSpec(p["gmean"].shape, const),
            pl.BlockSpec(p["w_sp"].shape, const),
            pl.BlockSpec(p["b_sp"].shape, const),
            pl.BlockSpec(p["cmat"].shape, const),
            pl.BlockSpec(p["smat"].shape, const),
        ],
        out_specs=[
            pl.BlockSpec((None, tm, aw), lambda i, j: (i, j, 0)),
            pl.BlockSpec((None, tm, bw_total), lambda i, j: (i, j, 0)),
            pl.BlockSpec((None, tm, bw_total), lambda i, j: (i, j, 0)),
        ],
        out_shape=[
            jax.ShapeDtypeStruct((bsz, length, aw), BF16),
            jax.ShapeDtypeStruct((bsz, length, bw_total), BF16),
            jax.ShapeDtypeStruct((bsz, length, bw_total), BF16),
        ],
        compiler_params=_cparams("parallel", "parallel"),
    )(h, mod, p["w_in"], p["v_gain"], p["gmean"], p["w_sp"], p["b_sp"], p["cmat"], p["smat"])

    b_out = _seq_dft(fc, fs)
    return _proj_residual(a_out, b_out, p["w_out"], h, mod, 2)


def _seq_dft_direct(fc, fs):
    bsz, length, width = fc.shape
    cl, sl = _dft_tables(length, length ** -0.5)
    tk = _row_tile(length, 512)
    return pl.pallas_call(
        _seq_dft_kernel,
        grid=(length // tk, bsz),
        in_specs=[
            pl.BlockSpec((tk, length), lambda k, b: (k, 0)),
            pl.BlockSpec((tk, length), lambda k, b: (k, 0)),
            pl.BlockSpec((None, length, width), lambda k, b: (b, 0, 0)),
            pl.BlockSpec((None, length, width), lambda k, b: (b, 0, 0)),
        ],
        out_specs=pl.BlockSpec((None, tk, width), lambda k, b: (b, k, 0)),
        out_shape=jax.ShapeDtypeStruct((bsz, length, width), BF16),
        compiler_params=_cparams("parallel", "parallel"),
    )(cl, sl, fc, fs)


def _fft_rows_kernel(fc_ref, fs_ref, ma_ref, tc_ref, ts_ref, yr_ref, yi_ref):
    n1 = fc_ref.shape[0]
    y = _dot(ma_ref[...], jnp.concatenate([fc_ref[...], fs_ref[...]], axis=0))
    yr, yi = y[:n1, :], y[n1:, :]
    c, s = tc_ref[...], ts_ref[...]
    yr_ref[...] = (yr * c + yi * s).astype(BF16)
    yi_ref[...] = (yi * c - yr * s).astype(BF16)


def _fft_cols_kernel(yr_ref, yi_ref, mb_ref, o_ref):
    width = yr_ref.shape[2]
    for j in range(yr_ref.shape[0]):
        y = jnp.concatenate([yr_ref[j], yi_ref[j]], axis=0)
        o_ref[:, j * width:(j + 1) * width] = _dot(mb_ref[...], y).astype(BF16)


def _seq_dft(fc, fs):
    bsz, length, width = fc.shape
    n2 = GRID_W
    n1 = length // n2
    if n1 < 16 or length % n2:
        return _seq_dft_direct(fc, fs)
    ncol = n2 * width

    def angles(a, b, period):
        prod = (jnp.arange(a, dtype=jnp.int32)[:, None] * jnp.arange(b, dtype=jnp.int32)[None, :]) % period
        return prod.astype(F32) * (2.0 * math.pi / period)

    th = angles(n1, n1, n1)
    c1, s1 = jnp.cos(th) * n1 ** -0.5, jnp.sin(th) * n1 ** -0.5
    ma = jnp.concatenate([jnp.concatenate([c1, -s1], axis=1),
                          jnp.concatenate([-s1, -c1], axis=1)], axis=0).astype(BF16)
    ph = angles(n1, n2, length)
    tc = jnp.repeat(jnp.cos(ph), width, axis=1)
    ts = jnp.repeat(jnp.sin(ph), width, axis=1)
    ps = angles(n2, n2, n2)
    mb = (jnp.concatenate([jnp.cos(ps), jnp.sin(ps)], axis=1) * n2 ** -0.5).astype(BF16)

    cols = _divisor_tile(ncol, 4096)
    row_blk = lambda b, j: (b, 0, j)
    yr, yi = pl.pallas_call(
        _fft_rows_kernel,
        grid=(bsz, ncol // cols),
        in_specs=[
            pl.BlockSpec((None, n1, cols), row_blk),
            pl.BlockSpec((None, n1, cols), row_blk),
            pl.BlockSpec((2 * n1, 2 * n1), lambda b, j: (0, 0)),
            pl.BlockSpec((n1, cols), lambda b, j: (0, j)),
            pl.BlockSpec((n1, cols), lambda b, j: (0, j)),
        ],
        out_specs=[pl.BlockSpec((None, n1, cols), row_blk)] * 2,
        out_shape=[jax.ShapeDtypeStruct((bsz, n1, ncol), BF16)] * 2,
        compiler_params=_cparams("parallel", "parallel"),
    )(fc.reshape(bsz, n1, ncol), fs.reshape(bsz, n1, ncol), ma, tc, ts)

    kb = _divisor_tile(n1, 8, 1)
    out = pl.pallas_call(
        _fft_cols_kernel,
        grid=(bsz, n1 // kb),
        in_specs=[
            pl.BlockSpec((None, kb, n2, width), lambda b, k: (b, k, 0, 0)),
            pl.BlockSpec((None, kb, n2, width), lambda b, k: (b, k, 0, 0)),
            pl.BlockSpec((n2, 2 * n2), lambda b, k: (0, 0)),
        ],
        out_specs=pl.BlockSpec((None, n2, kb * width), lambda b, k: (b, 0, k)),
        out_shape=jax.ShapeDtypeStruct((bsz, n2, n1 * width), BF16),
        compiler_params=_cparams("parallel", "parallel"),
    )(yr.reshape(bsz, n1, n2, width), yi.reshape(bsz, n1, n2, width), mb)
    return out.reshape(bsz, length, width)


def _dense_ffn_kernel(h_ref, mod_ref, wg_ref, wu_ref, wd_ref, o_ref, n_sc, acc_sc):
    f = pl.program_id(2)

    @pl.when(f == 0)
    def _():
        n_sc[...] = _modulate(h_ref[...], mod_ref[3:4, :], mod_ref[4:5, :]).astype(BF16)
        acc_sc[...] = jnp.zeros_like(acc_sc)

    n = n_sc[...]
    a = jax.nn.silu(_dot(n, wg_ref[...])) * _dot(n, wu_ref[...])
    acc_sc[...] += _dot(a.astype(BF16), wd_ref[...])

    @pl.when(f == pl.num_programs(2) - 1)
    def _():
        o_ref[...] = h_ref[...] + mod_ref[5:6, :] * acc_sc[...]


def _divisor_tile(n, target, unit=LANES):
    best = None
    for t in range(unit, min(n, target) + 1, unit):
        if n % t == 0:
            best = t
    assert best is not None
    return best


def _dense_ffn(h, mod, wg, wu, wd):
    bsz, length, d = h.shape
    ff = wg.shape[1]
    tm = _row_tile(length, 1024)
    tf = _divisor_tile(ff, 1408)
    return pl.pallas_call(
        _dense_ffn_kernel,
        grid=(bsz, length // tm, ff // tf),
        in_specs=[
            pl.BlockSpec((None, tm, d), lambda i, j, f: (i, j, 0)),
            pl.BlockSpec((None, 8, d), lambda i, j, f: (i, 0, 0)),
            pl.BlockSpec((d, tf), lambda i, j, f: (0, f)),
            pl.BlockSpec((d, tf), lambda i, j, f: (0, f)),
            pl.BlockSpec((tf, d), lambda i, j, f: (f, 0)),
        ],
        out_specs=pl.BlockSpec((None, tm, d), lambda i, j, f: (i, j, 0)),
        out_shape=jax.ShapeDtypeStruct((bsz, length, d), F32),
        scratch_shapes=[pltpu.VMEM((tm, d), BF16), pltpu.VMEM((tm, d), F32)],
        compiler_params=_cparams("parallel", "parallel", "arbitrary"),
    )(h, mod, wg, wu, wd)


def _pair_swap(x):
    width = x.shape[-1]
    lane = lax.broadcasted_iota(jnp.int32, x.shape, x.ndim - 1)
    nxt = pltpu.roll(x, width - 1, x.ndim - 1)
    prv = pltpu.roll(x, 1, x.ndim - 1)
    return jnp.where(lane % 2 == 0, nxt, prv)


def _rope(x, cos, sin):
    reps = x.shape[1] // LANES
    cos = jnp.concatenate([cos] * reps, axis=1) if reps > 1 else cos
    sin = jnp.concatenate([sin] * reps, axis=1) if reps > 1 else sin
    return x * cos + _pair_swap(x) * sin


def _cd_in_kernel(h_ref, mod_ref, w_ref, gm_ref, qg_ref, kg_ref, cos_ref, sin_ref,
                  cq_ref, ck_ref, cv_ref, dq_ref, dk_ref, dv_ref, *, rope):
    n = _modulate(h_ref[...], mod_ref[0:1, :], mod_ref[1:2, :])
    z = _dot(n.astype(BF16), w_ref[...])
    qw, kw, dw = cq_ref.shape[1], ck_ref.shape[1], dq_ref.shape[1]
    o1, o2, o3, o4, o5 = qw, qw + kw, qw + 2 * kw, qw + 2 * kw + dw, qw + 2 * kw + 2 * dw
    scale = HEAD_DIM ** -0.5

    zq = z[:, :o1]
    blocks = []
    for h in range(C_HEADS):
        t = zq[:, h * LANES:(h + 1) * LANES]
        ms = jnp.sum(t * t, axis=-1, keepdims=True) * (1.0 / HEAD_DIM)
        blocks.append(t * lax.rsqrt(ms + EPS))
    cq = jnp.concatenate(blocks, axis=1) * qg_ref[...]
    zk = z[:, o1:o2]
    ck = zk * lax.rsqrt(_dot((zk * zk).astype(BF16), gm_ref[...]) + EPS) * kg_ref[...]
    if rope:
        cq = _rope(cq, cos_ref[...], sin_ref[...])
        ck = _rope(ck, cos_ref[...], sin_ref[...])
    scale = scale * LOG2E
    cq_ref[...] = (cq * scale).astype(BF16)
    ck_ref[...] = ck.astype(BF16)
    cv_ref[...] = z[:, o2:o3].astype(BF16)
    dq_ref[...] = (z[:, o3:o4] * scale).astype(BF16)
    dk_ref[...] = z[:, o4:o5].astype(BF16)
    dv_ref[...] = z[:, o5:].astype(BF16)


def _project_cd(h, mod, p, rope):
    bsz, length, d = h.shape
    qw, kw, dw = C_HEADS * LANES, C_KV_HEADS * HEAD_DIM, D_HEADS * HEAD_DIM
    tm = _row_tile(length, 512)
    const = lambda i, j: (0, 0)
    tok = lambda w: pl.BlockSpec((None, tm, w), lambda i, j: (i, j, 0))
    widths = (qw, kw, kw, dw, dw, dw)
    return pl.pallas_call(
        functools.partial(_cd_in_kernel, rope=rope),
        grid=(bsz, length // tm),
        in_specs=[
            tok(d),
            pl.BlockSpec((None, 8, d), lambda i, j: (i, 0, 0)),
            pl.BlockSpec(p["w_in"].shape, const),
            pl.BlockSpec(p["gmean"].shape, const),
            pl.BlockSpec(p["q_gain"].shape, const),
            pl.BlockSpec(p["k_gain"].shape, const),
            pl.BlockSpec((tm, LANES), lambda i, j: (j, 0)),
            pl.BlockSpec((tm, LANES), lambda i, j: (j, 0)),
        ],
        out_specs=[tok(w) for w in widths],
        out_shape=[jax.ShapeDtypeStruct((bsz, length, w), BF16) for w in widths],
        compiler_params=_cparams("parallel", "parallel"),
    )(h, mod, p["w_in"], p["gmean"], p["q_gain"], p["k_gain"], p["cos"], p["sin"])


def _gqa_kernel(q_ref, kt_ref, v_ref, o_ref):
    tq = q_ref.shape[0]
    lane = lax.broadcasted_iota(jnp.int32, (tq, LANES), 1)

    def attend(h):
        s = _dot(q_ref[:, h * LANES:(h + 1) * LANES], kt_ref[...])
        p = jnp.exp2(s - jnp.max(s, axis=-1, keepdims=True)).astype(BF16)
        o = _dot(p, v_ref[...])
        return o[:, :LANES] / o[:, LANES:LANES + 1]

    for pair in range(C_HEADS // 2):
        a, b = attend(2 * pair), attend(2 * pair + 1)
        if (2 * pair) // C_GROUP == 0:
            blk = jnp.where(lane < HEAD_DIM, a, pltpu.roll(b, HEAD_DIM, 1))
        else:
            blk = jnp.where(lane < HEAD_DIM, pltpu.roll(a, HEAD_DIM, 1), b)
        o_ref[:, pair * LANES:(pair + 1) * LANES] = blk.astype(BF16)


def _gqa(q_ext, k_all, v_all):
    bsz, length, qw = q_ext.shape
    lk = k_all.shape[1]
    assert C_KV_HEADS * HEAD_DIM == LANES and C_KV_HEADS == 2
    kt = jnp.swapaxes(k_all, 1, 2)
    v_ext = jnp.concatenate([v_all, jnp.ones_like(v_all)], axis=2)
    tq = _row_tile(length, 256)
    ow = C_HEADS * HEAD_DIM
    return pl.pallas_call(
        _gqa_kernel,
        grid=(bsz, length // tq),
        in_specs=[
            pl.BlockSpec((None, tq, qw), lambda i, j: (i, j, 0)),
            pl.BlockSpec((None, LANES, lk), lambda i, j: (i, 0, 0)),
            pl.BlockSpec((None, lk, 2 * LANES), lambda i, j: (i, 0, 0)),
        ],
        out_specs=pl.BlockSpec((None, tq, ow), lambda i, j: (i, j, 0)),
        out_shape=jax.ShapeDtypeStruct((bsz, length, ow), BF16),
        compiler_params=_cparams("parallel", "parallel"),
    )(q_ext, kt, v_ext)


def _na_block_start(qb, rows):
    return jnp.clip(qb * NA_QROWS - NA_WIN_H // 2, 0, rows - NA_KROWS)


def _na_bias_tables(rel_bias, rows):
    nblk = rows // NA_QROWS
    n_dr, n_dc = 2 * NA_WIN_H - 1, 2 * NA_WIN_W - 1
    i = np.arange(NA_QROWS)[:, None]
    a = np.arange(NA_KROWS)[None, :]
    j = np.arange(GRID_W)[:, None]
    kc = np.arange(GRID_W)[None, :]
    col_start = np.clip(j - NA_WIN_W // 2, 0, GRID_W - NA_WIN_W)
    valid_col = (kc >= col_start) & (kc < col_start + NA_WIN_W)
    dc = np.clip(kc - j + (NA_WIN_W - 1), 0, n_dc - 1)
    onehot_c = (dc[:, :, None] == np.arange(n_dc)).astype(np.float32)
    onehot_r, valid = [], []
    for qb in (0, 1, nblk - 1):
        r = qb * NA_QROWS + i
        r0 = np.clip(r - NA_WIN_H // 2, 0, rows - NA_WIN_H)
        kr = int(np.clip(qb * NA_QROWS - NA_WIN_H // 2, 0, rows - NA_KROWS)) + a
        valid_row = (kr >= r0) & (kr < r0 + NA_WIN_H)
        dr = np.clip(kr - r + (NA_WIN_H - 1), 0, n_dr - 1)
        onehot_r.append((dr[:, :, None] == np.arange(n_dr)).astype(np.float32))
        valid.append(valid_row[:, None, :, None] & valid_col[None, :, None, :])
    onehot_r = jnp.asarray(np.stack(onehot_r))
    valid = np.stack(valid)
    hp = lax.Precision.HIGHEST
    by_row = jnp.einsum("hrc,ziar->zhiac", rel_bias.astype(F32), onehot_r, precision=hp)
    table = jnp.einsum("zhiac,jkc->zhijak", by_row, jnp.asarray(onehot_c), precision=hp)
    table = jnp.where(valid[:, None], table * LOG2E, MASK_VALUE)
    return table.reshape(3 * rel_bias.shape[0], NA_QROWS * GRID_W, NA_KROWS * GRID_W)


def _na_kernel(q_ref, k_ref, v_ref, kc_ref, vc_ref, bias_ref, o_ref, *, rows):
    qb = pl.program_id(1)
    nk = NA_KROWS * GRID_W
    start = pl.multiple_of(_na_block_start(qb, rows) * GRID_W, GRID_W)
    k_all = jnp.concatenate([k_ref[pl.ds(start, nk), :], kc_ref[...]], axis=0)
    v_all = jnp.concatenate([v_ref[pl.ds(start, nk), :], vc_ref[...]], axis=0)
    for h in range(D_HEADS):
        cols = slice(h * HEAD_DIM, (h + 1) * HEAD_DIM)
        s = _dot_nt(q_ref[:, cols], k_all[:, cols])
        s = jnp.concatenate([s[:, :nk] + bias_ref[h], s[:, nk:]], axis=1)
        p = jnp.exp2(s - jnp.max(s, axis=-1, keepdims=True))
        l = jnp.sum(p, axis=-1, keepdims=True)
        o_ref[:, cols] = (_dot(p.astype(BF16), v_all[:, cols]) / l).astype(BF16)


def _neighbourhood(dq, dk, dv, dk_c, dv_c, bias):
    bsz, length, w = dq.shape
    rows = length // GRID_W
    assert rows % NA_QROWS == 0 and rows >= NA_KROWS
    nblk = rows // NA_QROWS
    tq = NA_QROWS * GRID_W
    cl = dk_c.shape[1]

    def bias_class(i, j):
        cls = jnp.where(j == 0, 0, jnp.where(j == nblk - 1, 2, 1))
        return (cls, 0, 0)

    return pl.pallas_call(
        functools.partial(_na_kernel, rows=rows),
        grid=(bsz, nblk),
        in_specs=[
            pl.BlockSpec((None, tq, w), lambda i, j: (i, j, 0)),
            pl.BlockSpec((None, length, w), lambda i, j: (i, 0, 0)),
            pl.BlockSpec((None, length, w), lambda i, j: (i, 0, 0)),
            pl.BlockSpec((None, cl, w), lambda i, j: (i, 0, 0)),
            pl.BlockSpec((None, cl, w), lambda i, j: (i, 0, 0)),
            pl.BlockSpec((D_HEADS, tq, NA_KROWS * GRID_W), bias_class),
        ],
        out_specs=pl.BlockSpec((None, tq, w), lambda i, j: (i, j, 0)),
        out_shape=jax.ShapeDtypeStruct((bsz, length, w), BF16),
        compiler_params=_cparams("parallel", "arbitrary"),
    )(dq, dk, dv, dk_c, dv_c, bias)


def _router_kernel(h_ref, mod_ref, wr_ref, n_ref, sel_ref):
    n = _modulate(h_ref[...], mod_ref[3:4, :], mod_ref[4:5, :])
    n_hi = n.astype(BF16)
    n_lo = (n - n_hi.astype(F32)).astype(BF16)
    w = wr_ref[...]
    w_hi = w.astype(BF16)
    w_lo = (w - w_hi.astype(F32)).astype(BF16)
    logits = _dot(n_hi, w_hi) + (_dot(n_lo, w_hi) + _dot(n_hi, w_lo))
    lane = lax.broadcasted_iota(jnp.int32, logits.shape, 1)
    logits = jnp.where(lane < N_EXPERTS, logits, -jnp.inf)
    m1 = jnp.max(logits, axis=-1, keepdims=True)
    i1 = jnp.min(jnp.where(logits == m1, lane, LANES), axis=-1, keepdims=True)
    rest = jnp.where(lane == i1, -jnp.inf, logits)
    m2 = jnp.max(rest, axis=-1, keepdims=True)
    i2 = jnp.min(jnp.where(rest == m2, lane, LANES), axis=-1, keepdims=True)
    e2 = jnp.exp(m2 - m1)
    w1 = 1.0 / (1.0 + e2)
    w2 = e2 / (1.0 + e2)
    sel = jnp.where(lane == 0, i1.astype(F32), jnp.where(lane == 1, i2.astype(F32),
                    jnp.where(lane == 2, w1, jnp.where(lane == 3, w2, 0.0))))
    sel_ref[...] = sel
    n_ref[...] = n_hi


def _router(h, mod, w_router_padded):
    bsz, length, d = h.shape
    tm = _row_tile(length, 1024)
    return pl.pallas_call(
        _router_kernel,
        grid=(bsz, length // tm),
        in_specs=[
            pl.BlockSpec((None, tm, d), lambda i, j: (i, j, 0)),
            pl.BlockSpec((None, 8, d), lambda i, j: (i, 0, 0)),
            pl.BlockSpec((d, LANES), lambda i, j: (0, 0)),
        ],
        out_specs=[
            pl.BlockSpec((None, tm, d), lambda i, j: (i, j, 0)),
            pl.BlockSpec((None, tm, LANES), lambda i, j: (i, j, 0)),
        ],
        out_shape=[
            jax.ShapeDtypeStruct((bsz, length, d), BF16),
            jax.ShapeDtypeStruct((bsz, length, LANES), F32),
        ],
        compiler_params=_cparams("parallel", "parallel"),
    )(h, mod, w_router_padded)


def _moe_dispatch_kernel(pos_ref, w_ref, n_ref, xs_ref, gs_ref):
    nrow = xs_ref.shape[0]
    base = pl.program_id(1) * nrow
    row = base + lax.broadcasted_iota(jnp.int32, (nrow, 1), 0)
    eq0 = pos_ref[0:1, :] == row
    eq1 = pos_ref[1:2, :] == row
    perm = jnp.where(eq0, 1.0, jnp.where(eq1, 1.0, 0.0)).astype(BF16)
    xs_ref[...] = _dot(perm, n_ref[...]).astype(BF16)
    gate = jnp.where(eq0, w_ref[0:1, :], jnp.where(eq1, w_ref[1:2, :], 0.0))
    gs_ref[...] = jnp.sum(gate, axis=-1, keepdims=True)


def _moe_expert_kernel(blk_ref, exp_ref, nused_ref, xs_ref, gs_ref, wg_ref, wu_ref, wd_ref, ys_ref):
    i = pl.program_id(0)

    @pl.when(i < nused_ref[0])
    def _():
        x = xs_ref[...]
        ff = wg_ref.shape[1]
        y = jnp.zeros((x.shape[0], wd_ref.shape[1]), F32)
        for f in range(0, ff, MOE_FF_CHUNK):
            cols = slice(f, min(f + MOE_FF_CHUNK, ff))
            a = jax.nn.silu(_dot(x, wg_ref[:, cols])) * _dot(x, wu_ref[:, cols])
            y = y + _dot(a.astype(BF16), wd_ref[cols, :])
        ys_ref[...] = (y * gs_ref[...]).astype(BF16)

    @pl.when(i >= nused_ref[0])
    def _():
        ys_ref[...] = jnp.zeros_like(ys_ref)


def _moe_combine_kernel(pos_ref, ys_ref, h_ref, mod_ref, fg_ref, o_ref):
    ntok, nrow = pos_ref.shape[0], ys_ref.shape[0]
    row = lax.broadcasted_iota(jnp.int32, (ntok, nrow), 1)
    perm = jnp.where(pos_ref[:, 0:1] == row, 1.0, jnp.where(pos_ref[:, 1:2] == row, 1.0, 0.0)).astype(BF16)
    y = h_ref[...] + mod_ref[5:6, :] * _dot(perm, ys_ref[...])
    ms = jnp.mean(y * y, axis=-1, keepdims=True)
    o_ref[...] = y * lax.rsqrt(ms + EPS) * fg_ref[...]


def _moe_plan(sel, n_tiles, tile_tokens, rows_per_tile):
    blocks_per_tile = rows_per_tile // MOE_BLOCK
    experts = sel[:, :TOP_K].astype(jnp.int32).reshape(n_tiles, tile_tokens * TOP_K)
    weights = sel[:, TOP_K:2 * TOP_K].reshape(n_tiles, tile_tokens, TOP_K)
    onehot = (experts[:, :, None] == jnp.arange(N_EXPERTS, dtype=jnp.int32)).astype(jnp.int32)
    csum = jnp.cumsum(onehot, axis=1)
    rank = jnp.sum((csum - onehot) * onehot, axis=-1)
    count = csum[:, -1, :]
    nblk = (count + MOE_BLOCK - 1) // MOE_BLOCK
    blk_end = jnp.cumsum(nblk, axis=1)
    blk_off = blk_end - nblk
    pos = jnp.sum(onehot * blk_off[:, None, :], axis=-1) * MOE_BLOCK + rank
    pos = pos.reshape(n_tiles, tile_tokens, TOP_K)

    b = jnp.arange(blocks_per_tile, dtype=jnp.int32)
    blk_expert = jnp.sum((blk_end[:, None, :] <= b[None, :, None]).astype(jnp.int32), axis=-1)
    flat_expert = blk_expert.reshape(-1)
    n_blocks = flat_expert.shape[0]
    order = jnp.argsort(flat_expert * n_blocks + jnp.arange(n_blocks, dtype=jnp.int32)).astype(jnp.int32)
    n_used = jnp.sum((flat_expert < N_EXPERTS).astype(jnp.int32))
    last_used = order[n_used - 1]
    blk_exp = jnp.minimum(flat_expert[order], flat_expert[last_used])
    return pos, weights, order, blk_exp, n_used.reshape(1)


def _moe_final(n2, sel, h, mod, final_gain, wg, wu, wd):
    bsz, length, d = h.shape
    n_exp, _, ff = wg.shape
    tokens = bsz * length
    tt = _row_tile(length, MOE_TOKENS)
    n_tiles = tokens // tt
    rows = TOP_K * tt + N_EXPERTS * MOE_BLOCK
    drows = _divisor_tile(rows, MOE_DISPATCH_ROWS, MOE_BLOCK)
    ctok = _row_tile(tt, MOE_COMBINE_TOKENS)
    pos, weights, blk_ids, blk_exp, n_used = _moe_plan(sel.reshape(tokens, LANES), n_tiles, tt, rows)

    pos_rows = jnp.swapaxes(pos, 1, 2)
    w_rows = jnp.swapaxes(weights, 1, 2)
    xs, gs = pl.pallas_call(
        _moe_dispatch_kernel,
        grid=(n_tiles, rows // drows),
        in_specs=[
            pl.BlockSpec((None, TOP_K, tt), lambda t, r: (t, 0, 0)),
            pl.BlockSpec((None, TOP_K, tt), lambda t, r: (t, 0, 0)),
            pl.BlockSpec((tt, d), lambda t, r: (t, 0)),
        ],
        out_specs=[
            pl.BlockSpec((None, drows, d), lambda t, r: (t, r, 0)),
            pl.BlockSpec((None, drows, 1), lambda t, r: (t, r, 0)),
        ],
        out_shape=[
            jax.ShapeDtypeStruct((n_tiles, rows, d), BF16),
            jax.ShapeDtypeStruct((n_tiles, rows, 1), F32),
        ],
        compiler_params=_cparams("parallel", "parallel"),
    )(pos_rows, w_rows, n2.reshape(tokens, d))

    n_blocks = n_tiles * (rows // MOE_BLOCK)
    resident = pl.Buffered(1)
    ys = pl.pallas_call(
        _moe_expert_kernel,
        grid_spec=pltpu.PrefetchScalarGridSpec(
            num_scalar_prefetch=3,
            grid=(n_blocks,),
            in_specs=[
                pl.BlockSpec((MOE_BLOCK, d), lambda i, blk, ex, nu: (blk[i], 0)),
                pl.BlockSpec((MOE_BLOCK, 1), lambda i, blk, ex, nu: (blk[i], 0)),
                pl.BlockSpec((None, d, ff), lambda i, blk, ex, nu: (ex[i], 0, 0), pipeline_mode=resident),
                pl.BlockSpec((None, d, ff), lambda i, blk, ex, nu: (ex[i], 0, 0), pipeline_mode=resident),
                pl.BlockSpec((None, ff, d), lambda i, blk, ex, nu: (ex[i], 0, 0), pipeline_mode=resident),
            ],
            out_specs=pl.BlockSpec((MOE_BLOCK, d), lambda i, blk, ex, nu: (blk[i], 0)),
        ),
        out_shape=jax.ShapeDtypeStruct((n_tiles * rows, d), BF16),
        compiler_params=_cparams("arbitrary"),
    )(blk_ids, blk_exp, n_used, xs.reshape(n_tiles * rows, d), gs.reshape(n_tiles * rows, 1), wg, wu, wd)

    tiles_per_seq = length // tt
    out = pl.pallas_call(
        _moe_combine_kernel,
        grid=(n_tiles, tt // ctok),
        in_specs=[
            pl.BlockSpec((None, ctok, TOP_K), lambda t, c: (t, c, 0)),
            pl.BlockSpec((None, rows, d), lambda t, c: (t, 0, 0)),
            pl.BlockSpec((None, ctok, d), lambda t, c: (t, c, 0)),
            pl.BlockSpec((None, 8, d), lambda t, c: (t // tiles_per_seq, 0, 0)),
            pl.BlockSpec((1, d), lambda t, c: (0, 0)),
        ],
        out_specs=pl.BlockSpec((None, ctok, d), lambda t, c: (t, c, 0)),
        out_shape=jax.ShapeDtypeStruct((n_tiles, tt, d), F32),
        compiler_params=_cparams("parallel", "parallel"),
    )(pos, ys.reshape(n_tiles, rows, d), h.reshape(n_tiles, tt, d), mod, final_gain)
    return out.reshape(bsz, length, d)


def _group_mean_matrix(width, group):
    return jnp.asarray(np.kron(np.eye(width // group), np.full((group, group), 1.0 / group)), BF16)


def _rope_tables(length):
    t = jnp.arange(length, dtype=jnp.int32)
    row = (t // GRID_W).astype(F32)
    col = (t % GRID_W).astype(F32)
    n_axis = HEAD_DIM // 4
    inv_freq = ROPE_THETA ** (-jnp.arange(n_axis, dtype=F32) / n_axis)
    ang = jnp.concatenate([row[:, None] * inv_freq, col[:, None] * inv_freq], axis=-1)
    cos = jnp.repeat(jnp.cos(ang), 2, axis=-1)
    sin = jnp.repeat(jnp.sin(ang), 2, axis=-1)
    sign = jnp.tile(jnp.asarray([-1.0, 1.0], F32), HEAD_DIM // 2)
    reps = LANES // HEAD_DIM
    return jnp.tile(cos, (1, reps)), jnp.tile(sin * sign, (1, reps))


def _spread_q_heads(w_q):
    d = w_q.shape[0]
    w = w_q.reshape(d, C_HEADS, HEAD_DIM)
    zeros = jnp.zeros_like(w)
    first_kv = (jnp.arange(C_HEADS) // C_GROUP == 0)[None, :, None]
    lo = jnp.where(first_kv, w, zeros)
    hi = jnp.where(first_kv, zeros, w)
    return jnp.concatenate([lo, hi], axis=2).reshape(d, C_HEADS * LANES)


def kernel(x, c, ctx, c_ctx, w_ada, b_ada, w_in_ab, v_gain, w_spatial, b_spatial, w_out_ab,
           w_gate_dense, w_up_dense, w_down_dense, w_in_cd, q_gain, k_gain, rel_bias, w_out_cd,
           w_router, w_gate_moe, w_up_moe, w_down_moe, final_gain):
    bsz, length, d = x.shape
    depth = w_ada.shape[0]
    assert depth == 2 and bsz <= 8
    aw = v_gain.shape[1]
    group_dim = aw // A_GROUPS
    bgd = (w_in_ab.shape[2] - 2 * aw) // B_GROUPS

    cond = jnp.zeros((16, d), F32).at[:bsz].set(c).at[8].set(c_ctx)
    ada = _adaln(cond, w_ada, b_ada).reshape(depth, 16, 6, d)
    pad = jnp.zeros((depth, 16, 2, d), F32)
    ada = jnp.concatenate([ada, pad], axis=2)
    mod_lat = [ada[i, :bsz] for i in range(depth)]
    mod_ctx = [jnp.broadcast_to(ada[i, 8], (bsz, 8, d)) for i in range(depth)]

    cmat, smat = _dft_tables(bgd, bgd ** -0.5)
    p_ab = dict(
        w_in=w_in_ab[0].astype(BF16),
        v_gain=v_gain[0].reshape(1, aw),
        gmean=_group_mean_matrix(aw, group_dim),
        w_sp=w_spatial[0].transpose(1, 0, 2).reshape(CHUNK, A_GROUPS * CHUNK).astype(BF16),
        b_sp=jnp.repeat(b_spatial[0].T, group_dim, axis=1),
        cmat=cmat, smat=smat,
        w_out=w_out_ab[0].astype(BF16),
    )
    wg0, wu0, wd0 = (w.astype(BF16) for w in (w_gate_dense[0], w_up_dense[0], w_down_dense[0]))
    h = _mix_ab(x, mod_lat[0], p_ab)
    hc = _mix_ab(ctx, mod_ctx[0], p_ab)
    h = _dense_ffn(h, mod_lat[0], wg0, wu0, wd0)
    hc = _dense_ffn(hc, mod_ctx[0], wg0, wu0, wd0)

    cos, sin = _rope_tables(length)
    qw = C_HEADS * HEAD_DIM
    kw = C_KV_HEADS * HEAD_DIM
    w_cd = w_in_cd[0]
    p_cd = dict(
        w_in=jnp.concatenate([_spread_q_heads(w_cd[:, :qw]), w_cd[:, qw:]], axis=1).astype(BF16),
        gmean=_group_mean_matrix(kw, HEAD_DIM),
        q_gain=jnp.tile(q_gain[0], C_HEADS * LANES // HEAD_DIM).reshape(1, C_HEADS * LANES),
        k_gain=jnp.tile(k_gain[0], C_KV_HEADS).reshape(1, kw),
        cos=cos, sin=sin,
    )
    cq, ck, cv, dq, dk, dv = _project_cd(h, mod_lat[1], p_cd, rope=True)
    p_cd_ctx = dict(p_cd, cos=cos[:hc.shape[1]], sin=sin[:hc.shape[1]])
    _, ck_c, cv_c, _, dk_c, dv_c = _project_cd(hc, mod_ctx[1], p_cd_ctx, rope=False)
    o_c = _gqa(cq, jnp.concatenate([ck_c, ck], axis=1), jnp.concatenate([cv_c, cv], axis=1))
    o_d = _neighbourhood(dq, dk, dv, dk_c, dv_c, _na_bias_tables(rel_bias[0], length // GRID_W))
    h = _proj_residual(o_c, o_d, w_out_cd[0].astype(BF16), h, mod_lat[1], 2)

    wr = jnp.zeros((d, LANES), F32).at[:, :N_EXPERTS].set(w_router[0])
    n2, sel = _router(h, mod_lat[1], wr)
    return _moe_final(n2, sel, h, mod_lat[1], final_gain.reshape(1, d),
                      w_gate_moe[0].astype(BF16), w_up_moe[0].astype(BF16), w_down_moe[0].astype(BF16))
```

```python
import functools
import math

import numpy as np
import jax
import jax.numpy as jnp
from jax import lax
from jax.experimental import pallas as pl
from jax.experimental.pallas import tpu as pltpu

F32 = jnp.float32
BF16 = jnp.bfloat16

GRID_W = 64
EPS = 1e-6
CHUNK = 128
A_GROUPS = 8
B_GROUPS = 4
HEAD_DIM = 64
C_HEADS = 8
C_KV_HEADS = 2
C_GROUP = C_HEADS // C_KV_HEADS
D_HEADS = 8
NA_WIN_H = 8
NA_WIN_W = 16
ROPE_THETA = 10000.0
N_EXPERTS = 8
TOP_K = 2

LANES = 128
NA_QROWS = 4
NA_KROWS = NA_QROWS + NA_WIN_H
MASK_VALUE = -1e30
LOG2E = 1.4426950408889634
VMEM_LIMIT = 56 * 1024 * 1024

MOE_TOKENS = 1024
MOE_PIECE = 64
MOE_GROUP = 4
MOE_DISPATCH_ROWS = 1280
MOE_COMBINE_TOKENS = 512
MOE_FF_CHUNK = 512


def _cparams(*sem):
    return pltpu.CompilerParams(dimension_semantics=sem, vmem_limit_bytes=VMEM_LIMIT)


def _modulate(h, shift, scale):
    ms = jnp.mean(h * h, axis=-1, keepdims=True)
    return h * lax.rsqrt(ms + EPS) * (1.0 + scale) + shift


def _dot(a, b):
    return jnp.dot(a, b, preferred_element_type=F32)


def _dot_nt(a, b):
    return lax.dot_general(a, b, (((1,), (1,)), ((), ())), preferred_element_type=F32)


def _row_tile(length, target):
    t = min(length, target)
    assert length % t == 0
    return t


def _adaln_kernel(c_ref, w_ref, b_ref, o_ref):
    s = jax.nn.silu(c_ref[...]).astype(BF16)
    o_ref[...] = _dot(s, w_ref[...].astype(BF16)) + b_ref[...]


def _adaln(cond, w_ada, b_ada):
    depth, d, n = w_ada.shape
    r = cond.shape[0]
    tn = 512
    return pl.pallas_call(
        _adaln_kernel,
        grid=(depth, n // tn),
        in_specs=[
            pl.BlockSpec((r, d), lambda i, j: (0, 0)),
            pl.BlockSpec((None, d, tn), lambda i, j: (i, 0, j)),
            pl.BlockSpec((None, 1, tn), lambda i, j: (i, 0, j)),
        ],
        out_specs=pl.BlockSpec((None, r, tn), lambda i, j: (i, 0, j)),
        out_shape=jax.ShapeDtypeStruct((depth, r, n), F32),
        compiler_params=_cparams("parallel", "parallel"),
    )(cond, w_ada, b_ada.reshape(depth, 1, n))


def _ab_in_kernel(h_ref, mod_ref, w_ref, vg_ref, gm_ref, wsp_ref, bsp_ref, cm_ref, sm_ref,
                  a_ref, fc_ref, fs_ref, *, aw):
    tm = h_ref.shape[0]
    n = _modulate(h_ref[...], mod_ref[0:1, :], mod_ref[1:2, :])
    z = _dot(n.astype(BF16), w_ref[...])
    u = jax.nn.gelu(z[:, :aw])
    v = jax.nn.gelu(z[:, aw:2 * aw])
    fb = z[:, 2 * aw:].astype(BF16)
    ms = _dot((v * v).astype(BF16), gm_ref[...])
    vn = v * lax.rsqrt(ms + EPS) * vg_ref[...]
    group_dim = aw // A_GROUPS
    lane_group = lax.broadcasted_iota(jnp.int32, (CHUNK, aw), 1) // group_dim
    for c in range(tm // CHUNK):
        rows = slice(c * CHUNK, (c + 1) * CHUNK)
        vc = vn[rows, :]
        stack = jnp.concatenate(
            [jnp.where(lane_group == g, vc, 0.0).astype(BF16) for g in range(A_GROUPS)], axis=0)
        s = _dot(wsp_ref[...], stack) + bsp_ref[...]
        a_ref[rows, :] = (u[rows, :] * s).astype(BF16)
    bw = fb.shape[1] // B_GROUPS
    for g in range(B_GROUPS):
        cols = slice(g * bw, (g + 1) * bw)
        fc_ref[:, cols] = _dot(fb[:, cols], cm_ref[...]).astype(BF16)
        fs_ref[:, cols] = _dot(fb[:, cols], sm_ref[...]).astype(BF16)


def _seq_dft_kernel(c_ref, s_ref, fc_ref, fs_ref, o_ref):
    o_ref[...] = (_dot(c_ref[...], fc_ref[...]) - _dot(s_ref[...], fs_ref[...])).astype(BF16)


def _proj_residual_kernel(a_ref, b_ref, w_ref, h_ref, mod_ref, o_ref, *, gate_row):
    ka = a_ref.shape[1]
    y = _dot(a_ref[...], w_ref[:ka, :]) + _dot(b_ref[...], w_ref[ka:, :])
    o_ref[...] = h_ref[...] + mod_ref[gate_row:gate_row + 1, :] * y


def _proj_residual(a, b, w, h, mod, gate_row):
    bsz, length, d = h.shape
    tm = _row_tile(length, 1024)
    ka, kb = a.shape[2], b.shape[2]
    return pl.pallas_call(
        functools.partial(_proj_residual_kernel, gate_row=gate_row),
        grid=(bsz, length // tm),
        in_specs=[
            pl.BlockSpec((None, tm, ka), lambda i, j: (i, j, 0)),
            pl.BlockSpec((None, tm, kb), lambda i, j: (i, j, 0)),
            pl.BlockSpec((ka + kb, d), lambda i, j: (0, 0)),
            pl.BlockSpec((None, tm, d), lambda i, j: (i, j, 0)),
            pl.BlockSpec((None, 8, d), lambda i, j: (i, 0, 0)),
        ],
        out_specs=pl.BlockSpec((None, tm, d), lambda i, j: (i, j, 0)),
        out_shape=jax.ShapeDtypeStruct((bsz, length, d), F32),
        compiler_params=_cparams("parallel", "parallel"),
    )(a, b, w, h, mod)


def _dft_tables(n, scale):
    k = jnp.arange(n, dtype=jnp.int32)
    ang = ((k[:, None] * k[None, :]) % n).astype(F32) * (2.0 * math.pi / n)
    return (jnp.cos(ang) * scale).astype(BF16), (jnp.sin(ang) * scale).astype(BF16)


def _mix_ab(h, mod, p):
    bsz, length, d = h.shape
    aw = p["v_gain"].shape[1]
    bw_total = p["w_in"].shape[1] - 2 * aw
    tm = _row_tile(length, 512)
    const = lambda i, j: (0, 0)
    a_out, fc, fs = pl.pallas_call(
        functools.partial(_ab_in_kernel, aw=aw),
        grid=(bsz, length // tm),
        in_specs=[
            pl.BlockSpec((None, tm, d), lambda i, j: (i, j, 0)),
            pl.BlockSpec((None, 8, d), lambda i, j: (i, 0, 0)),
            pl.BlockSpec(p["w_in"].shape, const),
            pl.BlockSpec(p["v_gain"].shape, const),
            pl.BlockSpec(p["gmean"].shape, const),
            pl.BlockSpec(p["w_sp"].shape, const),
            pl.BlockSpec(p["b_sp"].shape, const),
            pl.BlockSpec(p["cmat"].shape, const),
            pl.BlockSpec(p["smat"].shape, const),
        ],
        out_specs=[
            pl.BlockSpec((None, tm, aw), lambda i, j: (i, j, 0)),
            pl.BlockSpec((None, tm, bw_total), lambda i, j: (i, j, 0)),
            pl.BlockSpec((None, tm, bw_total), lambda i, j: (i, j, 0)),
        ],
        out_shape=[
            jax.ShapeDtypeStruct((bsz, length, aw), BF16),
            jax.ShapeDtypeStruct((bsz, length, bw_total), BF16),
            jax.ShapeDtypeStruct((bsz, length, bw_total), BF16),
        ],
        compiler_params=_cparams("parallel", "parallel"),
    )(h, mod, p["w_in"], p["v_gain"], p["gmean"], p["w_sp"], p["b_sp"], p["cmat"], p["smat"])

    b_out = _seq_dft(fc, fs)
    return _proj_residual(a_out, b_out, p["w_out"], h, mod, 2)


def _seq_dft_direct(fc, fs):
    bsz, length, width = fc.shape
    cl, sl = _dft_tables(length, length ** -0.5)
    tk = _row_tile(length, 512)
    return pl.pallas_call(
        _seq_dft_kernel,
        grid=(length // tk, bsz),
        in_specs=[
            pl.BlockSpec((tk, length), lambda k, b: (k, 0)),
            pl.BlockSpec((tk, length), lambda k, b: (k, 0)),
            pl.BlockSpec((None, length, width), lambda k, b: (b, 0, 0)),
            pl.BlockSpec((None, length, width), lambda k, b: (b, 0, 0)),
        ],
        out_specs=pl.BlockSpec((None, tk, width), lambda k, b: (b, k, 0)),
        out_shape=jax.ShapeDtypeStruct((bsz, length, width), BF16),
        compiler_params=_cparams("parallel", "parallel"),
    )(cl, sl, fc, fs)


def _fft_rows_kernel(fc_ref, fs_ref, ma_ref, tc_ref, ts_ref, yr_ref, yi_ref):
    n1 = fc_ref.shape[0]
    y = _dot(ma_ref[...], jnp.concatenate([fc_ref[...], fs_ref[...]], axis=0))
    yr, yi = y[:n1, :], y[n1:, :]
    c, s = tc_ref[...], ts_ref[...]
    yr_ref[...] = (yr * c + yi * s).astype(BF16)
    yi_ref[...] = (yi * c - yr * s).astype(BF16)


def _fft_cols_kernel(yr_ref, yi_ref, mb_ref, o_ref):
    width = yr_ref.shape[2]
    for j in range(yr_ref.shape[0]):
        y = jnp.concatenate([yr_ref[j], yi_ref[j]], axis=0)
        o_ref[:, j * width:(j + 1) * width] = _dot(mb_ref[...], y).astype(BF16)


def _seq_dft(fc, fs):
    bsz, length, width = fc.shape
    n2 = GRID_W
    n1 = length // n2
    if n1 < 16 or length % n2:
        return _seq_dft_direct(fc, fs)
    ncol = n2 * width

    def angles(a, b, period):
        prod = (jnp.arange(a, dtype=jnp.int32)[:, None] * jnp.arange(b, dtype=jnp.int32)[None, :]) % period
        return prod.astype(F32) * (2.0 * math.pi / period)

    th = angles(n1, n1, n1)
    c1, s1 = jnp.cos(th) * n1 ** -0.5, jnp.sin(th) * n1 ** -0.5
    ma = jnp.concatenate([jnp.concatenate([c1, -s1], axis=1),
                          jnp.concatenate([-s1, -c1], axis=1)], axis=0).astype(BF16)
    ph = angles(n1, n2, length)
    tc = jnp.repeat(jnp.cos(ph), width, axis=1)
    ts = jnp.repeat(jnp.sin(ph), width, axis=1)
    ps = angles(n2, n2, n2)
    mb = (jnp.concatenate([jnp.cos(ps), jnp.sin(ps)], axis=1) * n2 ** -0.5).astype(BF16)

    cols = _divisor_tile(ncol, 4096)
    row_blk = lambda b, j: (b, 0, j)
    yr, yi = pl.pallas_call(
        _fft_rows_kernel,
        grid=(bsz, ncol // cols),
        in_specs=[
            pl.BlockSpec((None, n1, cols), row_blk),
            pl.BlockSpec((None, n1, cols), row_blk),
            pl.BlockSpec((2 * n1, 2 * n1), lambda b, j: (0, 0)),
            pl.BlockSpec((n1, cols), lambda b, j: (0, j)),
            pl.BlockSpec((n1, cols), lambda b, j: (0, j)),
        ],
        out_specs=[pl.BlockSpec((None, n1, cols), row_blk)] * 2,
        out_shape=[jax.ShapeDtypeStruct((bsz, n1, ncol), BF16)] * 2,
        compiler_params=_cparams("parallel", "parallel"),
    )(fc.reshape(bsz, n1, ncol), fs.reshape(bsz, n1, ncol), ma, tc, ts)

    kb = _divisor_tile(n1, 8, 1)
    out = pl.pallas_call(
        _fft_cols_kernel,
        grid=(bsz, n1 // kb),
        in_specs=[
            pl.BlockSpec((None, kb, n2, width), lambda b, k: (b, k, 0, 0)),
            pl.BlockSpec((None, kb, n2, width), lambda b, k: (b, k, 0, 0)),
            pl.BlockSpec((n2, 2 * n2), lambda b, k: (0, 0)),
        ],
        out_specs=pl.BlockSpec((None, n2, kb * width), lambda b, k: (b, 0, k)),
        out_shape=jax.ShapeDtypeStruct((bsz, n2, n1 * width), BF16),
        compiler_params=_cparams("parallel", "parallel"),
    )(yr.reshape(bsz, n1, n2, width), yi.reshape(bsz, n1, n2, width), mb)
    return out.reshape(bsz, length, width)


def _dense_ffn_kernel(h_ref, mod_ref, wg_ref, wu_ref, wd_ref, o_ref, n_sc, acc_sc):
    f = pl.program_id(2)

    @pl.when(f == 0)
    def _():
        n_sc[...] = _modulate(h_ref[...], mod_ref[3:4, :], mod_ref[4:5, :]).astype(BF16)
        acc_sc[...] = jnp.zeros_like(acc_sc)

    n = n_sc[...]
    a = jax.nn.silu(_dot(n, wg_ref[...])) * _dot(n, wu_ref[...])
    acc_sc[...] += _dot(a.astype(BF16), wd_ref[...])

    @pl.when(f == pl.num_programs(2) - 1)
    def _():
        o_ref[...] = h_ref[...] + mod_ref[5:6, :] * acc_sc[...]


def _divisor_tile(n, target, unit=LANES):
    best = None
    for t in range(unit, min(n, target) + 1, unit):
        if n % t == 0:
            best = t
    assert best is not None
    return best


def _dense_ffn(h, mod, wg, wu, wd):
    bsz, length, d = h.shape
    ff = wg.shape[1]
    tm = _row_tile(length, 1024)
    tf = _divisor_tile(ff, 1408)
    return pl.pallas_call(
        _dense_ffn_kernel,
        grid=(bsz, length // tm, ff // tf),
        in_specs=[
            pl.BlockSpec((None, tm, d), lambda i, j, f: (i, j, 0)),
            pl.BlockSpec((None, 8, d), lambda i, j, f: (i, 0, 0)),
            pl.BlockSpec((d, tf), lambda i, j, f: (0, f)),
            pl.BlockSpec((d, tf), lambda i, j, f: (0, f)),
            pl.BlockSpec((tf, d), lambda i, j, f: (f, 0)),
        ],
        out_specs=pl.BlockSpec((None, tm, d), lambda i, j, f: (i, j, 0)),
        out_shape=jax.ShapeDtypeStruct((bsz, length, d), F32),
        scratch_shapes=[pltpu.VMEM((tm, d), BF16), pltpu.VMEM((tm, d), F32)],
        compiler_params=_cparams("parallel", "parallel", "arbitrary"),
    )(h, mod, wg, wu, wd)


def _pair_swap(x):
    width = x.shape[-1]
    lane = lax.broadcasted_iota(jnp.int32, x.shape, x.ndim - 1)
    nxt = pltpu.roll(x, width - 1, x.ndim - 1)
    prv = pltpu.roll(x, 1, x.ndim - 1)
    return jnp.where(lane % 2 == 0, nxt, prv)


def _rope(x, cos, sin):
    reps = x.shape[1] // LANES
    cos = jnp.concatenate([cos] * reps, axis=1) if reps > 1 else cos
    sin = jnp.concatenate([sin] * reps, axis=1) if reps > 1 else sin
    return x * cos + _pair_swap(x) * sin


def _cd_in_kernel(h_ref, mod_ref, w_ref, gm_ref, qg_ref, kg_ref, cos_ref, sin_ref,
                  cq_ref, ck_ref, cv_ref, dq_ref, dk_ref, dv_ref, *, rope):
    n = _modulate(h_ref[...], mod_ref[0:1, :], mod_ref[1:2, :])
    z = _dot(n.astype(BF16), w_ref[...])
    qw, kw, dqw, dw = cq_ref.shape[1], ck_ref.shape[1], dq_ref.shape[1], dk_ref.shape[1]
    o1, o2, o3, o4, o5 = qw, qw + kw, qw + 2 * kw, qw + 2 * kw + dqw, qw + 2 * kw + dqw + dw
    scale = HEAD_DIM ** -0.5

    zq = z[:, :o1]
    blocks = []
    for h in range(C_HEADS):
        t = zq[:, h * LANES:(h + 1) * LANES]
        ms = jnp.sum(t * t, axis=-1, keepdims=True) * (1.0 / HEAD_DIM)
        blocks.append(t * lax.rsqrt(ms + EPS))
    cq = jnp.concatenate(blocks, axis=1) * qg_ref[...]
    zk = z[:, o1:o2]
    ck = zk * lax.rsqrt(_dot((zk * zk).astype(BF16), gm_ref[...]) + EPS) * kg_ref[...]
    if rope:
        cq = _rope(cq, cos_ref[...], sin_ref[...])
        ck = _rope(ck, cos_ref[...], sin_ref[...])
    scale = scale * LOG2E
    cq_ref[...] = (cq * scale).astype(BF16)
    ck_ref[...] = ck.astype(BF16)
    cv_ref[...] = z[:, o2:o3].astype(BF16)
    dq_ref[...] = (z[:, o3:o4] * scale).astype(BF16)
    dk_ref[...] = z[:, o4:o5].astype(BF16)
    dv_ref[...] = z[:, o5:].astype(BF16)


def _project_cd(h, mod, p, rope):
    bsz, length, d = h.shape
    qw, kw, dw = C_HEADS * LANES, C_KV_HEADS * HEAD_DIM, D_HEADS * HEAD_DIM
    tm = _row_tile(length, 512)
    const = lambda i, j: (0, 0)
    tok = lambda w: pl.BlockSpec((None, tm, w), lambda i, j: (i, j, 0))
    widths = (qw, kw, kw, D_HEADS * LANES, dw, dw)
    return pl.pallas_call(
        functools.partial(_cd_in_kernel, rope=rope),
        grid=(bsz, length // tm),
        in_specs=[
            tok(d),
            pl.BlockSpec((None, 8, d), lambda i, j: (i, 0, 0)),
            pl.BlockSpec(p["w_in"].shape, const),
            pl.BlockSpec(p["gmean"].shape, const),
            pl.BlockSpec(p["q_gain"].shape, const),
            pl.BlockSpec(p["k_gain"].shape, const),
            pl.BlockSpec((tm, LANES), lambda i, j: (j, 0)),
            pl.BlockSpec((tm, LANES), lambda i, j: (j, 0)),
        ],
        out_specs=[tok(w) for w in widths],
        out_shape=[jax.ShapeDtypeStruct((bsz, length, w), BF16) for w in widths],
        compiler_params=_cparams("parallel", "parallel"),
    )(h, mod, p["w_in"], p["gmean"], p["q_gain"], p["k_gain"], p["cos"], p["sin"])


def _gqa_kernel(q_ref, kt_ref, v_ref, o_ref):
    tq = q_ref.shape[0]
    lane = lax.broadcasted_iota(jnp.int32, (tq, LANES), 1)

    def attend(h):
        s = _dot(q_ref[:, h * LANES:(h + 1) * LANES], kt_ref[...])
        p = jnp.exp2(s - jnp.max(s, axis=-1, keepdims=True)).astype(BF16)
        o = _dot(p, v_ref[...])
        return o[:, :LANES] / o[:, LANES:LANES + 1]

    for pair in range(C_HEADS // 2):
        a, b = attend(2 * pair), attend(2 * pair + 1)
        if (2 * pair) // C_GROUP == 0:
            blk = jnp.where(lane < HEAD_DIM, a, pltpu.roll(b, HEAD_DIM, 1))
        else:
            blk = jnp.where(lane < HEAD_DIM, pltpu.roll(a, HEAD_DIM, 1), b)
        o_ref[:, pair * LANES:(pair + 1) * LANES] = blk.astype(BF16)


def _gqa(q_ext, k_all, v_all):
    bsz, length, qw = q_ext.shape
    lk = k_all.shape[1]
    assert C_KV_HEADS * HEAD_DIM == LANES and C_KV_HEADS == 2
    kt = jnp.swapaxes(k_all, 1, 2)
    v_ext = jnp.concatenate([v_all, jnp.ones_like(v_all)], axis=2)
    tq = _row_tile(length, 256)
    ow = C_HEADS * HEAD_DIM
    return pl.pallas_call(
        _gqa_kernel,
        grid=(bsz, length // tq),
        in_specs=[
            pl.BlockSpec((None, tq, qw), lambda i, j: (i, j, 0)),
            pl.BlockSpec((None, LANES, lk), lambda i, j: (i, 0, 0)),
            pl.BlockSpec((None, lk, 2 * LANES), lambda i, j: (i, 0, 0)),
        ],
        out_specs=pl.BlockSpec((None, tq, ow), lambda i, j: (i, j, 0)),
        out_shape=jax.ShapeDtypeStruct((bsz, length, ow), BF16),
        compiler_params=_cparams("parallel", "parallel"),
    )(q_ext, kt, v_ext)


def _na_block_start(qb, rows):
    return jnp.clip(qb * NA_QROWS - NA_WIN_H // 2, 0, rows - NA_KROWS)


def _na_bias_tables(rel_bias, rows):
    nblk = rows // NA_QROWS
    n_dr, n_dc = 2 * NA_WIN_H - 1, 2 * NA_WIN_W - 1
    i = np.arange(NA_QROWS)[:, None]
    a = np.arange(NA_KROWS)[None, :]
    j = np.arange(GRID_W)[:, None]
    kc = np.arange(GRID_W)[None, :]
    col_start = np.clip(j - NA_WIN_W // 2, 0, GRID_W - NA_WIN_W)
    valid_col = (kc >= col_start) & (kc < col_start + NA_WIN_W)
    dc = np.clip(kc - j + (NA_WIN_W - 1), 0, n_dc - 1)
    onehot_c = (dc[:, :, None] == np.arange(n_dc)).astype(np.float32)
    onehot_r, valid = [], []
    for qb in (0, 1, nblk - 1):
        r = qb * NA_QROWS + i
        r0 = np.clip(r - NA_WIN_H // 2, 0, rows - NA_WIN_H)
        kr = int(np.clip(qb * NA_QROWS - NA_WIN_H // 2, 0, rows - NA_KROWS)) + a
        valid_row = (kr >= r0) & (kr < r0 + NA_WIN_H)
        dr = np.clip(kr - r + (NA_WIN_H - 1), 0, n_dr - 1)
        onehot_r.append((dr[:, :, None] == np.arange(n_dr)).astype(np.float32))
        valid.append(valid_row[:, None, :, None] & valid_col[None, :, None, :])
    onehot_r = jnp.asarray(np.stack(onehot_r))
    valid = np.stack(valid)
    hp = lax.Precision.HIGHEST
    by_row = jnp.einsum("hrc,ziar->zhiac", rel_bias.astype(F32), onehot_r, precision=hp)
    table = jnp.einsum("zhiac,jkc->zhijak", by_row, jnp.asarray(onehot_c), precision=hp)
    table = jnp.where(valid[:, None], table * LOG2E, MASK_VALUE)
    return table.reshape(3 * rel_bias.shape[0], NA_QROWS * GRID_W, NA_KROWS * GRID_W)


def _na_kernel(q_ref, k_ref, v_ref, kc_ref, vc_ref, bias_ref, o_ref, *, rows):
    qb = pl.program_id(1)
    nk = NA_KROWS * GRID_W
    start = pl.multiple_of(_na_block_start(qb, rows) * GRID_W, GRID_W)
    k_all = jnp.concatenate([k_ref[pl.ds(start, nk), :], kc_ref[...]], axis=0)
    v_all = jnp.concatenate([v_ref[pl.ds(start, nk), :], vc_ref[...]], axis=0)
    ones = jnp.ones((k_all.shape[0], LANES), BF16)
    lane = lax.broadcasted_iota(jnp.int32, (q_ref.shape[0], LANES), 1)

    def attend(h, k_pair, v_pair):
        s = _dot_nt(q_ref[:, h * LANES:(h + 1) * LANES], k_pair)
        s = jnp.concatenate([s[:, :nk] + bias_ref[h], s[:, nk:]], axis=1)
        p = jnp.exp2(s - jnp.max(s, axis=-1, keepdims=True)).astype(BF16)
        o = _dot(p, v_pair)
        return o[:, :LANES] / o[:, LANES:LANES + 1]

    for pair in range(D_HEADS // 2):
        cols = slice(pair * LANES, (pair + 1) * LANES)
        k_pair = k_all[:, cols]
        v_pair = jnp.concatenate([v_all[:, cols], ones], axis=1)
        even, odd = attend(2 * pair, k_pair, v_pair), attend(2 * pair + 1, k_pair, v_pair)
        o_ref[:, cols] = jnp.where(lane < HEAD_DIM, even, odd).astype(BF16)


def _neighbourhood(dq, dk, dv, dk_c, dv_c, bias):
    bsz, length, w = dk.shape
    rows = length // GRID_W
    assert rows % NA_QROWS == 0 and rows >= NA_KROWS
    nblk = rows // NA_QROWS
    tq = NA_QROWS * GRID_W
    cl = dk_c.shape[1]

    def bias_class(i, j):
        cls = jnp.where(j == 0, 0, jnp.where(j == nblk - 1, 2, 1))
        return (cls, 0, 0)

    return pl.pallas_call(
        functools.partial(_na_kernel, rows=rows),
        grid=(bsz, nblk),
        in_specs=[
            pl.BlockSpec((None, tq, dq.shape[2]), lambda i, j: (i, j, 0)),
            pl.BlockSpec((None, length, w), lambda i, j: (i, 0, 0)),
            pl.BlockSpec((None, length, w), lambda i, j: (i, 0, 0)),
            pl.BlockSpec((None, cl, w), lambda i, j: (i, 0, 0)),
            pl.BlockSpec((None, cl, w), lambda i, j: (i, 0, 0)),
            pl.BlockSpec((D_HEADS, tq, NA_KROWS * GRID_W), bias_class),
        ],
        out_specs=pl.BlockSpec((None, tq, w), lambda i, j: (i, j, 0)),
        out_shape=jax.ShapeDtypeStruct((bsz, length, w), BF16),
        compiler_params=_cparams("parallel", "arbitrary"),
    )(dq, dk, dv, dk_c, dv_c, bias)


def _router_kernel(h_ref, mod_ref, wr_ref, n_ref, sel_ref):
    n = _modulate(h_ref[...], mod_ref[3:4, :], mod_ref[4:5, :])
    n_hi = n.astype(BF16)
    n_lo = (n - n_hi.astype(F32)).astype(BF16)
    w = wr_ref[...]
    w_hi = w.astype(BF16)
    w_lo = (w - w_hi.astype(F32)).astype(BF16)
    logits = _dot(n_hi, w_hi) + (_dot(n_lo, w_hi) + _dot(n_hi, w_lo))
    lane = lax.broadcasted_iota(jnp.int32, logits.shape, 1)
    logits = jnp.where(lane < N_EXPERTS, logits, -jnp.inf)
    m1 = jnp.max(logits, axis=-1, keepdims=True)
    i1 = jnp.min(jnp.where(logits == m1, lane, LANES), axis=-1, keepdims=True)
    rest = jnp.where(lane == i1, -jnp.inf, logits)
    m2 = jnp.max(rest, axis=-1, keepdims=True)
    i2 = jnp.min(jnp.where(rest == m2, lane, LANES), axis=-1, keepdims=True)
    e2 = jnp.exp(m2 - m1)
    w1 = 1.0 / (1.0 + e2)
    w2 = e2 / (1.0 + e2)
    sel = jnp.where(lane == 0, i1.astype(F32), jnp.where(lane == 1, i2.astype(F32),
                    jnp.where(lane == 2, w1, jnp.where(lane == 3, w2, 0.0))))
    sel_ref[...] = sel
    n_ref[...] = n_hi


def _router(h, mod, w_router_padded):
    bsz, length, d = h.shape
    tm = _row_tile(length, 1024)
    return pl.pallas_call(
        _router_kernel,
        grid=(bsz, length // tm),
        in_specs=[
            pl.BlockSpec((None, tm, d), lambda i, j: (i, j, 0)),
            pl.BlockSpec((None, 8, d), lambda i, j: (i, 0, 0)),
            pl.BlockSpec((d, LANES), lambda i, j: (0, 0)),
        ],
        out_specs=[
            pl.BlockSpec((None, tm, d), lambda i, j: (i, j, 0)),
            pl.BlockSpec((None, tm, LANES), lambda i, j: (i, j, 0)),
        ],
        out_shape=[
            jax.ShapeDtypeStruct((bsz, length, d), BF16),
            jax.ShapeDtypeStruct((bsz, length, LANES), F32),
        ],
        compiler_params=_cparams("parallel", "parallel"),
    )(h, mod, w_router_padded)


def _moe_dispatch_kernel(pos_ref, w_ref, n_ref, xs_ref, gs_ref):
    nrow = xs_ref.shape[0]
    base = pl.program_id(1) * nrow
    row = base + lax.broadcasted_iota(jnp.int32, (nrow, 1), 0)
    eq0 = pos_ref[0:1, :] == row
    eq1 = pos_ref[1:2, :] == row
    perm = jnp.where(eq0, 1.0, jnp.where(eq1, 1.0, 0.0)).astype(BF16)
    xs_ref[...] = _dot(perm, n_ref[...]).astype(BF16)
    gate = jnp.where(eq0, w_ref[0:1, :], jnp.where(eq1, w_ref[1:2, :], 0.0))
    gs_ref[...] = jnp.sum(gate, axis=-1, keepdims=True)


def _moe_expert_kernel(piece_ref, exp_ref, nused_ref, *refs):
    xs_refs, gs_refs = refs[:MOE_GROUP], refs[MOE_GROUP:2 * MOE_GROUP]
    wg_ref, wu_ref, wd_ref, ys_ref = refs[2 * MOE_GROUP:]
    i = pl.program_id(0)

    @pl.when(i < nused_ref[0])
    def _():
        x = jnp.concatenate([r[...] for r in xs_refs], axis=0)
        gate = jnp.concatenate([r[...] for r in gs_refs], axis=0)
        ff = wg_ref.shape[1]
        y = jnp.zeros((x.shape[0], wd_ref.shape[1]), F32)
        for f in range(0, ff, MOE_FF_CHUNK):
            cols = slice(f, min(f + MOE_FF_CHUNK, ff))
            a = jax.nn.silu(_dot(x, wg_ref[:, cols])) * _dot(x, wu_ref[:, cols])
            y = y + _dot(a.astype(BF16), wd_ref[cols, :])
        ys_ref[...] = (y * gate).astype(BF16)

    @pl.when(i >= nused_ref[0])
    def _():
        ys_ref[...] = jnp.zeros_like(ys_ref)


def _moe_combine_kernel(slot_ref, pos_ref, h_ref, mod_ref, fg_ref, *refs):
    ys = jnp.concatenate([r[...] for r in refs[:-1]], axis=0)
    o_ref = refs[-1]
    ntok, nrow = pos_ref.shape[0], ys.shape[0]
    row = lax.broadcasted_iota(jnp.int32, (ntok, nrow), 1)
    perm = jnp.where(pos_ref[:, 0:1] == row, 1.0, jnp.where(pos_ref[:, 1:2] == row, 1.0, 0.0)).astype(BF16)
    y = h_ref[...] + mod_ref[5:6, :] * _dot(perm, ys)
    ms = jnp.mean(y * y, axis=-1, keepdims=True)
    o_ref[...] = y * lax.rsqrt(ms + EPS) * fg_ref[...]


def _moe_plan(sel, n_tiles, tile_tokens, rows_per_tile):
    pieces_per_tile = rows_per_tile // MOE_PIECE
    experts = sel[:, :TOP_K].astype(jnp.int32).reshape(n_tiles, tile_tokens * TOP_K)
    weights = sel[:, TOP_K:2 * TOP_K].reshape(n_tiles, tile_tokens, TOP_K)
    onehot = (experts[:, :, None] == jnp.arange(N_EXPERTS, dtype=jnp.int32)).astype(jnp.int32)
    csum = jnp.cumsum(onehot, axis=1)
    rank = jnp.sum((csum - onehot) * onehot, axis=-1)
    count = csum[:, -1, :]
    npiece = (count + MOE_PIECE - 1) // MOE_PIECE
    piece_end = jnp.cumsum(npiece, axis=1)
    piece_off = piece_end - npiece
    pos = jnp.sum(onehot * piece_off[:, None, :], axis=-1) * MOE_PIECE + rank
    pos = pos.reshape(n_tiles, tile_tokens, TOP_K)

    b = jnp.arange(pieces_per_tile, dtype=jnp.int32)
    piece_expert = jnp.sum((piece_end[:, None, :] <= b[None, :, None]).astype(jnp.int32), axis=-1)
    flat_expert = piece_expert.reshape(-1)
    n_pieces = flat_expert.shape[0]
    n_slots = n_pieces + N_EXPERTS * MOE_GROUP
    classes = jnp.arange(N_EXPERTS + 1, dtype=jnp.int32)
    cls_onehot = (flat_expert[:, None] == classes).astype(jnp.int32)
    cls_csum = jnp.cumsum(cls_onehot, axis=0)
    piece_rank = jnp.sum((cls_csum - cls_onehot) * cls_onehot, axis=-1)
    n_cls = cls_csum[-1]
    groups = (n_cls[:N_EXPERTS] + MOE_GROUP - 1) // MOE_GROUP
    start = (jnp.cumsum(groups) - groups) * MOE_GROUP
    n_used_groups = jnp.sum(groups)

    slot = jnp.arange(n_slots, dtype=jnp.int32)
    slot_expert = jnp.sum((start[None, :] <= slot[:, None]).astype(jnp.int32), axis=-1) - 1
    slot_taken = (slot - start[slot_expert]) < n_cls[slot_expert]
    free_slots = jnp.argsort(slot_taken.astype(jnp.int32) * n_slots + slot).astype(jnp.int32)
    is_used = flat_expert < N_EXPERTS
    safe_expert = jnp.minimum(flat_expert, N_EXPERTS - 1)
    piece_slot = jnp.where(is_used, start[safe_expert] + piece_rank, free_slots[piece_rank])
    first_unused = jnp.argmax(jnp.logical_not(is_used)).astype(jnp.int32)
    piece_ids = jnp.arange(n_pieces, dtype=jnp.int32)
    slot_piece = jnp.full((n_slots,), first_unused, jnp.int32).at[piece_slot].set(piece_ids, unique_indices=True)
    step_expert = slot_expert[::MOE_GROUP]
    return pos, weights, slot_piece, piece_slot.astype(jnp.int32), step_expert, n_used_groups.reshape(1)


def _moe_final(n2, sel, h, mod, final_gain, wg, wu, wd):
    bsz, length, d = h.shape
    n_exp, _, ff = wg.shape
    tokens = bsz * length
    tt = _row_tile(length, MOE_TOKENS)
    n_tiles = tokens // tt
    rows = TOP_K * tt + N_EXPERTS * MOE_PIECE
    drows = _divisor_tile(rows, MOE_DISPATCH_ROWS, MOE_PIECE)
    ctok = _row_tile(tt, MOE_COMBINE_TOKENS)
    pos, weights, slot_piece, piece_slot, step_expert, n_used = _moe_plan(
        sel.reshape(tokens, LANES), n_tiles, tt, rows)

    pos_rows = jnp.swapaxes(pos, 1, 2)
    w_rows = jnp.swapaxes(weights, 1, 2)
    xs, gs = pl.pallas_call(
        _moe_dispatch_kernel,
        grid=(n_tiles, rows // drows),
        in_specs=[
            pl.BlockSpec((None, TOP_K, tt), lambda t, r: (t, 0, 0)),
            pl.BlockSpec((None, TOP_K, tt), lambda t, r: (t, 0, 0)),
            pl.BlockSpec((tt, d), lambda t, r: (t, 0)),
        ],
        out_specs=[
            pl.BlockSpec((None, drows, d), lambda t, r: (t, r, 0)),
            pl.BlockSpec((None, drows, 1), lambda t, r: (t, r, 0)),
        ],
        out_shape=[
            jax.ShapeDtypeStruct((n_tiles, rows, d), BF16),
            jax.ShapeDtypeStruct((n_tiles, rows, 1), F32),
        ],
        compiler_params=_cparams("parallel", "parallel"),
    )(pos_rows, w_rows, n2.reshape(tokens, d))

    n_steps = slot_piece.shape[0] // MOE_GROUP
    resident = pl.Buffered(1)

    def piece_spec(j, width):
        return pl.BlockSpec((MOE_PIECE, width), lambda i, sp, ex, nu: (sp[i * MOE_GROUP + j], 0))

    def weight_spec(shape):
        return pl.BlockSpec((None,) + shape, lambda i, sp, ex, nu: (ex[i], 0, 0), pipeline_mode=resident)

    xs_flat, gs_flat = xs.reshape(n_tiles * rows, d), gs.reshape(n_tiles * rows, 1)
    step_rows = MOE_GROUP * MOE_PIECE
    ys = pl.pallas_call(
        _moe_expert_kernel,
        grid_spec=pltpu.PrefetchScalarGridSpec(
            num_scalar_prefetch=3,
            grid=(n_steps,),
            in_specs=([piece_spec(j, d) for j in range(MOE_GROUP)]
                      + [piece_spec(j, 1) for j in range(MOE_GROUP)]
                      + [weight_spec((d, ff)), weight_spec((d, ff)), weight_spec((ff, d))]),
            out_specs=pl.BlockSpec((step_rows, d), lambda i, sp, ex, nu: (i, 0)),
        ),
        out_shape=jax.ShapeDtypeStruct((n_steps * step_rows, d), BF16),
        compiler_params=_cparams("arbitrary"),
    )(slot_piece, step_expert, n_used, *([xs_flat] * MOE_GROUP), *([gs_flat] * MOE_GROUP), wg, wu, wd)

    tiles_per_seq = length // tt
    pieces_per_tile = rows // MOE_PIECE

    def tile_piece_spec(k):
        return pl.BlockSpec((MOE_PIECE, d), lambda t, c, slot: (slot[t * pieces_per_tile + k], 0))

    out = pl.pallas_call(
        _moe_combine_kernel,
        grid_spec=pltpu.PrefetchScalarGridSpec(
            num_scalar_prefetch=1,
            grid=(n_tiles, tt // ctok),
            in_specs=[
                pl.BlockSpec((None, ctok, TOP_K), lambda t, c, slot: (t, c, 0)),
                pl.BlockSpec((None, ctok, d), lambda t, c, slot: (t, c, 0)),
                pl.BlockSpec((None, 8, d), lambda t, c, slot: (t // tiles_per_seq, 0, 0)),
                pl.BlockSpec((1, d), lambda t, c, slot: (0, 0)),
            ] + [tile_piece_spec(k) for k in range(pieces_per_tile)],
            out_specs=pl.BlockSpec((None, ctok, d), lambda t, c, slot: (t, c, 0)),
        ),
        out_shape=jax.ShapeDtypeStruct((n_tiles, tt, d), F32),
        compiler_params=_cparams("parallel", "parallel"),
    )(piece_slot, pos, h.reshape(n_tiles, tt, d), mod, final_gain, *([ys] * pieces_per_tile))
    return out.reshape(bsz, length, d)


def _group_mean_matrix(width, group):
    return jnp.asarray(np.kron(np.eye(width // group), np.full((group, group), 1.0 / group)), BF16)


def _rope_tables(length):
    t = jnp.arange(length, dtype=jnp.int32)
    row = (t // GRID_W).astype(F32)
    col = (t % GRID_W).astype(F32)
    n_axis = HEAD_DIM // 4
    inv_freq = ROPE_THETA ** (-jnp.arange(n_axis, dtype=F32) / n_axis)
    ang = jnp.concatenate([row[:, None] * inv_freq, col[:, None] * inv_freq], axis=-1)
    cos = jnp.repeat(jnp.cos(ang), 2, axis=-1)
    sin = jnp.repeat(jnp.sin(ang), 2, axis=-1)
    sign = jnp.tile(jnp.asarray([-1.0, 1.0], F32), HEAD_DIM // 2)
    reps = LANES // HEAD_DIM
    return jnp.tile(cos, (1, reps)), jnp.tile(sin * sign, (1, reps))


def _spread_heads(w_q, upper_half):
    d = w_q.shape[0]
    n_heads = w_q.shape[1] // HEAD_DIM
    w = w_q.reshape(d, n_heads, HEAD_DIM)
    zeros = jnp.zeros_like(w)
    upper = jnp.asarray(upper_half, bool)[None, :, None]
    lo = jnp.where(upper, zeros, w)
    hi = jnp.where(upper, w, zeros)
    return jnp.concatenate([lo, hi], axis=2).reshape(d, n_heads * LANES)


def kernel(x, c, ctx, c_ctx, w_ada, b_ada, w_in_ab, v_gain, w_spatial, b_spatial, w_out_ab,
           w_gate_dense, w_up_dense, w_down_dense, w_in_cd, q_gain, k_gain, rel_bias, w_out_cd,
           w_router, w_gate_moe, w_up_moe, w_down_moe, final_gain):
    bsz, length, d = x.shape
    depth = w_ada.shape[0]
    assert depth == 2 and bsz <= 8
    aw = v_gain.shape[1]
    group_dim = aw // A_GROUPS
    bgd = (w_in_ab.shape[2] - 2 * aw) // B_GROUPS

    cond = jnp.zeros((16, d), F32).at[:bsz].set(c).at[8].set(c_ctx)
    ada = _adaln(cond, w_ada, b_ada).reshape(depth, 16, 6, d)
    pad = jnp.zeros((depth, 16, 2, d), F32)
    ada = jnp.concatenate([ada, pad], axis=2)
    mod_lat = [ada[i, :bsz] for i in range(depth)]
    mod_ctx = [jnp.broadcast_to(ada[i, 8], (bsz, 8, d)) for i in range(depth)]

    cmat, smat = _dft_tables(bgd, bgd ** -0.5)
    p_ab = dict(
        w_in=w_in_ab[0].astype(BF16),
        v_gain=v_gain[0].reshape(1, aw),
        gmean=_group_mean_matrix(aw, group_dim),
        w_sp=w_spatial[0].transpose(1, 0, 2).reshape(CHUNK, A_GROUPS * CHUNK).astype(BF16),
        b_sp=jnp.repeat(b_spatial[0].T, group_dim, axis=1),
        cmat=cmat, smat=smat,
        w_out=w_out_ab[0].astype(BF16),
    )
    wg0, wu0, wd0 = (w.astype(BF16) for w in (w_gate_dense[0], w_up_dense[0], w_down_dense[0]))
    h = _mix_ab(x, mod_lat[0], p_ab)
    hc = _mix_ab(ctx, mod_ctx[0], p_ab)
    h = _dense_ffn(h, mod_lat[0], wg0, wu0, wd0)
    hc = _dense_ffn(hc, mod_ctx[0], wg0, wu0, wd0)

    cos, sin = _rope_tables(length)
    qw = C_HEADS * HEAD_DIM
    kw = C_KV_HEADS * HEAD_DIM
    w_cd = w_in_cd[0]
    dq0, dq1 = qw + 2 * kw, qw + 2 * kw + D_HEADS * HEAD_DIM
    p_cd = dict(
        w_in=jnp.concatenate([
            _spread_heads(w_cd[:, :qw], [h // C_GROUP == 1 for h in range(C_HEADS)]),
            w_cd[:, qw:dq0],
            _spread_heads(w_cd[:, dq0:dq1], [h % 2 == 1 for h in range(D_HEADS)]),
            w_cd[:, dq1:]], axis=1).astype(BF16),
        gmean=_group_mean_matrix(kw, HEAD_DIM),
        q_gain=jnp.tile(q_gain[0], C_HEADS * LANES // HEAD_DIM).reshape(1, C_HEADS * LANES),
        k_gain=jnp.tile(k_gain[0], C_KV_HEADS).reshape(1, kw),
        cos=cos, sin=sin,
    )
    cq, ck, cv, dq, dk, dv = _project_cd(h, mod_lat[1], p_cd, rope=True)
    p_cd_ctx = dict(p_cd, cos=cos[:hc.shape[1]], sin=sin[:hc.shape[1]])
    _, ck_c, cv_c, _, dk_c, dv_c = _project_cd(hc, mod_ctx[1], p_cd_ctx, rope=False)
    o_c = _gqa(cq, jnp.concatenate([ck_c, ck], axis=1), jnp.concatenate([cv_c, cv], axis=1))
    o_d = _neighbourhood(dq, dk, dv, dk_c, dv_c, _na_bias_tables(rel_bias[0], length // GRID_W))
    h = _proj_residual(o_c, o_d, w_out_cd[0].astype(BF16), h, mod_lat[1], 2)

    wr = jnp.zeros((d, LANES), F32).at[:, :N_EXPERTS].set(w_router[0])
    n2, sel = _router(h, mod_lat[1], wr)
    return _moe_final(n2, sel, h, mod_lat[1], final_gain.reshape(1, d),
                      w_gate_moe[0].astype(BF16), w_up_moe[0].astype(BF16), w_down_moe[0].astype(BF16))
```

```python
import functools
import math

import numpy as np
import jax
import jax.numpy as jnp
from jax import lax
from jax.experimental import pallas as pl
from jax.experimental.pallas import tpu as pltpu

F32 = jnp.float32
BF16 = jnp.bfloat16

GRID_W = 64
EPS = 1e-6
CHUNK = 128
A_GROUPS = 8
B_GROUPS = 4
HEAD_DIM = 64
C_HEADS = 8
C_KV_HEADS = 2
C_GROUP = C_HEADS // C_KV_HEADS
D_HEADS = 8
NA_WIN_H = 8
NA_WIN_W = 16
ROPE_THETA = 10000.0
N_EXPERTS = 8
TOP_K = 2

LANES = 128
GQA_SPLIT = 2
NA_QROWS = 4
NA_KROWS = NA_QROWS + NA_WIN_H
MASK_VALUE = -1e30
LOG2E = 1.4426950408889634
VMEM_LIMIT = 56 * 1024 * 1024

MOE_TOKENS = 1024
MOE_PIECE = 32
MOE_GROUP = 8
MOE_DISPATCH_ROWS = 1280
MOE_COMBINE_TOKENS = 512
MOE_FF_CHUNK = 512


def _cparams(*sem):
    return pltpu.CompilerParams(dimension_semantics=sem, vmem_limit_bytes=VMEM_LIMIT)


def _modulate(h, shift, scale):
    ms = jnp.mean(h * h, axis=-1, keepdims=True)
    return h * lax.rsqrt(ms + EPS) * (1.0 + scale) + shift


def _dot(a, b):
    return jnp.dot(a, b, preferred_element_type=F32)


def _dot_nt(a, b):
    return lax.dot_general(a, b, (((1,), (1,)), ((), ())), preferred_element_type=F32)


def _row_tile(length, target):
    t = min(length, target)
    assert length % t == 0
    return t


def _adaln_kernel(c_ref, w_ref, b_ref, o_ref):
    s = jax.nn.silu(c_ref[...]).astype(BF16)
    o_ref[...] = _dot(s, w_ref[...].astype(BF16)) + b_ref[...]


def _adaln(cond, w_ada, b_ada):
    depth, d, n = w_ada.shape
    r = cond.shape[0]
    tn = 512
    return pl.pallas_call(
        _adaln_kernel,
        grid=(depth, n // tn),
        in_specs=[
            pl.BlockSpec((r, d), lambda i, j: (0, 0)),
            pl.BlockSpec((None, d, tn), lambda i, j: (i, 0, j)),
            pl.BlockSpec((None, 1, tn), lambda i, j: (i, 0, j)),
        ],
        out_specs=pl.BlockSpec((None, r, tn), lambda i, j: (i, 0, j)),
        out_shape=jax.ShapeDtypeStruct((depth, r, n), F32),
        compiler_params=_cparams("parallel", "parallel"),
    )(cond, w_ada, b_ada.reshape(depth, 1, n))


def _ab_in_kernel(h_ref, mod_ref, w_ref, vg_ref, gm_ref, wsp_ref, bsp_ref, cm_ref, sm_ref,
                  a_ref, fc_ref, fs_ref, *, aw):
    tm = h_ref.shape[0]
    n = _modulate(h_ref[...], mod_ref[0:1, :], mod_ref[1:2, :])
    z = _dot(n.astype(BF16), w_ref[...])
    u = jax.nn.gelu(z[:, :aw])
    v = jax.nn.gelu(z[:, aw:2 * aw])
    fb = z[:, 2 * aw:].astype(BF16)
    ms = _dot((v * v).astype(BF16), gm_ref[...])
    vn = v * lax.rsqrt(ms + EPS) * vg_ref[...]
    group_dim = aw // A_GROUPS
    lane_group = lax.broadcasted_iota(jnp.int32, (CHUNK, aw), 1) // group_dim
    for c in range(tm // CHUNK):
        rows = slice(c * CHUNK, (c + 1) * CHUNK)
        vc = vn[rows, :]
        stack = jnp.concatenate(
            [jnp.where(lane_group == g, vc, 0.0).astype(BF16) for g in range(A_GROUPS)], axis=0)
        s = _dot(wsp_ref[...], stack) + bsp_ref[...]
        a_ref[rows, :] = (u[rows, :] * s).astype(BF16)
    bw = fb.shape[1] // B_GROUPS
    for g in range(B_GROUPS):
        cols = slice(g * bw, (g + 1) * bw)
        fc_ref[:, cols] = _dot(fb[:, cols], cm_ref[...]).astype(BF16)
        fs_ref[:, cols] = _dot(fb[:, cols], sm_ref[...]).astype(BF16)


def _seq_dft_kernel(c_ref, s_ref, fc_ref, fs_ref, o_ref):
    o_ref[...] = (_dot(c_ref[...], fc_ref[...]) - _dot(s_ref[...], fs_ref[...])).astype(BF16)


def _mixer_residual(a_ref, b_ref, w_ref, h_ref, mod_ref):
    ka = a_ref.shape[1]
    y = _dot(a_ref[...], w_ref[:ka, :]) + _dot(b_ref[...], w_ref[ka:, :])
    return h_ref[...] + mod_ref[2:3, :] * y


def _dft_tables(n, scale):
    k = jnp.arange(n, dtype=jnp.int32)
    ang = ((k[:, None] * k[None, :]) % n).astype(F32) * (2.0 * math.pi / n)
    return (jnp.cos(ang) * scale).astype(BF16), (jnp.sin(ang) * scale).astype(BF16)


def _mix_ab(h, mod, p):
    bsz, length, d = h.shape
    aw = p["v_gain"].shape[1]
    bw_total = p["w_in"].shape[1] - 2 * aw
    tm = _row_tile(length, 512)
    const = lambda i, j: (0, 0)
    a_out, fc, fs = pl.pallas_call(
        functools.partial(_ab_in_kernel, aw=aw),
        grid=(bsz, length // tm),
        in_specs=[
            pl.BlockSpec((None, tm, d), lambda i, j: (i, j, 0)),
            pl.BlockSpec((None, 8, d), lambda i, j: (i, 0, 0)),
            pl.BlockSpec(p["w_in"].shape, const),
            pl.BlockSpec(p["v_gain"].shape, const),
            pl.BlockSpec(p["gmean"].shape, const),
            pl.BlockSpec(p["w_sp"].shape, const),
            pl.BlockSpec(p["b_sp"].shape, const),
            pl.BlockSpec(p["cmat"].shape, const),
            pl.BlockSpec(p["smat"].shape, const),
        ],
        out_specs=[
            pl.BlockSpec((None, tm, aw), lambda i, j: (i, j, 0)),
            pl.BlockSpec((None, tm, bw_total), lambda i, j: (i, j, 0)),
            pl.BlockSpec((None, tm, bw_total), lambda i, j: (i, j, 0)),
        ],
        out_shape=[
            jax.ShapeDtypeStruct((bsz, length, aw), BF16),
            jax.ShapeDtypeStruct((bsz, length, bw_total), BF16),
            jax.ShapeDtypeStruct((bsz, length, bw_total), BF16),
        ],
        compiler_params=_cparams("parallel", "parallel"),
    )(h, mod, p["w_in"], p["v_gain"], p["gmean"], p["w_sp"], p["b_sp"], p["cmat"], p["smat"])

    return a_out, _seq_dft(fc, fs)


def _seq_dft_direct(fc, fs):
    bsz, length, width = fc.shape
    cl, sl = _dft_tables(length, length ** -0.5)
    tk = _row_tile(length, 512)
    return pl.pallas_call(
        _seq_dft_kernel,
        grid=(length // tk, bsz),
        in_specs=[
            pl.BlockSpec((tk, length), lambda k, b: (k, 0)),
            pl.BlockSpec((tk, length), lambda k, b: (k, 0)),
            pl.BlockSpec((None, length, width), lambda k, b: (b, 0, 0)),
            pl.BlockSpec((None, length, width), lambda k, b: (b, 0, 0)),
        ],
        out_specs=pl.BlockSpec((None, tk, width), lambda k, b: (b, k, 0)),
        out_shape=jax.ShapeDtypeStruct((bsz, length, width), BF16),
        compiler_params=_cparams("parallel", "parallel"),
    )(cl, sl, fc, fs)


def _fft_rows_kernel(fc_ref, fs_ref, ma_ref, tc_ref, ts_ref, yr_ref, yi_ref):
    n1 = fc_ref.shape[0]
    y = _dot(ma_ref[...], jnp.concatenate([fc_ref[...], fs_ref[...]], axis=0))
    yr, yi = y[:n1, :], y[n1:, :]
    c, s = tc_ref[...], ts_ref[...]
    yr_ref[...] = (yr * c + yi * s).astype(BF16)
    yi_ref[...] = (yi * c - yr * s).astype(BF16)


def _fft_cols_kernel(yr_ref, yi_ref, mb_ref, o_ref):
    width = yr_ref.shape[2]
    for j in range(yr_ref.shape[0]):
        y = jnp.concatenate([yr_ref[j], yi_ref[j]], axis=0)
        o_ref[:, j * width:(j + 1) * width] = _dot(mb_ref[...], y).astype(BF16)


def _seq_dft(fc, fs):
    bsz, length, width = fc.shape
    n2 = GRID_W
    n1 = length // n2
    if n1 < 16 or length % n2:
        return _seq_dft_direct(fc, fs)
    ncol = n2 * width

    def angles(a, b, period):
        prod = (jnp.arange(a, dtype=jnp.int32)[:, None] * jnp.arange(b, dtype=jnp.int32)[None, :]) % period
        return prod.astype(F32) * (2.0 * math.pi / period)

    th = angles(n1, n1, n1)
    c1, s1 = jnp.cos(th) * n1 ** -0.5, jnp.sin(th) * n1 ** -0.5
    ma = jnp.concatenate([jnp.concatenate([c1, -s1], axis=1),
                          jnp.concatenate([-s1, -c1], axis=1)], axis=0).astype(BF16)
    ph = angles(n1, n2, length)
    tc = jnp.repeat(jnp.cos(ph), width, axis=1)
    ts = jnp.repeat(jnp.sin(ph), width, axis=1)
    ps = angles(n2, n2, n2)
    mb = (jnp.concatenate([jnp.cos(ps), jnp.sin(ps)], axis=1) * n2 ** -0.5).astype(BF16)

    cols = _divisor_tile(ncol, 4096)
    row_blk = lambda b, j: (b, 0, j)
    yr, yi = pl.pallas_call(
        _fft_rows_kernel,
        grid=(bsz, ncol // cols),
        in_specs=[
            pl.BlockSpec((None, n1, cols), row_blk),
            pl.BlockSpec((None, n1, cols), row_blk),
            pl.BlockSpec((2 * n1, 2 * n1), lambda b, j: (0, 0)),
            pl.BlockSpec((n1, cols), lambda b, j: (0, j)),
            pl.BlockSpec((n1, cols), lambda b, j: (0, j)),
        ],
        out_specs=[pl.BlockSpec((None, n1, cols), row_blk)] * 2,
        out_shape=[jax.ShapeDtypeStruct((bsz, n1, ncol), BF16)] * 2,
        compiler_params=_cparams("parallel", "parallel"),
    )(fc.reshape(bsz, n1, ncol), fs.reshape(bsz, n1, ncol), ma, tc, ts)

    kb = _divisor_tile(n1, 8, 1)
    out = pl.pallas_call(
        _fft_cols_kernel,
        grid=(bsz, n1 // kb),
        in_specs=[
            pl.BlockSpec((None, kb, n2, width), lambda b, k: (b, k, 0, 0)),
            pl.BlockSpec((None, kb, n2, width), lambda b, k: (b, k, 0, 0)),
            pl.BlockSpec((n2, 2 * n2), lambda b, k: (0, 0)),
        ],
        out_specs=pl.BlockSpec((None, n2, kb * width), lambda b, k: (b, 0, k)),
        out_shape=jax.ShapeDtypeStruct((bsz, n2, n1 * width), BF16),
        compiler_params=_cparams("parallel", "parallel"),
    )(yr.reshape(bsz, n1, n2, width), yi.reshape(bsz, n1, n2, width), mb)
    return out.reshape(bsz, length, width)


def _dense_ffn_kernel(a_ref, b_ref, wo_ref, h_ref, mod_ref, wg_ref, wu_ref, wd_ref, o_ref, n_sc, acc_sc, h_sc):
    f = pl.program_id(2)

    @pl.when(f == 0)
    def _():
        h = _mixer_residual(a_ref, b_ref, wo_ref, h_ref, mod_ref)
        h_sc[...] = h
        n_sc[...] = _modulate(h, mod_ref[3:4, :], mod_ref[4:5, :]).astype(BF16)
        acc_sc[...] = jnp.zeros_like(acc_sc)

    n = n_sc[...]
    a = jax.nn.silu(_dot(n, wg_ref[...])) * _dot(n, wu_ref[...])
    acc_sc[...] += _dot(a.astype(BF16), wd_ref[...])

    @pl.when(f == pl.num_programs(2) - 1)
    def _():
        o_ref[...] = h_sc[...] + mod_ref[5:6, :] * acc_sc[...]


def _divisor_tile(n, target, unit=LANES):
    best = None
    for t in range(unit, min(n, target) + 1, unit):
        if n % t == 0:
            best = t
    assert best is not None
    return best


def _mixer_out_dense_ffn(a, b, w_out, h, mod, wg, wu, wd):
    bsz, length, d = h.shape
    ff = wg.shape[1]
    ka, kb = a.shape[2], b.shape[2]
    tm = _row_tile(length, 1024)
    tf = _divisor_tile(ff, 704)
    tok = lambda w: pl.BlockSpec((None, tm, w), lambda i, j, f: (i, j, 0))
    return pl.pallas_call(
        _dense_ffn_kernel,
        grid=(bsz, length // tm, ff // tf),
        in_specs=[
            tok(ka), tok(kb),
            pl.BlockSpec((ka + kb, d), lambda i, j, f: (0, 0)),
            tok(d),
            pl.BlockSpec((None, 8, d), lambda i, j, f: (i, 0, 0)),
            pl.BlockSpec((d, tf), lambda i, j, f: (0, f)),
            pl.BlockSpec((d, tf), lambda i, j, f: (0, f)),
            pl.BlockSpec((tf, d), lambda i, j, f: (f, 0)),
        ],
        out_specs=tok(d),
        out_shape=jax.ShapeDtypeStruct((bsz, length, d), F32),
        scratch_shapes=[pltpu.VMEM((tm, d), BF16), pltpu.VMEM((tm, d), F32), pltpu.VMEM((tm, d), F32)],
        compiler_params=_cparams("parallel", "parallel", "arbitrary"),
    )(a, b, w_out, h, mod, wg, wu, wd)


def _pair_swap(x):
    width = x.shape[-1]
    lane = lax.broadcasted_iota(jnp.int32, x.shape, x.ndim - 1)
    nxt = pltpu.roll(x, width - 1, x.ndim - 1)
    prv = pltpu.roll(x, 1, x.ndim - 1)
    return jnp.where(lane % 2 == 0, nxt, prv)


def _rope(x, cos, sin):
    reps = x.shape[1] // LANES
    cos = jnp.concatenate([cos] * reps, axis=1) if reps > 1 else cos
    sin = jnp.concatenate([sin] * reps, axis=1) if reps > 1 else sin
    return x * cos + _pair_swap(x) * sin


def _cd_in_kernel(h_ref, mod_ref, w_ref, gm_ref, qg_ref, kg_ref, cos_ref, sin_ref,
                  cq_ref, ck_ref, cv_ref, dq_ref, dk_ref, dv_ref, *, rope):
    n = _modulate(h_ref[...], mod_ref[0:1, :], mod_ref[1:2, :])
    z = _dot(n.astype(BF16), w_ref[...])
    qw, kw, dqw, dw = cq_ref.shape[1], ck_ref.shape[1], dq_ref.shape[1], dk_ref.shape[1]
    o1, o2, o3, o4, o5 = qw, qw + kw, qw + 2 * kw, qw + 2 * kw + dqw, qw + 2 * kw + dqw + dw
    scale = HEAD_DIM ** -0.5

    zq = z[:, :o1]
    blocks = []
    for h in range(C_HEADS):
        t = zq[:, h * LANES:(h + 1) * LANES]
        ms = jnp.sum(t * t, axis=-1, keepdims=True) * (1.0 / HEAD_DIM)
        blocks.append(t * lax.rsqrt(ms + EPS))
    cq = jnp.concatenate(blocks, axis=1) * qg_ref[...]
    zk = z[:, o1:o2]
    ck = zk * lax.rsqrt(_dot((zk * zk).astype(BF16), gm_ref[...]) + EPS) * kg_ref[...]
    if rope:
        cq = _rope(cq, cos_ref[...], sin_ref[...])
        ck = _rope(ck, cos_ref[...], sin_ref[...])
    scale = scale * LOG2E
    cq_ref[...] = (cq * scale).astype(BF16)
    ck_ref[...] = ck.astype(BF16)
    cv_ref[...] = z[:, o2:o3].astype(BF16)
    dq_ref[...] = (z[:, o3:o4] * scale).astype(BF16)
    dk_ref[...] = z[:, o4:o5].astype(BF16)
    dv_ref[...] = z[:, o5:].astype(BF16)


def _project_cd(h, mod, p, rope):
    bsz, length, d = h.shape
    qw, kw, dw = C_HEADS * LANES, C_KV_HEADS * HEAD_DIM, D_HEADS * HEAD_DIM
    tm = _row_tile(length, 512)
    const = lambda i, j: (0, 0)
    tok = lambda w: pl.BlockSpec((None, tm, w), lambda i, j: (i, j, 0))
    widths = (qw, kw, kw, D_HEADS * LANES, dw, dw)
    return pl.pallas_call(
        functools.partial(_cd_in_kernel, rope=rope),
        grid=(bsz, length // tm),
        in_specs=[
            tok(d),
            pl.BlockSpec((None, 8, d), lambda i, j: (i, 0, 0)),
            pl.BlockSpec(p["w_in"].shape, const),
            pl.BlockSpec(p["gmean"].shape, const),
            pl.BlockSpec(p["q_gain"].shape, const),
            pl.BlockSpec(p["k_gain"].shape, const),
            pl.BlockSpec((tm, LANES), lambda i, j: (j, 0)),
            pl.BlockSpec((tm, LANES), lambda i, j: (j, 0)),
        ],
        out_specs=[tok(w) for w in widths],
        out_shape=[jax.ShapeDtypeStruct((bsz, length, w), BF16) for w in widths],
        compiler_params=_cparams("parallel", "parallel"),
    )(h, mod, p["w_in"], p["gmean"], p["q_gain"], p["k_gain"], p["cos"], p["sin"])


def _gqa_kernel(q_ref, kt_ref, v_ref, o_ref):
    tq = q_ref.shape[0]
    sub = tq // GQA_SPLIT
    lane = lax.broadcasted_iota(jnp.int32, (sub, LANES), 1)

    def attend(h, rows):
        s = _dot(q_ref[rows, h * LANES:(h + 1) * LANES], kt_ref[...])
        p = jnp.exp2(s - jnp.max(s, axis=-1, keepdims=True)).astype(BF16)
        o = _dot(p, v_ref[...])
        return o[:, :LANES] / o[:, LANES:LANES + 1]

    for pair in range(C_HEADS // 2):
        for part in range(GQA_SPLIT):
            rows = slice(part * sub, (part + 1) * sub)
            a, b = attend(2 * pair, rows), attend(2 * pair + 1, rows)
            if (2 * pair) // C_GROUP == 0:
                blk = jnp.where(lane < HEAD_DIM, a, pltpu.roll(b, HEAD_DIM, 1))
            else:
                blk = jnp.where(lane < HEAD_DIM, pltpu.roll(a, HEAD_DIM, 1), b)
            o_ref[rows, pair * LANES:(pair + 1) * LANES] = blk.astype(BF16)


def _gqa(q_ext, k_all, v_all):
    bsz, length, qw = q_ext.shape
    lk = k_all.shape[1]
    assert C_KV_HEADS * HEAD_DIM == LANES and C_KV_HEADS == 2
    kt = jnp.swapaxes(k_all, 1, 2)
    v_ext = jnp.concatenate([v_all, jnp.ones_like(v_all)], axis=2)
    tq = _row_tile(length, 256)
    ow = C_HEADS * HEAD_DIM
    return pl.pallas_call(
        _gqa_kernel,
        grid=(bsz, length // tq),
        in_specs=[
            pl.BlockSpec((None, tq, qw), lambda i, j: (i, j, 0)),
            pl.BlockSpec((None, LANES, lk), lambda i, j: (i, 0, 0)),
            pl.BlockSpec((None, lk, 2 * LANES), lambda i, j: (i, 0, 0)),
        ],
        out_specs=pl.BlockSpec((None, tq, ow), lambda i, j: (i, j, 0)),
        out_shape=jax.ShapeDtypeStruct((bsz, length, ow), BF16),
        compiler_params=_cparams("parallel", "parallel"),
    )(q_ext, kt, v_ext)


def _na_block_start(qb, rows):
    return jnp.clip(qb * NA_QROWS - NA_WIN_H // 2, 0, rows - NA_KROWS)


def _na_bias_tables(rel_bias, rows):
    nblk = rows // NA_QROWS
    n_dr, n_dc = 2 * NA_WIN_H - 1, 2 * NA_WIN_W - 1
    i = np.arange(NA_QROWS)[:, None]
    a = np.arange(NA_KROWS)[None, :]
    j = np.arange(GRID_W)[:, None]
    kc = np.arange(GRID_W)[None, :]
    col_start = np.clip(j - NA_WIN_W // 2, 0, GRID_W - NA_WIN_W)
    valid_col = (kc >= col_start) & (kc < col_start + NA_WIN_W)
    dc = np.clip(kc - j + (NA_WIN_W - 1), 0, n_dc - 1)
    onehot_c = (dc[:, :, None] == np.arange(n_dc)).astype(np.float32)
    onehot_r, valid = [], []
    for qb in (0, 1, nblk - 1):
        r = qb * NA_QROWS + i
        r0 = np.clip(r - NA_WIN_H // 2, 0, rows - NA_WIN_H)
        kr = int(np.clip(qb * NA_QROWS - NA_WIN_H // 2, 0, rows - NA_KROWS)) + a
        valid_row = (kr >= r0) & (kr < r0 + NA_WIN_H)
        dr = np.clip(kr - r + (NA_WIN_H - 1), 0, n_dr - 1)
        onehot_r.append((dr[:, :, None] == np.arange(n_dr)).astype(np.float32))
        valid.append(valid_row[:, None, :, None] & valid_col[None, :, None, :])
    onehot_r = jnp.asarray(np.stack(onehot_r))
    valid = np.stack(valid)
    hp = lax.Precision.HIGHEST
    by_row = jnp.einsum("hrc,ziar->zhiac", rel_bias.astype(F32), onehot_r, precision=hp)
    table = jnp.einsum("zhiac,jkc->zhijak", by_row, jnp.asarray(onehot_c), precision=hp)
    table = jnp.where(valid[:, None], table * LOG2E, MASK_VALUE)
    return table.reshape(3 * rel_bias.shape[0], NA_QROWS * GRID_W, NA_KROWS * GRID_W)


def _na_kernel(q_ref, k_ref, v_ref, kc_ref, vc_ref, bias_ref, o_ref, *, rows):
    qb = pl.program_id(1)
    nk = NA_KROWS * GRID_W
    start = pl.multiple_of(_na_block_start(qb, rows) * GRID_W, GRID_W)
    k_all = jnp.concatenate([k_ref[pl.ds(start, nk), :], kc_ref[...]], axis=0)
    v_all = jnp.concatenate([v_ref[pl.ds(start, nk), :], vc_ref[...]], axis=0)
    ones = jnp.ones((k_all.shape[0], LANES), BF16)
    lane = lax.broadcasted_iota(jnp.int32, (q_ref.shape[0], LANES), 1)

    def attend(h, k_pair, v_pair):
        s = _dot_nt(q_ref[:, h * LANES:(h + 1) * LANES], k_pair)
        s = jnp.concatenate([s[:, :nk] + bias_ref[h], s[:, nk:]], axis=1)
        p = jnp.exp2(s - jnp.max(s, axis=-1, keepdims=True)).astype(BF16)
        o = _dot(p, v_pair)
        return o[:, :LANES] / o[:, LANES:LANES + 1]

    for pair in range(D_HEADS // 2):
        cols = slice(pair * LANES, (pair + 1) * LANES)
        k_pair = k_all[:, cols]
        v_pair = jnp.concatenate([v_all[:, cols], ones], axis=1)
        even, odd = attend(2 * pair, k_pair, v_pair), attend(2 * pair + 1, k_pair, v_pair)
        o_ref[:, cols] = jnp.where(lane < HEAD_DIM, even, odd).astype(BF16)


def _neighbourhood(dq, dk, dv, dk_c, dv_c, bias):
    bsz, length, w = dk.shape
    rows = length // GRID_W
    assert rows % NA_QROWS == 0 and rows >= NA_KROWS
    nblk = rows // NA_QROWS
    tq = NA_QROWS * GRID_W
    cl = dk_c.shape[1]

    def bias_class(i, j):
        cls = jnp.where(j == 0, 0, jnp.where(j == nblk - 1, 2, 1))
        return (cls, 0, 0)

    return pl.pallas_call(
        functools.partial(_na_kernel, rows=rows),
        grid=(bsz, nblk),
        in_specs=[
            pl.BlockSpec((None, tq, dq.shape[2]), lambda i, j: (i, j, 0)),
            pl.BlockSpec((None, length, w), lambda i, j: (i, 0, 0)),
            pl.BlockSpec((None, length, w), lambda i, j: (i, 0, 0)),
            pl.BlockSpec((None, cl, w), lambda i, j: (i, 0, 0)),
            pl.BlockSpec((None, cl, w), lambda i, j: (i, 0, 0)),
            pl.BlockSpec((D_HEADS, tq, NA_KROWS * GRID_W), bias_class),
        ],
        out_specs=pl.BlockSpec((None, tq, w), lambda i, j: (i, j, 0)),
        out_shape=jax.ShapeDtypeStruct((bsz, length, w), BF16),
        compiler_params=_cparams("parallel", "arbitrary"),
    )(dq, dk, dv, dk_c, dv_c, bias)


def _router_kernel(a_ref, b_ref, wo_ref, h_ref, mod_ref, wr_ref, h1_ref, n_ref, sel_ref):
    h1 = _mixer_residual(a_ref, b_ref, wo_ref, h_ref, mod_ref)
    h1_ref[...] = h1
    n = _modulate(h1, mod_ref[3:4, :], mod_ref[4:5, :])
    n_hi = n.astype(BF16)
    n_lo = (n - n_hi.astype(F32)).astype(BF16)
    w = wr_ref[...]
    w_hi = w.astype(BF16)
    w_lo = (w - w_hi.astype(F32)).astype(BF16)
    logits = _dot(n_hi, w_hi) + (_dot(n_lo, w_hi) + _dot(n_hi, w_lo))
    lane = lax.broadcasted_iota(jnp.int32, logits.shape, 1)
    logits = jnp.where(lane < N_EXPERTS, logits, -jnp.inf)
    m1 = jnp.max(logits, axis=-1, keepdims=True)
    i1 = jnp.min(jnp.where(logits == m1, lane, LANES), axis=-1, keepdims=True)
    rest = jnp.where(lane == i1, -jnp.inf, logits)
    m2 = jnp.max(rest, axis=-1, keepdims=True)
    i2 = jnp.min(jnp.where(rest == m2, lane, LANES), axis=-1, keepdims=True)
    e2 = jnp.exp(m2 - m1)
    w1 = 1.0 / (1.0 + e2)
    w2 = e2 / (1.0 + e2)
    sel = jnp.where(lane == 0, i1.astype(F32), jnp.where(lane == 1, i2.astype(F32),
                    jnp.where(lane == 2, w1, jnp.where(lane == 3, w2, 0.0))))
    sel_ref[...] = sel
    n_ref[...] = n_hi


def _mixer_out_router(a, b, w_out, h, mod, w_router_padded):
    bsz, length, d = h.shape
    ka, kb = a.shape[2], b.shape[2]
    tm = _row_tile(length, 1024)
    tok = lambda w: pl.BlockSpec((None, tm, w), lambda i, j: (i, j, 0))
    return pl.pallas_call(
        _router_kernel,
        grid=(bsz, length // tm),
        in_specs=[
            tok(ka), tok(kb),
            pl.BlockSpec((ka + kb, d), lambda i, j: (0, 0)),
            tok(d),
            pl.BlockSpec((None, 8, d), lambda i, j: (i, 0, 0)),
            pl.BlockSpec((d, LANES), lambda i, j: (0, 0)),
        ],
        out_specs=[tok(d), tok(d), tok(LANES)],
        out_shape=[
            jax.ShapeDtypeStruct((bsz, length, d), F32),
            jax.ShapeDtypeStruct((bsz, length, d), BF16),
            jax.ShapeDtypeStruct((bsz, length, LANES), F32),
        ],
        compiler_params=_cparams("parallel", "parallel"),
    )(a, b, w_out, h, mod, w_router_padded)


def _moe_dispatch_kernel(pos_ref, w_ref, n_ref, xs_ref, gs_ref):
    nrow = xs_ref.shape[0]
    base = pl.program_id(1) * nrow
    row = base + lax.broadcasted_iota(jnp.int32, (nrow, 1), 0)
    eq0 = pos_ref[0:1, :] == row
    eq1 = pos_ref[1:2, :] == row
    perm = jnp.where(eq0, 1.0, jnp.where(eq1, 1.0, 0.0)).astype(BF16)
    xs_ref[...] = _dot(perm, n_ref[...]).astype(BF16)
    gate = jnp.where(eq0, w_ref[0:1, :], jnp.where(eq1, w_ref[1:2, :], 0.0))
    gs_ref[...] = jnp.sum(gate, axis=-1, keepdims=True)


def _moe_expert_kernel(piece_ref, exp_ref, nused_ref, *refs):
    xs_refs, gs_refs = refs[:MOE_GROUP], refs[MOE_GROUP:2 * MOE_GROUP]
    wg_ref, wu_ref, wd_ref, ys_ref = refs[2 * MOE_GROUP:]
    i = pl.program_id(0)

    @pl.when(i < nused_ref[0])
    def _():
        x = jnp.concatenate([r[...] for r in xs_refs], axis=0)
        gate = jnp.concatenate([r[...] for r in gs_refs], axis=0)
        ff = wg_ref.shape[1]
        y = jnp.zeros((x.shape[0], wd_ref.shape[1]), F32)
        for f in range(0, ff, MOE_FF_CHUNK):
            cols = slice(f, min(f + MOE_FF_CHUNK, ff))
            a = jax.nn.silu(_dot(x, wg_ref[:, cols])) * _dot(x, wu_ref[:, cols])
            y = y + _dot(a.astype(BF16), wd_ref[cols, :])
        ys_ref[...] = (y * gate).astype(BF16)

    @pl.when(i >= nused_ref[0])
    def _():
        ys_ref[...] = jnp.zeros_like(ys_ref)


def _moe_combine_kernel(slot_ref, pos_ref, h_ref, mod_ref, fg_ref, *refs):
    ys = jnp.concatenate([r[...] for r in refs[:-1]], axis=0)
    o_ref = refs[-1]
    ntok, nrow = pos_ref.shape[0], ys.shape[0]
    row = lax.broadcasted_iota(jnp.int32, (ntok, nrow), 1)
    perm = jnp.where(pos_ref[:, 0:1] == row, 1.0, jnp.where(pos_ref[:, 1:2] == row, 1.0, 0.0)).astype(BF16)
    y = h_ref[...] + mod_ref[5:6, :] * _dot(perm, ys)
    ms = jnp.mean(y * y, axis=-1, keepdims=True)
    o_ref[...] = y * lax.rsqrt(ms + EPS) * fg_ref[...]


def _moe_plan(sel, n_tiles, tile_tokens, rows_per_tile):
    pieces_per_tile = rows_per_tile // MOE_PIECE
    experts = sel[:, :TOP_K].astype(jnp.int32).reshape(n_tiles, tile_tokens * TOP_K)
    weights = sel[:, TOP_K:2 * TOP_K].reshape(n_tiles, tile_tokens, TOP_K)
    onehot = (experts[:, :, None] == jnp.arange(N_EXPERTS, dtype=jnp.int32)).astype(jnp.int32)
    csum = jnp.cumsum(onehot, axis=1)
    rank = jnp.sum((csum - onehot) * onehot, axis=-1)
    count = csum[:, -1, :]
    npiece = (count + MOE_PIECE - 1) // MOE_PIECE
    piece_end = jnp.cumsum(npiece, axis=1)
    piece_off = piece_end - npiece
    pos = jnp.sum(onehot * piece_off[:, None, :], axis=-1) * MOE_PIECE + rank
    pos = pos.reshape(n_tiles, tile_tokens, TOP_K)

    b = jnp.arange(pieces_per_tile, dtype=jnp.int32)
    piece_expert = jnp.sum((piece_end[:, None, :] <= b[None, :, None]).astype(jnp.int32), axis=-1)
    flat_expert = piece_expert.reshape(-1)
    n_pieces = flat_expert.shape[0]
    n_slots = n_pieces + N_EXPERTS * MOE_GROUP
    classes = jnp.arange(N_EXPERTS + 1, dtype=jnp.int32)
    cls_onehot = (flat_expert[:, None] == classes).astype(jnp.int32)
    cls_csum = jnp.cumsum(cls_onehot, axis=0)
    piece_rank = jnp.sum((cls_csum - cls_onehot) * cls_onehot, axis=-1)
    n_cls = cls_csum[-1]
    groups = (n_cls[:N_EXPERTS] + MOE_GROUP - 1) // MOE_GROUP
    start = (jnp.cumsum(groups) - groups) * MOE_GROUP
    n_used_groups = jnp.sum(groups)

    slot = jnp.arange(n_slots, dtype=jnp.int32)
    slot_expert = jnp.sum((start[None, :] <= slot[:, None]).astype(jnp.int32), axis=-1) - 1
    slot_taken = (slot - start[slot_expert]) < n_cls[slot_expert]
    free_slots = jnp.argsort(slot_taken.astype(jnp.int32) * n_slots + slot).astype(jnp.int32)
    is_used = flat_expert < N_EXPERTS
    safe_expert = jnp.minimum(flat_expert, N_EXPERTS - 1)
    piece_slot = jnp.where(is_used, start[safe_expert] + piece_rank, free_slots[piece_rank])
    first_unused = jnp.argmax(jnp.logical_not(is_used)).astype(jnp.int32)
    piece_ids = jnp.arange(n_pieces, dtype=jnp.int32)
    slot_piece = jnp.full((n_slots,), first_unused, jnp.int32).at[piece_slot].set(piece_ids, unique_indices=True)
    step_expert = slot_expert[::MOE_GROUP]
    return pos, weights, slot_piece, piece_slot.astype(jnp.int32), step_expert, n_used_groups.reshape(1)


def _moe_final(n2, sel, h, mod, final_gain, wg, wu, wd):
    bsz, length, d = h.shape
    n_exp, _, ff = wg.shape
    tokens = bsz * length
    tt = _row_tile(length, MOE_TOKENS)
    n_tiles = tokens // tt
    rows = TOP_K * tt + N_EXPERTS * MOE_PIECE
    drows = _divisor_tile(rows, MOE_DISPATCH_ROWS, MOE_PIECE)
    ctok = _row_tile(tt, MOE_COMBINE_TOKENS)
    pos, weights, slot_piece, piece_slot, step_expert, n_used = _moe_plan(
        sel.reshape(tokens, LANES), n_tiles, tt, rows)

    pos_rows = jnp.swapaxes(pos, 1, 2)
    w_rows = jnp.swapaxes(weights, 1, 2)
    xs, gs = pl.pallas_call(
        _moe_dispatch_kernel,
        grid=(n_tiles, rows // drows),
        in_specs=[
            pl.BlockSpec((None, TOP_K, tt), lambda t, r: (t, 0, 0)),
            pl.BlockSpec((None, TOP_K, tt), lambda t, r: (t, 0, 0)),
            pl.BlockSpec((tt, d), lambda t, r: (t, 0)),
        ],
        out_specs=[
            pl.BlockSpec((None, drows, d), lambda t, r: (t, r, 0)),
            pl.BlockSpec((None, drows, 1), lambda t, r: (t, r, 0)),
        ],
        out_shape=[
            jax.ShapeDtypeStruct((n_tiles, rows, d), BF16),
            jax.ShapeDtypeStruct((n_tiles, rows, 1), F32),
        ],
        compiler_params=_cparams("parallel", "parallel"),
    )(pos_rows, w_rows, n2.reshape(tokens, d))

    n_steps = slot_piece.shape[0] // MOE_GROUP
    resident = pl.Buffered(1)

    def piece_spec(j, width):
        return pl.BlockSpec((MOE_PIECE, width), lambda i, sp, ex, nu: (sp[i * MOE_GROUP + j], 0))

    def weight_spec(shape):
        return pl.BlockSpec((None,) + shape, lambda i, sp, ex, nu: (ex[i], 0, 0), pipeline_mode=resident)

    xs_flat, gs_flat = xs.reshape(n_tiles * rows, d), gs.reshape(n_tiles * rows, 1)
    step_rows = MOE_GROUP * MOE_PIECE
    ys = pl.pallas_call(
        _moe_expert_kernel,
        grid_spec=pltpu.PrefetchScalarGridSpec(
            num_scalar_prefetch=3,
            grid=(n_steps,),
            in_specs=([piece_spec(j, d) for j in range(MOE_GROUP)]
                      + [piece_spec(j, 1) for j in range(MOE_GROUP)]
                      + [weight_spec((d, ff)), weight_spec((d, ff)), weight_spec((ff, d))]),
            out_specs=pl.BlockSpec((step_rows, d), lambda i, sp, ex, nu: (i, 0)),
        ),
        out_shape=jax.ShapeDtypeStruct((n_steps * step_rows, d), BF16),
        compiler_params=_cparams("arbitrary"),
    )(slot_piece, step_expert, n_used, *([xs_flat] * MOE_GROUP), *([gs_flat] * MOE_GROUP), wg, wu, wd)

    tiles_per_seq = length // tt
    pieces_per_tile = rows // MOE_PIECE

    def tile_piece_spec(k):
        return pl.BlockSpec((MOE_PIECE, d), lambda t, c, slot: (slot[t * pieces_per_tile + k], 0))

    out = pl.pallas_call(
        _moe_combine_kernel,
        grid_spec=pltpu.PrefetchScalarGridSpec(
            num_scalar_prefetch=1,
            grid=(n_tiles, tt // ctok),
            in_specs=[
                pl.BlockSpec((None, ctok, TOP_K), lambda t, c, slot: (t, c, 0)),
                pl.BlockSpec((None, ctok, d), lambda t, c, slot: (t, c, 0)),
                pl.BlockSpec((None, 8, d), lambda t, c, slot: (t // tiles_per_seq, 0, 0)),
                pl.BlockSpec((1, d), lambda t, c, slot: (0, 0)),
            ] + [tile_piece_spec(k) for k in range(pieces_per_tile)],
            out_specs=pl.BlockSpec((None, ctok, d), lambda t, c, slot: (t, c, 0)),
        ),
        out_shape=jax.ShapeDtypeStruct((n_tiles, tt, d), F32),
        compiler_params=_cparams("parallel", "parallel"),
    )(piece_slot, pos, h.reshape(n_tiles, tt, d), mod, final_gain, *([ys] * pieces_per_tile))
    return out.reshape(bsz, length, d)


def _group_mean_matrix(width, group):
    return jnp.asarray(np.kron(np.eye(width // group), np.full((group, group), 1.0 / group)), BF16)


def _rope_tables(length):
    t = jnp.arange(length, dtype=jnp.int32)
    row = (t // GRID_W).astype(F32)
    col = (t % GRID_W).astype(F32)
    n_axis = HEAD_DIM // 4
    inv_freq = ROPE_THETA ** (-jnp.arange(n_axis, dtype=F32) / n_axis)
    ang = jnp.concatenate([row[:, None] * inv_freq, col[:, None] * inv_freq], axis=-1)
    cos = jnp.repeat(jnp.cos(ang), 2, axis=-1)
    sin = jnp.repeat(jnp.sin(ang), 2, axis=-1)
    sign = jnp.tile(jnp.asarray([-1.0, 1.0], F32), HEAD_DIM // 2)
    reps = LANES // HEAD_DIM
    return jnp.tile(cos, (1, reps)), jnp.tile(sin * sign, (1, reps))


def _spread_heads(w_q, upper_half):
    d = w_q.shape[0]
    n_heads = w_q.shape[1] // HEAD_DIM
    w = w_q.reshape(d, n_heads, HEAD_DIM)
    zeros = jnp.zeros_like(w)
    upper = jnp.asarray(upper_half, bool)[None, :, None]
    lo = jnp.where(upper, zeros, w)
    hi = jnp.where(upper, w, zeros)
    return jnp.concatenate([lo, hi], axis=2).reshape(d, n_heads * LANES)


def kernel(x, c, ctx, c_ctx, w_ada, b_ada, w_in_ab, v_gain, w_spatial, b_spatial, w_out_ab,
           w_gate_dense, w_up_dense, w_down_dense, w_in_cd, q_gain, k_gain, rel_bias, w_out_cd,
           w_router, w_gate_moe, w_up_moe, w_down_moe, final_gain):
    bsz, length, d = x.shape
    depth = w_ada.shape[0]
    assert depth == 2 and bsz <= 8
    aw = v_gain.shape[1]
    group_dim = aw // A_GROUPS
    bgd = (w_in_ab.shape[2] - 2 * aw) // B_GROUPS

    cond = jnp.zeros((16, d), F32).at[:bsz].set(c).at[8].set(c_ctx)
    ada = _adaln(cond, w_ada, b_ada).reshape(depth, 16, 6, d)
    pad = jnp.zeros((depth, 16, 2, d), F32)
    ada = jnp.concatenate([ada, pad], axis=2)
    mod_lat = [ada[i, :bsz] for i in range(depth)]
    mod_ctx = [jnp.broadcast_to(ada[i, 8], (bsz, 8, d)) for i in range(depth)]

    cmat, smat = _dft_tables(bgd, bgd ** -0.5)
    p_ab = dict(
        w_in=w_in_ab[0].astype(BF16),
        v_gain=v_gain[0].reshape(1, aw),
        gmean=_group_mean_matrix(aw, group_dim),
        w_sp=w_spatial[0].transpose(1, 0, 2).reshape(CHUNK, A_GROUPS * CHUNK).astype(BF16),
        b_sp=jnp.repeat(b_spatial[0].T, group_dim, axis=1),
        cmat=cmat, smat=smat,
        w_out=w_out_ab[0].astype(BF16),
    )
    wg0, wu0, wd0 = (w.astype(BF16) for w in (w_gate_dense[0], w_up_dense[0], w_down_dense[0]))
    h = _mixer_out_dense_ffn(*_mix_ab(x, mod_lat[0], p_ab), p_ab["w_out"], x, mod_lat[0], wg0, wu0, wd0)
    hc = _mixer_out_dense_ffn(*_mix_ab(ctx, mod_ctx[0], p_ab), p_ab["w_out"], ctx, mod_ctx[0], wg0, wu0, wd0)

    cos, sin = _rope_tables(length)
    qw = C_HEADS * HEAD_DIM
    kw = C_KV_HEADS * HEAD_DIM
    w_cd = w_in_cd[0]
    dq0, dq1 = qw + 2 * kw, qw + 2 * kw + D_HEADS * HEAD_DIM
    p_cd = dict(
        w_in=jnp.concatenate([
            _spread_heads(w_cd[:, :qw], [h // C_GROUP == 1 for h in range(C_HEADS)]),
            w_cd[:, qw:dq0],
            _spread_heads(w_cd[:, dq0:dq1], [h % 2 == 1 for h in range(D_HEADS)]),
            w_cd[:, dq1:]], axis=1).astype(BF16),
        gmean=_group_mean_matrix(kw, HEAD_DIM),
        q_gain=jnp.tile(q_gain[0], C_HEADS * LANES // HEAD_DIM).reshape(1, C_HEADS * LANES),
        k_gain=jnp.tile(k_gain[0], C_KV_HEADS).reshape(1, kw),
        cos=cos, sin=sin,
    )
    cq, ck, cv, dq, dk, dv = _project_cd(h, mod_lat[1], p_cd, rope=True)
    p_cd_ctx = dict(p_cd, cos=cos[:hc.shape[1]], sin=sin[:hc.shape[1]])
    _, ck_c, cv_c, _, dk_c, dv_c = _project_cd(hc, mod_ctx[1], p_cd_ctx, rope=False)
    o_c = _gqa(cq, jnp.concatenate([ck_c, ck], axis=1), jnp.concatenate([cv_c, cv], axis=1))
    o_d = _neighbourhood(dq, dk, dv, dk_c, dv_c, _na_bias_tables(rel_bias[0], length // GRID_W))
    wr = jnp.zeros((d, LANES), F32).at[:, :N_EXPERTS].set(w_router[0])
    h, n2, sel = _mixer_out_router(o_c, o_d, w_out_cd[0].astype(BF16), h, mod_lat[1], wr)
    return _moe_final(n2, sel, h, mod_lat[1], final_gain.reshape(1, d),
                      w_gate_moe[0].astype(BF16), w_up_moe[0].astype(BF16), w_down_moe[0].astype(BF16))
```

```python
import functools
import math

import numpy as np
import jax
import jax.numpy as jnp
from jax import lax
from jax.experimental import pallas as pl
from jax.experimental.pallas import tpu as pltpu

F32 = jnp.float32
BF16 = jnp.bfloat16

GRID_W = 64
EPS = 1e-6
CHUNK = 128
A_GROUPS = 8
B_GROUPS = 4
HEAD_DIM = 64
C_HEADS = 8
C_KV_HEADS = 2
C_GROUP = C_HEADS // C_KV_HEADS
D_HEADS = 8
NA_WIN_H = 8
NA_WIN_W = 16
ROPE_THETA = 10000.0
N_EXPERTS = 8
TOP_K = 2

LANES = 128
GQA_SPLIT = 2
NA_QROWS = 4
NA_KROWS = NA_QROWS + NA_WIN_H
MASK_VALUE = -1e30
LOG2E = 1.4426950408889634
VMEM_LIMIT = 56 * 1024 * 1024

MOE_TOKENS = 1024
MOE_PIECE = 32
MOE_GROUP = 8
MOE_DISPATCH_ROWS = 1280
MOE_COMBINE_TOKENS = 512
MOE_FF_CHUNK = 1792


def _cparams(*sem):
    return pltpu.CompilerParams(dimension_semantics=sem, vmem_limit_bytes=VMEM_LIMIT)


def _modulate(h, shift, scale):
    ms = jnp.mean(h * h, axis=-1, keepdims=True)
    return h * lax.rsqrt(ms + EPS) * (1.0 + scale) + shift


def _dot(a, b):
    return jnp.dot(a, b, preferred_element_type=F32)


def _dot_nt(a, b):
    return lax.dot_general(a, b, (((1,), (1,)), ((), ())), preferred_element_type=F32)


def _row_tile(length, target):
    t = min(length, target)
    assert length % t == 0
    return t


def _adaln_kernel(c_ref, w_ref, b_ref, o_ref):
    s = jax.nn.silu(c_ref[...]).astype(BF16)
    o_ref[...] = _dot(s, w_ref[...].astype(BF16)) + b_ref[...]


def _adaln(cond, w_ada, b_ada):
    depth, d, n = w_ada.shape
    r = cond.shape[0]
    tn = 512
    return pl.pallas_call(
        _adaln_kernel,
        grid=(depth, n // tn),
        in_specs=[
            pl.BlockSpec((r, d), lambda i, j: (0, 0)),
            pl.BlockSpec((None, d, tn), lambda i, j: (i, 0, j)),
            pl.BlockSpec((None, 1, tn), lambda i, j: (i, 0, j)),
        ],
        out_specs=pl.BlockSpec((None, r, tn), lambda i, j: (i, 0, j)),
        out_shape=jax.ShapeDtypeStruct((depth, r, n), F32),
        compiler_params=_cparams("parallel", "parallel"),
    )(cond, w_ada, b_ada.reshape(depth, 1, n))


def _ab_in_kernel(h_ref, mod_ref, w_ref, vg_ref, gm_ref, wsp_ref, bsp_ref, cm_ref, sm_ref,
                  a_ref, fc_ref, fs_ref, *, aw):
    tm = h_ref.shape[0]
    n = _modulate(h_ref[...], mod_ref[0:1, :], mod_ref[1:2, :])
    z = _dot(n.astype(BF16), w_ref[...])
    u = jax.nn.gelu(z[:, :aw])
    v = jax.nn.gelu(z[:, aw:2 * aw])
    fb = z[:, 2 * aw:].astype(BF16)
    ms = _dot((v * v).astype(BF16), gm_ref[...])
    vn = v * lax.rsqrt(ms + EPS) * vg_ref[...]
    group_dim = aw // A_GROUPS
    lane_group = lax.broadcasted_iota(jnp.int32, (CHUNK, aw), 1) // group_dim
    for c in range(tm // CHUNK):
        rows = slice(c * CHUNK, (c + 1) * CHUNK)
        vc = vn[rows, :]
        stack = jnp.concatenate(
            [jnp.where(lane_group == g, vc, 0.0).astype(BF16) for g in range(A_GROUPS)], axis=0)
        s = _dot(wsp_ref[...], stack) + bsp_ref[...]
        a_ref[rows, :] = (u[rows, :] * s).astype(BF16)
    bw = fb.shape[1] // B_GROUPS
    for g in range(B_GROUPS):
        cols = slice(g * bw, (g + 1) * bw)
        fc_ref[:, cols] = _dot(fb[:, cols], cm_ref[...]).astype(BF16)
        fs_ref[:, cols] = _dot(fb[:, cols], sm_ref[...]).astype(BF16)


def _seq_dft_kernel(c_ref, s_ref, fc_ref, fs_ref, o_ref):
    o_ref[...] = (_dot(c_ref[...], fc_ref[...]) - _dot(s_ref[...], fs_ref[...])).astype(BF16)


def _proj_residual_kernel(a_ref, b_ref, w_ref, h_ref, mod_ref, o_ref):
    ka = a_ref.shape[1]
    y = _dot(a_ref[...], w_ref[:ka, :]) + _dot(b_ref[...], w_ref[ka:, :])
    o_ref[...] = h_ref[...] + mod_ref[2:3, :] * y


def _proj_residual(a, b, w, h, mod):
    bsz, length, d = h.shape
    tm = _row_tile(length, 1024)
    ka, kb = a.shape[2], b.shape[2]
    return pl.pallas_call(
        _proj_residual_kernel,
        grid=(bsz, length // tm),
        in_specs=[
            pl.BlockSpec((None, tm, ka), lambda i, j: (i, j, 0)),
            pl.BlockSpec((None, tm, kb), lambda i, j: (i, j, 0)),
            pl.BlockSpec((ka + kb, d), lambda i, j: (0, 0)),
            pl.BlockSpec((None, tm, d), lambda i, j: (i, j, 0)),
            pl.BlockSpec((None, 8, d), lambda i, j: (i, 0, 0)),
        ],
        out_specs=pl.BlockSpec((None, tm, d), lambda i, j: (i, j, 0)),
        out_shape=jax.ShapeDtypeStruct((bsz, length, d), F32),
        compiler_params=_cparams("parallel", "parallel"),
    )(a, b, w, h, mod)


def _dft_tables(n, scale):
    k = jnp.arange(n, dtype=jnp.int32)
    ang = ((k[:, None] * k[None, :]) % n).astype(F32) * (2.0 * math.pi / n)
    return (jnp.cos(ang) * scale).astype(BF16), (jnp.sin(ang) * scale).astype(BF16)


def _mix_ab(h, mod, p):
    bsz, length, d = h.shape
    aw = p["v_gain"].shape[1]
    bw_total = p["w_in"].shape[1] - 2 * aw
    const = lambda i, j: (0, 0)
    tm = _row_tile(length, 512)
    f_spec = pl.BlockSpec((None, tm, bw_total), lambda i, j: (i, j, 0))
    f_shape = jax.ShapeDtypeStruct((bsz, length, bw_total), BF16)
    a_out, fc, fs = pl.pallas_call(
        functools.partial(_ab_in_kernel, aw=aw),
        grid=(bsz, length // tm),
        in_specs=[
            pl.BlockSpec((None, tm, d), lambda i, j: (i, j, 0)),
            pl.BlockSpec((None, 8, d), lambda i, j: (i, 0, 0)),
            pl.BlockSpec(p["w_in"].shape, const),
            pl.BlockSpec(p["v_gain"].shape, const),
            pl.BlockSpec(p["gmean"].shape, const),
            pl.BlockSpec(p["w_sp"].shape, const),
            pl.BlockSpec(p["b_sp"].shape, const),
            pl.BlockSpec(p["cmat"].shape, const),
            pl.BlockSpec(p["smat"].shape, const),
        ],
        out_specs=[pl.BlockSpec((None, tm, aw), lambda i, j: (i, j, 0)), f_spec, f_spec],
        out_shape=[jax.ShapeDtypeStruct((bsz, length, aw), BF16), f_shape, f_shape],
        compiler_params=_cparams("parallel", "parallel"),
    )(h, mod, p["w_in"], p["v_gain"], p["gmean"], p["w_sp"], p["b_sp"], p["cmat"], p["smat"])
    return a_out, _seq_dft(fc, fs)


def _seq_dft_direct(fc, fs):
    bsz, length, width = fc.shape
    cl, sl = _dft_tables(length, length ** -0.5)
    tk = _row_tile(length, 512)
    return pl.pallas_call(
        _seq_dft_kernel,
        grid=(length // tk, bsz),
        in_specs=[
            pl.BlockSpec((tk, length), lambda k, b: (k, 0)),
            pl.BlockSpec((tk, length), lambda k, b: (k, 0)),
            pl.BlockSpec((None, length, width), lambda k, b: (b, 0, 0)),
            pl.BlockSpec((None, length, width), lambda k, b: (b, 0, 0)),
        ],
        out_specs=pl.BlockSpec((None, tk, width), lambda k, b: (b, k, 0)),
        out_shape=jax.ShapeDtypeStruct((bsz, length, width), BF16),
        compiler_params=_cparams("parallel", "parallel"),
    )(cl, sl, fc, fs)


def _fft_rows_kernel(fc_ref, fs_ref, ma_ref, tc_ref, ts_ref, yr_ref, yi_ref):
    n1 = fc_ref.shape[0]
    y = _dot(ma_ref[...], jnp.concatenate([fc_ref[...], fs_ref[...]], axis=0))
    yr, yi = y[:n1, :], y[n1:, :]
    c, s = tc_ref[...], ts_ref[...]
    yr_ref[...] = (yr * c + yi * s).astype(BF16)
    yi_ref[...] = (yi * c - yr * s).astype(BF16)


def _fft_cols_kernel(yr_ref, yi_ref, mb_ref, o_ref):
    width = yr_ref.shape[2]
    for j in range(yr_ref.shape[0]):
        y = jnp.concatenate([yr_ref[j], yi_ref[j]], axis=0)
        o_ref[:, j * width:(j + 1) * width] = _dot(mb_ref[...], y).astype(BF16)


def _seq_dft(fc, fs):
    bsz, length, width = fc.shape
    n2 = GRID_W
    n1 = length // n2
    if n1 < 16 or length % n2:
        return _seq_dft_direct(fc, fs)
    ncol = n2 * width

    def angles(a, b, period):
        prod = (jnp.arange(a, dtype=jnp.int32)[:, None] * jnp.arange(b, dtype=jnp.int32)[None, :]) % period
        return prod.astype(F32) * (2.0 * math.pi / period)

    th = angles(n1, n1, n1)
    c1, s1 = jnp.cos(th) * n1 ** -0.5, jnp.sin(th) * n1 ** -0.5
    ma = jnp.concatenate([jnp.concatenate([c1, -s1], axis=1),
                          jnp.concatenate([-s1, -c1], axis=1)], axis=0).astype(BF16)
    ph = angles(n1, n2, length)
    tc = jnp.repeat(jnp.cos(ph), width, axis=1)
    ts = jnp.repeat(jnp.sin(ph), width, axis=1)
    ps = angles(n2, n2, n2)
    mb = (jnp.concatenate([jnp.cos(ps), jnp.sin(ps)], axis=1) * n2 ** -0.5).astype(BF16)

    cols = _divisor_tile(ncol, 4096)
    row_blk = lambda b, j: (b, 0, j)
    yr, yi = pl.pallas_call(
        _fft_rows_kernel,
        grid=(bsz, ncol // cols),
        in_specs=[
            pl.BlockSpec((None, n1, cols), row_blk),
            pl.BlockSpec((None, n1, cols), row_blk),
            pl.BlockSpec((2 * n1, 2 * n1), lambda b, j: (0, 0)),
            pl.BlockSpec((n1, cols), lambda b, j: (0, j)),
            pl.BlockSpec((n1, cols), lambda b, j: (0, j)),
        ],
        out_specs=[pl.BlockSpec((None, n1, cols), row_blk)] * 2,
        out_shape=[jax.ShapeDtypeStruct((bsz, n1, ncol), BF16)] * 2,
        compiler_params=_cparams("parallel", "parallel"),
    )(fc.reshape(bsz, n1, ncol), fs.reshape(bsz, n1, ncol), ma, tc, ts)

    kb = _divisor_tile(n1, 8, 1)
    out = pl.pallas_call(
        _fft_cols_kernel,
        grid=(bsz, n1 // kb),
        in_specs=[
            pl.BlockSpec((None, kb, n2, width), lambda b, k: (b, k, 0, 0)),
            pl.BlockSpec((None, kb, n2, width), lambda b, k: (b, k, 0, 0)),
            pl.BlockSpec((n2, 2 * n2), lambda b, k: (0, 0)),
        ],
        out_specs=pl.BlockSpec((None, n2, kb * width), lambda b, k: (b, 0, k)),
        out_shape=jax.ShapeDtypeStruct((bsz, n2, n1 * width), BF16),
        compiler_params=_cparams("parallel", "parallel"),
    )(yr.reshape(bsz, n1, n2, width), yi.reshape(bsz, n1, n2, width), mb)
    return out.reshape(bsz, length, width)


def _dense_ffn_kernel(h_ref, mod_ref, wg_ref, wu_ref, wd_ref, o_ref, n_sc, acc_sc):
    f = pl.program_id(2)

    @pl.when(f == 0)
    def _():
        n_sc[...] = _modulate(h_ref[...], mod_ref[3:4, :], mod_ref[4:5, :]).astype(BF16)
        acc_sc[...] = jnp.zeros_like(acc_sc)

    n = n_sc[...]
    a = jax.nn.silu(_dot(n, wg_ref[...])) * _dot(n, wu_ref[...])
    acc_sc[...] += _dot(a.astype(BF16), wd_ref[...])

    @pl.when(f == pl.num_programs(2) - 1)
    def _():
        o_ref[...] = h_ref[...] + mod_ref[5:6, :] * acc_sc[...]


def _divisor_tile(n, target, unit=LANES):
    best = None
    for t in range(unit, min(n, target) + 1, unit):
        if n % t == 0:
            best = t
    assert best is not None
    return best


def _dense_ffn(h, mod, wg, wu, wd):
    bsz, length, d = h.shape
    ff = wg.shape[1]
    tm = _row_tile(length, 1024)
    tf = _divisor_tile(ff, 1408)
    tok = pl.BlockSpec((None, tm, d), lambda i, j, f: (i, j, 0))
    return pl.pallas_call(
        _dense_ffn_kernel,
        grid=(bsz, length // tm, ff // tf),
        in_specs=[
            tok,
            pl.BlockSpec((None, 8, d), lambda i, j, f: (i, 0, 0)),
            pl.BlockSpec((d, tf), lambda i, j, f: (0, f)),
            pl.BlockSpec((d, tf), lambda i, j, f: (0, f)),
            pl.BlockSpec((tf, d), lambda i, j, f: (f, 0)),
        ],
        out_specs=tok,
        out_shape=jax.ShapeDtypeStruct((bsz, length, d), F32),
        scratch_shapes=[pltpu.VMEM((tm, d), BF16), pltpu.VMEM((tm, d), F32)],
        compiler_params=_cparams("parallel", "parallel", "arbitrary"),
    )(h, mod, wg, wu, wd)


def _pair_swap(x):
    width = x.shape[-1]
    lane = lax.broadcasted_iota(jnp.int32, x.shape, x.ndim - 1)
    nxt = pltpu.roll(x, width - 1, x.ndim - 1)
    prv = pltpu.roll(x, 1, x.ndim - 1)
    return jnp.where(lane % 2 == 0, nxt, prv)


def _rope(x, cos, sin):
    reps = x.shape[1] // LANES
    cos = jnp.concatenate([cos] * reps, axis=1) if reps > 1 else cos
    sin = jnp.concatenate([sin] * reps, axis=1) if reps > 1 else sin
    return x * cos + _pair_swap(x) * sin


def _cd_in_kernel(h_ref, mod_ref, w_ref, gm_ref, qg_ref, kg_ref, cos_ref, sin_ref,
                  cq_ref, ck_ref, cv_ref, dq_ref, dk_ref, dv_ref, *, rope):
    n = _modulate(h_ref[...], mod_ref[0:1, :], mod_ref[1:2, :])
    z = _dot(n.astype(BF16), w_ref[...])
    qw, kw, dqw, dw = cq_ref.shape[1], ck_ref.shape[1], dq_ref.shape[1], dk_ref.shape[1]
    o1, o2, o3, o4, o5 = qw, qw + kw, qw + 2 * kw, qw + 2 * kw + dqw, qw + 2 * kw + dqw + dw
    scale = HEAD_DIM ** -0.5

    zq = z[:, :o1]
    blocks = []
    for h in range(C_HEADS):
        t = zq[:, h * LANES:(h + 1) * LANES]
        ms = jnp.sum(t * t, axis=-1, keepdims=True) * (1.0 / HEAD_DIM)
        blocks.append(t * lax.rsqrt(ms + EPS))
    cq = jnp.concatenate(blocks, axis=1) * qg_ref[...]
    zk = z[:, o1:o2]
    ck = zk * lax.rsqrt(_dot((zk * zk).astype(BF16), gm_ref[...]) + EPS) * kg_ref[...]
    if rope:
        cq = _rope(cq, cos_ref[...], sin_ref[...])
        ck = _rope(ck, cos_ref[...], sin_ref[...])
    scale = scale * LOG2E
    cq_ref[...] = (cq * scale).astype(BF16)
    ck_ref[...] = ck.astype(BF16)
    cv_ref[...] = z[:, o2:o3].astype(BF16)
    dq_ref[...] = (z[:, o3:o4] * scale).astype(BF16)
    dk_ref[...] = z[:, o4:o5].astype(BF16)
    dv_ref[...] = z[:, o5:].astype(BF16)


def _project_cd(h, mod, p, rope):
    bsz, length, d = h.shape
    qw, kw, dw = C_HEADS * LANES, C_KV_HEADS * HEAD_DIM, D_HEADS * HEAD_DIM
    tm = _row_tile(length, 512)
    const = lambda i, j: (0, 0)
    tok = lambda w: pl.BlockSpec((None, tm, w), lambda i, j: (i, j, 0))
    widths = (qw, kw, kw, D_HEADS * LANES, dw, dw)
    return pl.pallas_call(
        functools.partial(_cd_in_kernel, rope=rope),
        grid=(bsz, length // tm),
        in_specs=[
            tok(d),
            pl.BlockSpec((None, 8, d), lambda i, j: (i, 0, 0)),
            pl.BlockSpec(p["w_in"].shape, const),
            pl.BlockSpec(p["gmean"].shape, const),
            pl.BlockSpec(p["q_gain"].shape, const),
            pl.BlockSpec(p["k_gain"].shape, const),
            pl.BlockSpec((tm, LANES), lambda i, j: (j, 0)),
            pl.BlockSpec((tm, LANES), lambda i, j: (j, 0)),
        ],
        out_specs=[tok(w) for w in widths],
        out_shape=[jax.ShapeDtypeStruct((bsz, length, w), BF16) for w in widths],
        compiler_params=_cparams("parallel", "parallel"),
    )(h, mod, p["w_in"], p["gmean"], p["q_gain"], p["k_gain"], p["cos"], p["sin"])


def _gqa_kernel(q_ref, kt_ref, v_ref, o_ref):
    tq = q_ref.shape[0]
    sub = tq // GQA_SPLIT
    lane = lax.broadcasted_iota(jnp.int32, (sub, LANES), 1)

    def attend(h, rows):
        s = _dot(q_ref[rows, h * LANES:(h + 1) * LANES], kt_ref[...])
        p = jnp.exp2(s - jnp.max(s, axis=-1, keepdims=True)).astype(BF16)
        o = _dot(p, v_ref[...])
        return o[:, :LANES] / o[:, LANES:LANES + 1]

    for pair in range(C_HEADS // 2):
        for part in range(GQA_SPLIT):
            rows = slice(part * sub, (part + 1) * sub)
            a, b = attend(2 * pair, rows), attend(2 * pair + 1, rows)
            if (2 * pair) // C_GROUP == 0:
                blk = jnp.where(lane < HEAD_DIM, a, pltpu.roll(b, HEAD_DIM, 1))
            else:
                blk = jnp.where(lane < HEAD_DIM, pltpu.roll(a, HEAD_DIM, 1), b)
            o_ref[rows, pair * LANES:(pair + 1) * LANES] = blk.astype(BF16)


def _gqa(q_ext, k_all, v_all):
    bsz, length, qw = q_ext.shape
    lk = k_all.shape[1]
    assert C_KV_HEADS * HEAD_DIM == LANES and C_KV_HEADS == 2
    kt = jnp.swapaxes(k_all, 1, 2)
    v_ext = jnp.concatenate([v_all, jnp.ones_like(v_all)], axis=2)
    tq = _row_tile(length, 256)
    ow = C_HEADS * HEAD_DIM
    return pl.pallas_call(
        _gqa_kernel,
        grid=(bsz, length // tq),
        in_specs=[
            pl.BlockSpec((None, tq, qw), lambda i, j: (i, j, 0)),
            pl.BlockSpec((None, LANES, lk), lambda i, j: (i, 0, 0)),
            pl.BlockSpec((None, lk, 2 * LANES), lambda i, j: (i, 0, 0)),
        ],
        out_specs=pl.BlockSpec((None, tq, ow), lambda i, j: (i, j, 0)),
        out_shape=jax.ShapeDtypeStruct((bsz, length, ow), BF16),
        compiler_params=_cparams("parallel", "parallel"),
    )(q_ext, kt, v_ext)


def _na_block_start(qb, rows):
    return jnp.clip(qb * NA_QROWS - NA_WIN_H // 2, 0, rows - NA_KROWS)


def _na_bias_tables(rel_bias, rows):
    nblk = rows // NA_QROWS
    n_dr, n_dc = 2 * NA_WIN_H - 1, 2 * NA_WIN_W - 1
    i = np.arange(NA_QROWS)[:, None]
    a = np.arange(NA_KROWS)[None, :]
    j = np.arange(GRID_W)[:, None]
    kc = np.arange(GRID_W)[None, :]
    col_start = np.clip(j - NA_WIN_W // 2, 0, GRID_W - NA_WIN_W)
    valid_col = (kc >= col_start) & (kc < col_start + NA_WIN_W)
    dc = np.clip(kc - j + (NA_WIN_W - 1), 0, n_dc - 1)
    onehot_c = (dc[:, :, None] == np.arange(n_dc)).astype(np.float32)
    onehot_r, valid = [], []
    for qb in (0, 1, nblk - 1):
        r = qb * NA_QROWS + i
        r0 = np.clip(r - NA_WIN_H // 2, 0, rows - NA_WIN_H)
        kr = int(np.clip(qb * NA_QROWS - NA_WIN_H // 2, 0, rows - NA_KROWS)) + a
        valid_row = (kr >= r0) & (kr < r0 + NA_WIN_H)
        dr = np.clip(kr - r + (NA_WIN_H - 1), 0, n_dr - 1)
        onehot_r.append((dr[:, :, None] == np.arange(n_dr)).astype(np.float32))
        valid.append(valid_row[:, None, :, None] & valid_col[None, :, None, :])
    onehot_r = jnp.asarray(np.stack(onehot_r))
    valid = np.stack(valid)
    hp = lax.Precision.HIGHEST
    by_row = jnp.einsum("hrc,ziar->zhiac", rel_bias.astype(F32), onehot_r, precision=hp)
    table = jnp.einsum("zhiac,jkc->zhijak", by_row, jnp.asarray(onehot_c), precision=hp)
    table = jnp.where(valid[:, None], table * LOG2E, MASK_VALUE)
    return table.reshape(3 * rel_bias.shape[0], NA_QROWS * GRID_W, NA_KROWS * GRID_W)


def _na_kernel(q_ref, k_ref, v_ref, kc_ref, vc_ref, bias_ref, o_ref, *, rows):
    qb = pl.program_id(1)
    nk = NA_KROWS * GRID_W
    start = pl.multiple_of(_na_block_start(qb, rows) * GRID_W, GRID_W)
    k_all = jnp.concatenate([k_ref[pl.ds(start, nk), :], kc_ref[...]], axis=0)
    v_all = jnp.concatenate([v_ref[pl.ds(start, nk), :], vc_ref[...]], axis=0)
    ones = jnp.ones((k_all.shape[0], LANES), BF16)
    lane = lax.broadcasted_iota(jnp.int32, (q_ref.shape[0], LANES), 1)

    def attend(h, k_pair, v_pair):
        s = _dot_nt(q_ref[:, h * LANES:(h + 1) * LANES], k_pair)
        s = jnp.concatenate([s[:, :nk] + bias_ref[h], s[:, nk:]], axis=1)
        p = jnp.exp2(s - jnp.max(s, axis=-1, keepdims=True)).astype(BF16)
        o = _dot(p, v_pair)
        return o[:, :LANES] / o[:, LANES:LANES + 1]

    for pair in range(D_HEADS // 2):
        cols = slice(pair * LANES, (pair + 1) * LANES)
        k_pair = k_all[:, cols]
        v_pair = jnp.concatenate([v_all[:, cols], ones], axis=1)
        even, odd = attend(2 * pair, k_pair, v_pair), attend(2 * pair + 1, k_pair, v_pair)
        o_ref[:, cols] = jnp.where(lane < HEAD_DIM, even, odd).astype(BF16)


def _neighbourhood(dq, dk, dv, dk_c, dv_c, bias):
    bsz, length, w = dk.shape
    rows = length // GRID_W
    assert rows % NA_QROWS == 0 and rows >= NA_KROWS
    nblk = rows // NA_QROWS
    tq = NA_QROWS * GRID_W
    cl = dk_c.shape[1]

    def bias_class(i, j):
        cls = jnp.where(j == 0, 0, jnp.where(j == nblk - 1, 2, 1))
        return (cls, 0, 0)

    return pl.pallas_call(
        functools.partial(_na_kernel, rows=rows),
        grid=(bsz, nblk),
        in_specs=[
            pl.BlockSpec((None, tq, dq.shape[2]), lambda i, j: (i, j, 0)),
            pl.BlockSpec((None, length, w), lambda i, j: (i, 0, 0)),
            pl.BlockSpec((None, length, w), lambda i, j: (i, 0, 0)),
            pl.BlockSpec((None, cl, w), lambda i, j: (i, 0, 0)),
            pl.BlockSpec((None, cl, w), lambda i, j: (i, 0, 0)),
            pl.BlockSpec((D_HEADS, tq, NA_KROWS * GRID_W), bias_class),
        ],
        out_specs=pl.BlockSpec((None, tq, w), lambda i, j: (i, j, 0)),
        out_shape=jax.ShapeDtypeStruct((bsz, length, w), BF16),
        compiler_params=_cparams("parallel", "arbitrary"),
    )(dq, dk, dv, dk_c, dv_c, bias)


def _router_kernel(h_ref, mod_ref, wr_ref, n_ref, sel_ref, cnt_ref):
    n = _modulate(h_ref[...], mod_ref[3:4, :], mod_ref[4:5, :])
    n_hi = n.astype(BF16)
    n_lo = (n - n_hi.astype(F32)).astype(BF16)
    w = wr_ref[...]
    w_hi = w.astype(BF16)
    w_lo = (w - w_hi.astype(F32)).astype(BF16)
    logits = _dot(n_hi, w_hi) + (_dot(n_lo, w_hi) + _dot(n_hi, w_lo))
    lane = lax.broadcasted_iota(jnp.int32, logits.shape, 1)
    logits = jnp.where(lane < N_EXPERTS, logits, -jnp.inf)
    m1 = jnp.max(logits, axis=-1, keepdims=True)
    i1 = jnp.min(jnp.where(logits == m1, lane, LANES), axis=-1, keepdims=True)
    rest = jnp.where(lane == i1, -jnp.inf, logits)
    m2 = jnp.max(rest, axis=-1, keepdims=True)
    i2 = jnp.min(jnp.where(rest == m2, lane, LANES), axis=-1, keepdims=True)
    e2 = jnp.exp(m2 - m1)
    w1 = 1.0 / (1.0 + e2)
    w2 = e2 / (1.0 + e2)

    ntok = logits.shape[0]
    chosen = jnp.where(lane == i1, 1.0, jnp.where(lane == i2, 1.0, 0.0))
    earlier = (lax.broadcasted_iota(jnp.int32, (ntok, ntok), 1)
               < lax.broadcasted_iota(jnp.int32, (ntok, ntok), 0))
    before = _dot(jnp.where(earlier, 1.0, 0.0).astype(BF16), chosen.astype(BF16))
    count = jnp.sum(chosen, axis=0, keepdims=True).astype(jnp.int32)
    npiece = lax.shift_right_logical(count + (MOE_PIECE - 1), jnp.full_like(count, MOE_PIECE.bit_length() - 1))
    lower = (lax.broadcasted_iota(jnp.int32, (LANES, LANES), 0)
             < lax.broadcasted_iota(jnp.int32, (LANES, LANES), 1))
    npiece_rows = jnp.broadcast_to(npiece.astype(F32), (8, LANES)).astype(BF16)
    piece_off = _dot(npiece_rows, jnp.where(lower, 1.0, 0.0).astype(BF16))[0:1, :]
    row = piece_off * float(MOE_PIECE) + before
    pos1 = jnp.sum(jnp.where(lane == i1, row, 0.0), axis=-1, keepdims=True)
    pos2 = jnp.sum(jnp.where(lane == i2, row, 0.0), axis=-1, keepdims=True)

    sel = jnp.where(lane == 0, i1.astype(F32), jnp.where(lane == 1, i2.astype(F32),
                    jnp.where(lane == 2, w1, jnp.where(lane == 3, w2,
                              jnp.where(lane == 4, pos1, jnp.where(lane == 5, pos2, 0.0))))))
    sel_ref[...] = sel
    cnt_ref[...] = jnp.broadcast_to(npiece.astype(F32), cnt_ref.shape)
    n_ref[...] = n_hi


def _router(h, mod, w_router_padded):
    bsz, length, d = h.shape
    tm = _row_tile(length, MOE_TOKENS)
    tiles = length // tm
    tok = lambda w: pl.BlockSpec((None, tm, w), lambda i, j: (i, j, 0))
    return pl.pallas_call(
        _router_kernel,
        grid=(bsz, tiles),
        in_specs=[
            tok(d),
            pl.BlockSpec((None, 8, d), lambda i, j: (i, 0, 0)),
            pl.BlockSpec((d, LANES), lambda i, j: (0, 0)),
        ],
        out_specs=[tok(d), tok(LANES), pl.BlockSpec((None, None, 8, LANES), lambda i, j: (i, j, 0, 0))],
        out_shape=[
            jax.ShapeDtypeStruct((bsz, length, d), BF16),
            jax.ShapeDtypeStruct((bsz, length, LANES), F32),
            jax.ShapeDtypeStruct((bsz, tiles, 8, LANES), F32),
        ],
        compiler_params=_cparams("parallel", "parallel"),
    )(h, mod, w_router_padded)


def _moe_dispatch_kernel(pos_ref, w_ref, n_ref, xs_ref, gs_ref):
    nrow = xs_ref.shape[0]
    base = pl.program_id(1) * nrow
    row = base + lax.broadcasted_iota(jnp.int32, (nrow, 1), 0)
    eq0 = pos_ref[0:1, :] == row
    eq1 = pos_ref[1:2, :] == row
    perm = jnp.where(eq0, 1.0, jnp.where(eq1, 1.0, 0.0)).astype(BF16)
    xs_ref[...] = _dot(perm, n_ref[...]).astype(BF16)
    gate = jnp.where(eq0, w_ref[0:1, :], jnp.where(eq1, w_ref[1:2, :], 0.0))
    gs_ref[...] = jnp.sum(gate, axis=-1, keepdims=True)


def _moe_expert_kernel(piece_ref, exp_ref, nused_ref, *refs):
    xs_refs, gs_refs = refs[:MOE_GROUP], refs[MOE_GROUP:2 * MOE_GROUP]
    wg_ref, wu_ref, wd_ref, ys_ref = refs[2 * MOE_GROUP:]
    i = pl.program_id(0)

    @pl.when(i < nused_ref[0])
    def _():
        x = jnp.concatenate([r[...] for r in xs_refs], axis=0)
        gate = jnp.concatenate([r[...] for r in gs_refs], axis=0)
        ff = wg_ref.shape[1]
        y = jnp.zeros((x.shape[0], wd_ref.shape[1]), F32)
        for f in range(0, ff, MOE_FF_CHUNK):
            cols = slice(f, min(f + MOE_FF_CHUNK, ff))
            a = jax.nn.silu(_dot(x, wg_ref[:, cols])) * _dot(x, wu_ref[:, cols])
            y = y + _dot(a.astype(BF16), wd_ref[cols, :])
        ys_ref[...] = (y * gate).astype(BF16)

    @pl.when(i >= nused_ref[0])
    def _():
        ys_ref[...] = jnp.zeros_like(ys_ref)


def _moe_combine_kernel(slot_ref, pos_ref, h_ref, mod_ref, fg_ref, *refs):
    ys = jnp.concatenate([r[...] for r in refs[:-1]], axis=0)
    o_ref = refs[-1]
    ntok, nrow = pos_ref.shape[0], ys.shape[0]
    row = lax.broadcasted_iota(jnp.int32, (ntok, nrow), 1)
    perm = jnp.where(pos_ref[:, 0:1] == row, 1.0, jnp.where(pos_ref[:, 1:2] == row, 1.0, 0.0)).astype(BF16)
    y = h_ref[...] + mod_ref[5:6, :] * _dot(perm, ys)
    ms = jnp.mean(y * y, axis=-1, keepdims=True)
    o_ref[...] = y * lax.rsqrt(ms + EPS) * fg_ref[...]


def _moe_plan(sel, npiece, n_tiles, tile_tokens, rows_per_tile):
    pieces_per_tile = rows_per_tile // MOE_PIECE
    weights = sel[:, TOP_K:2 * TOP_K].reshape(n_tiles, tile_tokens, TOP_K)
    pos = sel[:, 2 * TOP_K:3 * TOP_K].astype(jnp.int32).reshape(n_tiles, tile_tokens, TOP_K)
    piece_end = jnp.cumsum(npiece.astype(jnp.int32), axis=1)

    b = jnp.arange(pieces_per_tile, dtype=jnp.int32)
    piece_expert = jnp.sum((piece_end[:, None, :] <= b[None, :, None]).astype(jnp.int32), axis=-1)
    flat_expert = piece_expert.reshape(-1)
    n_pieces = flat_expert.shape[0]
    n_slots = n_pieces + N_EXPERTS * MOE_GROUP
    classes = jnp.arange(N_EXPERTS + 1, dtype=jnp.int32)
    cls_onehot = (flat_expert[:, None] == classes).astype(jnp.int32)
    cls_csum = jnp.cumsum(cls_onehot, axis=0)
    piece_rank = jnp.sum((cls_csum - cls_onehot) * cls_onehot, axis=-1)
    n_cls = cls_csum[-1]
    groups = (n_cls[:N_EXPERTS] + MOE_GROUP - 1) // MOE_GROUP
    start = (jnp.cumsum(groups) - groups) * MOE_GROUP
    n_used_groups = jnp.sum(groups)

    slot = jnp.arange(n_slots, dtype=jnp.int32)
    slot_expert = jnp.sum((start[None, :] <= slot[:, None]).astype(jnp.int32), axis=-1) - 1
    slot_taken = (slot - start[slot_expert]) < n_cls[slot_expert]
    free_slots = jnp.argsort(slot_taken.astype(jnp.int32) * n_slots + slot).astype(jnp.int32)
    is_used = flat_expert < N_EXPERTS
    safe_expert = jnp.minimum(flat_expert, N_EXPERTS - 1)
    piece_slot = jnp.where(is_used, start[safe_expert] + piece_rank, free_slots[piece_rank])
    first_unused = jnp.argmax(jnp.logical_not(is_used)).astype(jnp.int32)
    piece_ids = jnp.arange(n_pieces, dtype=jnp.int32)
    slot_piece = jnp.full((n_slots,), first_unused, jnp.int32).at[piece_slot].set(piece_ids, unique_indices=True)
    step_expert = slot_expert[::MOE_GROUP]
    return pos, weights, slot_piece, piece_slot.astype(jnp.int32), step_expert, n_used_groups.reshape(1)


def _moe_final(n2, sel, npiece, h, mod, final_gain, wg, wu, wd):
    bsz, length, d = h.shape
    n_exp, _, ff = wg.shape
    tokens = bsz * length
    tt = _row_tile(length, MOE_TOKENS)
    n_tiles = tokens // tt
    rows = TOP_K * tt + N_EXPERTS * MOE_PIECE
    drows = _divisor_tile(rows, MOE_DISPATCH_ROWS, MOE_PIECE)
    ctok = _row_tile(tt, MOE_COMBINE_TOKENS)
    pos, weights, slot_piece, piece_slot, step_expert, n_used = _moe_plan(
        sel.reshape(tokens, LANES), npiece.reshape(n_tiles, 8, LANES)[:, 0, :N_EXPERTS], n_tiles, tt, rows)

    pos_rows = jnp.swapaxes(pos, 1, 2)
    w_rows = jnp.swapaxes(weights, 1, 2)
    xs, gs = pl.pallas_call(
        _moe_dispatch_kernel,
        grid=(n_tiles, rows // drows),
        in_specs=[
            pl.BlockSpec((None, TOP_K, tt), lambda t, r: (t, 0, 0)),
            pl.BlockSpec((None, TOP_K, tt), lambda t, r: (t, 0, 0)),
            pl.BlockSpec((tt, d), lambda t, r: (t, 0)),
        ],
        out_specs=[
            pl.BlockSpec((None, drows, d), lambda t, r: (t, r, 0)),
            pl.BlockSpec((None, drows, 1), lambda t, r: (t, r, 0)),
        ],
        out_shape=[
            jax.ShapeDtypeStruct((n_tiles, rows, d), BF16),
            jax.ShapeDtypeStruct((n_tiles, rows, 1), F32),
        ],
        compiler_params=_cparams("parallel", "parallel"),
    )(pos_rows, w_rows, n2.reshape(tokens, d))

    n_steps = slot_piece.shape[0] // MOE_GROUP
    resident = pl.Buffered(1)

    def piece_spec(j, width):
        return pl.BlockSpec((MOE_PIECE, width), lambda i, sp, ex, nu: (sp[i * MOE_GROUP + j], 0))

    def weight_spec(shape):
        return pl.BlockSpec((None,) + shape, lambda i, sp, ex, nu: (ex[i], 0, 0), pipeline_mode=resident)

    xs_flat, gs_flat = xs.reshape(n_tiles * rows, d), gs.reshape(n_tiles * rows, 1)
    step_rows = MOE_GROUP * MOE_PIECE
    ys = pl.pallas_call(
        _moe_expert_kernel,
        grid_spec=pltpu.PrefetchScalarGridSpec(
            num_scalar_prefetch=3,
            grid=(n_steps,),
            in_specs=([piece_spec(j, d) for j in range(MOE_GROUP)]
                      + [piece_spec(j, 1) for j in range(MOE_GROUP)]
                      + [weight_spec((d, ff)), weight_spec((d, ff)), weight_spec((ff, d))]),
            out_specs=pl.BlockSpec((step_rows, d), lambda i, sp, ex, nu: (i, 0)),
        ),
        out_shape=jax.ShapeDtypeStruct((n_steps * step_rows, d), BF16),
        compiler_params=_cparams("arbitrary"),
    )(slot_piece, step_expert, n_used, *([xs_flat] * MOE_GROUP), *([gs_flat] * MOE_GROUP), wg, wu, wd)

    tiles_per_seq = length // tt
    pieces_per_tile = rows // MOE_PIECE

    def tile_piece_spec(k):
        return pl.BlockSpec((MOE_PIECE, d), lambda t, c, slot: (slot[t * pieces_per_tile + k], 0))

    out = pl.pallas_call(
        _moe_combine_kernel,
        grid_spec=pltpu.PrefetchScalarGridSpec(
            num_scalar_prefetch=1,
            grid=(n_tiles, tt // ctok),
            in_specs=[
                pl.BlockSpec((None, ctok, TOP_K), lambda t, c, slot: (t, c, 0)),
                pl.BlockSpec((None, ctok, d), lambda t, c, slot: (t, c, 0)),
                pl.BlockSpec((None, 8, d), lambda t, c, slot: (t // tiles_per_seq, 0, 0)),
                pl.BlockSpec((1, d), lambda t, c, slot: (0, 0)),
            ] + [tile_piece_spec(k) for k in range(pieces_per_tile)],
            out_specs=pl.BlockSpec((None, ctok, d), lambda t, c, slot: (t, c, 0)),
        ),
        out_shape=jax.ShapeDtypeStruct((n_tiles, tt, d), F32),
        compiler_params=_cparams("parallel", "parallel"),
    )(piece_slot, pos, h.reshape(n_tiles, tt, d), mod, final_gain, *([ys] * pieces_per_tile))
    return out.reshape(bsz, length, d)


def _group_mean_matrix(width, group):
    return jnp.asarray(np.kron(np.eye(width // group), np.full((group, group), 1.0 / group)), BF16)


def _rope_tables(length):
    t = jnp.arange(length, dtype=jnp.int32)
    row = (t // GRID_W).astype(F32)
    col = (t % GRID_W).astype(F32)
    n_axis = HEAD_DIM // 4
    inv_freq = ROPE_THETA ** (-jnp.arange(n_axis, dtype=F32) / n_axis)
    ang = jnp.concatenate([row[:, None] * inv_freq, col[:, None] * inv_freq], axis=-1)
    cos = jnp.repeat(jnp.cos(ang), 2, axis=-1)
    sin = jnp.repeat(jnp.sin(ang), 2, axis=-1)
    sign = jnp.tile(jnp.asarray([-1.0, 1.0], F32), HEAD_DIM // 2)
    reps = LANES // HEAD_DIM
    return jnp.tile(cos, (1, reps)), jnp.tile(sin * sign, (1, reps))


def _spread_heads(w_q, upper_half):
    d = w_q.shape[0]
    n_heads = w_q.shape[1] // HEAD_DIM
    w = w_q.reshape(d, n_heads, HEAD_DIM)
    zeros = jnp.zeros_like(w)
    upper = jnp.asarray(upper_half, bool)[None, :, None]
    lo = jnp.where(upper, zeros, w)
    hi = jnp.where(upper, w, zeros)
    return jnp.concatenate([lo, hi], axis=2).reshape(d, n_heads * LANES)


def kernel(x, c, ctx, c_ctx, w_ada, b_ada, w_in_ab, v_gain, w_spatial, b_spatial, w_out_ab,
           w_gate_dense, w_up_dense, w_down_dense, w_in_cd, q_gain, k_gain, rel_bias, w_out_cd,
           w_router, w_gate_moe, w_up_moe, w_down_moe, final_gain):
    bsz, length, d = x.shape
    depth = w_ada.shape[0]
    assert depth == 2 and bsz <= 8
    aw = v_gain.shape[1]
    group_dim = aw // A_GROUPS
    bgd = (w_in_ab.shape[2] - 2 * aw) // B_GROUPS

    cond = jnp.zeros((16, d), F32).at[:bsz].set(c).at[8].set(c_ctx)
    ada = _adaln(cond, w_ada, b_ada).reshape(depth, 16, 6, d)
    pad = jnp.zeros((depth, 16, 2, d), F32)
    ada = jnp.concatenate([ada, pad], axis=2)
    mod_lat = [ada[i, :bsz] for i in range(depth)]
    mod_ctx = [jnp.broadcast_to(ada[i, 8], (bsz, 8, d)) for i in range(depth)]

    cmat, smat = _dft_tables(bgd, bgd ** -0.5)
    p_ab = dict(
        w_in=w_in_ab[0].astype(BF16),
        v_gain=v_gain[0].reshape(1, aw),
        gmean=_group_mean_matrix(aw, group_dim),
        w_sp=w_spatial[0].transpose(1, 0, 2).reshape(CHUNK, A_GROUPS * CHUNK).astype(BF16),
        b_sp=jnp.repeat(b_spatial[0].T, group_dim, axis=1),
        cmat=cmat, smat=smat,
        w_out=w_out_ab[0].astype(BF16),
    )
    wg0, wu0, wd0 = (w.astype(BF16) for w in (w_gate_dense[0], w_up_dense[0], w_down_dense[0]))
    h = _proj_residual(*_mix_ab(x, mod_lat[0], p_ab), p_ab["w_out"], x, mod_lat[0])
    hc = _proj_residual(*_mix_ab(ctx, mod_ctx[0], p_ab), p_ab["w_out"], ctx, mod_ctx[0])
    h = _dense_ffn(h, mod_lat[0], wg0, wu0, wd0)
    hc = _dense_ffn(hc, mod_ctx[0], wg0, wu0, wd0)

    cos, sin = _rope_tables(length)
    qw = C_HEADS * HEAD_DIM
    kw = C_KV_HEADS * HEAD_DIM
    w_cd = w_in_cd[0]
    dq0, dq1 = qw + 2 * kw, qw + 2 * kw + D_HEADS * HEAD_DIM
    p_cd = dict(
        w_in=jnp.concatenate([
            _spread_heads(w_cd[:, :qw], [h // C_GROUP == 1 for h in range(C_HEADS)]),
            w_cd[:, qw:dq0],
            _spread_heads(w_cd[:, dq0:dq1], [h % 2 == 1 for h in range(D_HEADS)]),
            w_cd[:, dq1:]], axis=1).astype(BF16),
        gmean=_group_mean_matrix(kw, HEAD_DIM),
        q_gain=jnp.tile(q_gain[0], C_HEADS * LANES // HEAD_DIM).reshape(1, C_HEADS * LANES),
        k_gain=jnp.tile(k_gain[0], C_KV_HEADS).reshape(1, kw),
        cos=cos, sin=sin,
    )
    cq, ck, cv, dq, dk, dv = _project_cd(h, mod_lat[1], p_cd, rope=True)
    p_cd_ctx = dict(p_cd, cos=cos[:hc.shape[1]], sin=sin[:hc.shape[1]])
    _, ck_c, cv_c, _, dk_c, dv_c = _project_cd(hc, mod_ctx[1], p_cd_ctx, rope=False)
    o_c = _gqa(cq, jnp.concatenate([ck_c, ck], axis=1), jnp.concatenate([cv_c, cv], axis=1))
    o_d = _neighbourhood(dq, dk, dv, dk_c, dv_c, _na_bias_tables(rel_bias[0], length // GRID_W))
    wr = jnp.zeros((d, LANES), F32).at[:, :N_EXPERTS].set(w_router[0])
    h = _proj_residual(o_c, o_d, w_out_cd[0].astype(BF16), h, mod_lat[1])
    n2, sel, npiece = _router(h, mod_lat[1], wr)
    return _moe_final(n2, sel, npiece, h, mod_lat[1], final_gain.reshape(1, d),
                      w_gate_moe[0].astype(BF16), w_up_moe[0].astype(BF16), w_down_moe[0].astype(BF16))
```

```python
import functools
import math

import numpy as np
import jax
import jax.numpy as jnp
from jax import lax
from jax.experimental import pallas as pl
from jax.experimental.pallas import tpu as pltpu

F32 = jnp.float32
BF16 = jnp.bfloat16

GRID_W = 64
EPS = 1e-6
CHUNK = 128
A_GROUPS = 8
B_GROUPS = 4
HEAD_DIM = 64
C_HEADS = 8
C_KV_HEADS = 2
C_GROUP = C_HEADS // C_KV_HEADS
D_HEADS = 8
NA_WIN_H = 8
NA_WIN_W = 16
ROPE_THETA = 10000.0
N_EXPERTS = 8
TOP_K = 2

LANES = 128
GQA_SPLIT = 4
NA_QROWS = 4
NA_SPLIT = 2
NA_KROWS = NA_QROWS + NA_WIN_H
MASK_VALUE = -1e30
LOG2E = 1.4426950408889634
VMEM_LIMIT = 56 * 1024 * 1024

DENSE_FF_CHUNK = 1536
MOE_TOKENS = 1024
MOE_PIECE = 32
MOE_GROUP = 8
MOE_DISPATCH_ROWS = 1280
MOE_COMBINE_TOKENS = 1024
MOE_COMBINE_SPLIT = 4
MOE_FF_CHUNK = 1792


def _cparams(*sem):
    return pltpu.CompilerParams(dimension_semantics=sem, vmem_limit_bytes=VMEM_LIMIT)


def _modulate(h, shift, scale):
    ms = jnp.mean(h * h, axis=-1, keepdims=True)
    return h * lax.rsqrt(ms + EPS) * (1.0 + scale) + shift


def _dot(a, b):
    return jnp.dot(a, b, preferred_element_type=F32)


def _dot_nt(a, b):
    return lax.dot_general(a, b, (((1,), (1,)), ((), ())), preferred_element_type=F32)


def _row_tile(length, target):
    t = min(length, target)
    assert length % t == 0
    return t


def _adaln_kernel(c_ref, w_ref, b_ref, o_ref):
    s = jax.nn.silu(c_ref[...]).astype(BF16)
    o_ref[...] = _dot(s, w_ref[...].astype(BF16)) + b_ref[...]


def _adaln(cond, w_ada, b_ada):
    depth, d, n = w_ada.shape
    r = cond.shape[0]
    tn = 512
    return pl.pallas_call(
        _adaln_kernel,
        grid=(depth, n // tn),
        in_specs=[
            pl.BlockSpec((r, d), lambda i, j: (0, 0)),
            pl.BlockSpec((None, d, tn), lambda i, j: (i, 0, j)),
            pl.BlockSpec((None, 1, tn), lambda i, j: (i, 0, j)),
        ],
        out_specs=pl.BlockSpec((None, r, tn), lambda i, j: (i, 0, j)),
        out_shape=jax.ShapeDtypeStruct((depth, r, n), F32),
        compiler_params=_cparams("parallel", "parallel"),
    )(cond, w_ada, b_ada.reshape(depth, 1, n))


def _ab_in_kernel(h_ref, mod_ref, w_ref, vg_ref, gm_ref, wsp_ref, bsp_ref, cm_ref, sm_ref,
                  a_ref, fc_ref, fs_ref, *, aw):
    tm = h_ref.shape[0]
    n = _modulate(h_ref[...], mod_ref[0:1, :], mod_ref[1:2, :])
    z = _dot(n.astype(BF16), w_ref[...])
    u = jax.nn.gelu(z[:, :aw])
    v = jax.nn.gelu(z[:, aw:2 * aw])
    fb = z[:, 2 * aw:].astype(BF16)
    ms = _dot((v * v).astype(BF16), gm_ref[...])
    vn = v * lax.rsqrt(ms + EPS) * vg_ref[...]
    group_dim = aw // A_GROUPS
    lane_group = lax.broadcasted_iota(jnp.int32, (CHUNK, aw), 1) // group_dim
    for c in range(tm // CHUNK):
        rows = slice(c * CHUNK, (c + 1) * CHUNK)
        vc = vn[rows, :]
        stack = jnp.concatenate(
            [jnp.where(lane_group == g, vc, 0.0).astype(BF16) for g in range(A_GROUPS)], axis=0)
        s = _dot(wsp_ref[...], stack) + bsp_ref[...]
        a_ref[rows, :] = (u[rows, :] * s).astype(BF16)
    bw = fb.shape[1] // B_GROUPS
    for g in range(B_GROUPS):
        cols = slice(g * bw, (g + 1) * bw)
        fc_ref[:, cols] = _dot(fb[:, cols], cm_ref[...]).astype(BF16)
        fs_ref[:, cols] = _dot(fb[:, cols], sm_ref[...]).astype(BF16)


def _seq_dft_kernel(c_ref, s_ref, fc_ref, fs_ref, o_ref):
    o_ref[...] = (_dot(c_ref[...], fc_ref[...]) - _dot(s_ref[...], fs_ref[...])).astype(BF16)


def _proj_residual_kernel(a_ref, b_ref, w_ref, h_ref, mod_ref, o_ref):
    ka = a_ref.shape[1]
    y = _dot(a_ref[...], w_ref[:ka, :]) + _dot(b_ref[...], w_ref[ka:, :])
    o_ref[...] = h_ref[...] + mod_ref[2:3, :] * y


def _proj_residual(a, b, w, h, mod):
    bsz, length, d = h.shape
    tm = _row_tile(length, 1024)
    ka, kb = a.shape[2], b.shape[2]
    return pl.pallas_call(
        _proj_residual_kernel,
        grid=(bsz, length // tm),
        in_specs=[
            pl.BlockSpec((None, tm, ka), lambda i, j: (i, j, 0)),
            pl.BlockSpec((None, tm, kb), lambda i, j: (i, j, 0)),
            pl.BlockSpec((ka + kb, d), lambda i, j: (0, 0)),
            pl.BlockSpec((None, tm, d), lambda i, j: (i, j, 0)),
            pl.BlockSpec((None, 8, d), lambda i, j: (i, 0, 0)),
        ],
        out_specs=pl.BlockSpec((None, tm, d), lambda i, j: (i, j, 0)),
        out_shape=jax.ShapeDtypeStruct((bsz, length, d), F32),
        compiler_params=_cparams("parallel", "parallel"),
    )(a, b, w, h, mod)


def _dft_tables(n, scale):
    k = jnp.arange(n, dtype=jnp.int32)
    ang = ((k[:, None] * k[None, :]) % n).astype(F32) * (2.0 * math.pi / n)
    return (jnp.cos(ang) * scale).astype(BF16), (jnp.sin(ang) * scale).astype(BF16)


def _mix_ab(h, mod, p):
    bsz, length, d = h.shape
    aw = p["v_gain"].shape[1]
    bw_total = p["w_in"].shape[1] - 2 * aw
    const = lambda i, j: (0, 0)
    tm = _row_tile(length, 512)
    f_spec = pl.BlockSpec((None, tm, bw_total), lambda i, j: (i, j, 0))
    f_shape = jax.ShapeDtypeStruct((bsz, length, bw_total), BF16)
    a_out, fc, fs = pl.pallas_call(
        functools.partial(_ab_in_kernel, aw=aw),
        grid=(bsz, length // tm),
        in_specs=[
            pl.BlockSpec((None, tm, d), lambda i, j: (i, j, 0)),
            pl.BlockSpec((None, 8, d), lambda i, j: (i, 0, 0)),
            pl.BlockSpec(p["w_in"].shape, const),
            pl.BlockSpec(p["v_gain"].shape, const),
            pl.BlockSpec(p["gmean"].shape, const),
            pl.BlockSpec(p["w_sp"].shape, const),
            pl.BlockSpec(p["b_sp"].shape, const),
            pl.BlockSpec(p["cmat"].shape, const),
            pl.BlockSpec(p["smat"].shape, const),
        ],
        out_specs=[pl.BlockSpec((None, tm, aw), lambda i, j: (i, j, 0)), f_spec, f_spec],
        out_shape=[jax.ShapeDtypeStruct((bsz, length, aw), BF16), f_shape, f_shape],
        compiler_params=_cparams("parallel", "parallel"),
    )(h, mod, p["w_in"], p["v_gain"], p["gmean"], p["w_sp"], p["b_sp"], p["cmat"], p["smat"])
    return a_out, _seq_dft(fc, fs)


def _seq_dft_direct(fc, fs):
    bsz, length, width = fc.shape
    cl, sl = _dft_tables(length, length ** -0.5)
    tk = _row_tile(length, 512)
    return pl.pallas_call(
        _seq_dft_kernel,
        grid=(length // tk, bsz),
        in_specs=[
            pl.BlockSpec((tk, length), lambda k, b: (k, 0)),
            pl.BlockSpec((tk, length), lambda k, b: (k, 0)),
            pl.BlockSpec((None, length, width), lambda k, b: (b, 0, 0)),
            pl.BlockSpec((None, length, width), lambda k, b: (b, 0, 0)),
        ],
        out_specs=pl.BlockSpec((None, tk, width), lambda k, b: (b, k, 0)),
        out_shape=jax.ShapeDtypeStruct((bsz, length, width), BF16),
        compiler_params=_cparams("parallel", "parallel"),
    )(cl, sl, fc, fs)


def _fft_rows_kernel(fc_ref, fs_ref, ma_ref, tc_ref, ts_ref, yr_ref, yi_ref):
    n1 = fc_ref.shape[0]
    y = _dot(ma_ref[...], jnp.concatenate([fc_ref[...], fs_ref[...]], axis=0))
    yr, yi = y[:n1, :], y[n1:, :]
    c, s = tc_ref[...], ts_ref[...]
    yr_ref[...] = (yr * c + yi * s).astype(BF16)
    yi_ref[...] = (yi * c - yr * s).astype(BF16)


def _fft_cols_kernel(yr_ref, yi_ref, mb_ref, o_ref):
    width = yr_ref.shape[2]
    for j in range(yr_ref.shape[0]):
        y = jnp.concatenate([yr_ref[j], yi_ref[j]], axis=0)
        o_ref[:, j * width:(j + 1) * width] = _dot(mb_ref[...], y).astype(BF16)


def _seq_dft(fc, fs):
    bsz, length, width = fc.shape
    n2 = GRID_W
    n1 = length // n2
    if n1 < 16 or length % n2:
        return _seq_dft_direct(fc, fs)
    ncol = n2 * width

    def angles(a, b, period):
        prod = (jnp.arange(a, dtype=jnp.int32)[:, None] * jnp.arange(b, dtype=jnp.int32)[None, :]) % period
        return prod.astype(F32) * (2.0 * math.pi / period)

    th = angles(n1, n1, n1)
    c1, s1 = jnp.cos(th) * n1 ** -0.5, jnp.sin(th) * n1 ** -0.5
    ma = jnp.concatenate([jnp.concatenate([c1, -s1], axis=1),
                          jnp.concatenate([-s1, -c1], axis=1)], axis=0).astype(BF16)
    ph = angles(n1, n2, length)
    tc = jnp.repeat(jnp.cos(ph), width, axis=1)
    ts = jnp.repeat(jnp.sin(ph), width, axis=1)
    ps = angles(n2, n2, n2)
    mb = (jnp.concatenate([jnp.cos(ps), jnp.sin(ps)], axis=1) * n2 ** -0.5).astype(BF16)

    cols = _divisor_tile(ncol, 4096)
    row_blk = lambda b, j: (b, 0, j)
    yr, yi = pl.pallas_call(
        _fft_rows_kernel,
        grid=(bsz, ncol // cols),
        in_specs=[
            pl.BlockSpec((None, n1, cols), row_blk),
            pl.BlockSpec((None, n1, cols), row_blk),
            pl.BlockSpec((2 * n1, 2 * n1), lambda b, j: (0, 0)),
            pl.BlockSpec((n1, cols), lambda b, j: (0, j)),
            pl.BlockSpec((n1, cols), lambda b, j: (0, j)),
        ],
        out_specs=[pl.BlockSpec((None, n1, cols), row_blk)] * 2,
        out_shape=[jax.ShapeDtypeStruct((bsz, n1, ncol), BF16)] * 2,
        compiler_params=_cparams("parallel", "parallel"),
    )(fc.reshape(bsz, n1, ncol), fs.reshape(bsz, n1, ncol), ma, tc, ts)

    kb = _divisor_tile(n1, 8, 1)
    out = pl.pallas_call(
        _fft_cols_kernel,
        grid=(bsz, n1 // kb),
        in_specs=[
            pl.BlockSpec((None, kb, n2, width), lambda b, k: (b, k, 0, 0)),
            pl.BlockSpec((None, kb, n2, width), lambda b, k: (b, k, 0, 0)),
            pl.BlockSpec((n2, 2 * n2), lambda b, k: (0, 0)),
        ],
        out_specs=pl.BlockSpec((None, n2, kb * width), lambda b, k: (b, 0, k)),
        out_shape=jax.ShapeDtypeStruct((bsz, n2, n1 * width), BF16),
        compiler_params=_cparams("parallel", "parallel"),
    )(yr.reshape(bsz, n1, n2, width), yi.reshape(bsz, n1, n2, width), mb)
    return out.reshape(bsz, length, width)


def _swiglu(x, wg_ref, wu_ref, wd_ref, chunk):
    ff = wg_ref.shape[1]
    y = jnp.zeros((x.shape[0], wd_ref.shape[1]), F32)
    for f in range(0, ff, chunk):
        cols = slice(f, min(f + chunk, ff))
        a = jax.nn.silu(_dot(x, wg_ref[:, cols])) * _dot(x, wu_ref[:, cols])
        y = y + _dot(a.astype(BF16), wd_ref[cols, :])
    return y


def _dense_ffn_kernel(h_ref, mod_ref, wg_ref, wu_ref, wd_ref, o_ref):
    h = h_ref[...]
    n = _modulate(h, mod_ref[3:4, :], mod_ref[4:5, :]).astype(BF16)
    o_ref[...] = h + mod_ref[5:6, :] * _swiglu(n, wg_ref, wu_ref, wd_ref, DENSE_FF_CHUNK)


def _divisor_tile(n, target, unit=LANES):
    best = None
    for t in range(unit, min(n, target) + 1, unit):
        if n % t == 0:
            best = t
    assert best is not None
    return best


def _dense_ffn(h, mod, wg, wu, wd):
    bsz, length, d = h.shape
    ff = wg.shape[1]
    tm = _row_tile(length, 512)
    tok = pl.BlockSpec((None, tm, d), lambda i, j: (i, j, 0))
    resident = pl.Buffered(1)
    return pl.pallas_call(
        _dense_ffn_kernel,
        grid=(bsz, length // tm),
        in_specs=[
            tok,
            pl.BlockSpec((None, 8, d), lambda i, j: (i, 0, 0)),
            pl.BlockSpec((d, ff), lambda i, j: (0, 0), pipeline_mode=resident),
            pl.BlockSpec((d, ff), lambda i, j: (0, 0), pipeline_mode=resident),
            pl.BlockSpec((ff, d), lambda i, j: (0, 0), pipeline_mode=resident),
        ],
        out_specs=tok,
        out_shape=jax.ShapeDtypeStruct((bsz, length, d), F32),
        compiler_params=_cparams("parallel", "parallel"),
    )(h, mod, wg, wu, wd)


def _pair_swap(x):
    width = x.shape[-1]
    lane = lax.broadcasted_iota(jnp.int32, x.shape, x.ndim - 1)
    nxt = pltpu.roll(x, width - 1, x.ndim - 1)
    prv = pltpu.roll(x, 1, x.ndim - 1)
    return jnp.where(lane % 2 == 0, nxt, prv)


def _rope(x, cos, sin):
    reps = x.shape[1] // LANES
    cos = jnp.concatenate([cos] * reps, axis=1) if reps > 1 else cos
    sin = jnp.concatenate([sin] * reps, axis=1) if reps > 1 else sin
    return x * cos + _pair_swap(x) * sin


def _cd_in_kernel(h_ref, mod_ref, w_ref, gm_ref, qg_ref, kg_ref, cos_ref, sin_ref,
                  cq_ref, ck_ref, cv_ref, dq_ref, dk_ref, dv_ref, *, rope):
    n = _modulate(h_ref[...], mod_ref[0:1, :], mod_ref[1:2, :])
    z = _dot(n.astype(BF16), w_ref[...])
    qw, kw, dqw, dw = cq_ref.shape[1], ck_ref.shape[1], dq_ref.shape[1], dk_ref.shape[1]
    o1, o2, o3, o4, o5 = qw, qw + kw, qw + 2 * kw, qw + 2 * kw + dqw, qw + 2 * kw + dqw + dw
    scale = HEAD_DIM ** -0.5

    zq = z[:, :o1]
    blocks = []
    for h in range(C_HEADS):
        t = zq[:, h * LANES:(h + 1) * LANES]
        ms = jnp.sum(t * t, axis=-1, keepdims=True) * (1.0 / HEAD_DIM)
        blocks.append(t * lax.rsqrt(ms + EPS))
    cq = jnp.concatenate(blocks, axis=1) * qg_ref[...]
    zk = z[:, o1:o2]
    ck = zk * lax.rsqrt(_dot((zk * zk).astype(BF16), gm_ref[...]) + EPS) * kg_ref[...]
    if rope:
        cq = _rope(cq, cos_ref[...], sin_ref[...])
        ck = _rope(ck, cos_ref[...], sin_ref[...])
    scale = scale * LOG2E
    cq_ref[...] = (cq * scale).astype(BF16)
    ck_ref[...] = ck.astype(BF16)
    cv_ref[...] = z[:, o2:o3].astype(BF16)
    dq_ref[...] = (z[:, o3:o4] * scale).astype(BF16)
    dk_ref[...] = z[:, o4:o5].astype(BF16)
    dv_ref[...] = z[:, o5:].astype(BF16)


def _project_cd(h, mod, p, rope):
    bsz, length, d = h.shape
    qw, kw, dw = C_HEADS * LANES, C_KV_HEADS * HEAD_DIM, D_HEADS * HEAD_DIM
    tm = _row_tile(length, 512)
    const = lambda i, j: (0, 0)
    tok = lambda w: pl.BlockSpec((None, tm, w), lambda i, j: (i, j, 0))
    widths = (qw, kw, kw, D_HEADS * LANES, dw, dw)
    return pl.pallas_call(
        functools.partial(_cd_in_kernel, rope=rope),
        grid=(bsz, length // tm),
        in_specs=[
            tok(d),
            pl.BlockSpec((None, 8, d), lambda i, j: (i, 0, 0)),
            pl.BlockSpec(p["w_in"].shape, const),
            pl.BlockSpec(p["gmean"].shape, const),
            pl.BlockSpec(p["q_gain"].shape, const),
            pl.BlockSpec(p["k_gain"].shape, const),
            pl.BlockSpec((tm, LANES), lambda i, j: (j, 0)),
            pl.BlockSpec((tm, LANES), lambda i, j: (j, 0)),
        ],
        out_specs=[tok(w) for w in widths],
        out_shape=[jax.ShapeDtypeStruct((bsz, length, w), BF16) for w in widths],
        compiler_params=_cparams("parallel", "parallel"),
    )(h, mod, p["w_in"], p["gmean"], p["q_gain"], p["k_gain"], p["cos"], p["sin"])


def _gqa_kernel(q_ref, kt_ref, v_ref, o_ref):
    tq = q_ref.shape[0]
    sub = tq // GQA_SPLIT
    lane = lax.broadcasted_iota(jnp.int32, (sub, LANES), 1)

    def attend(h, rows):
        s = _dot(q_ref[rows, h * LANES:(h + 1) * LANES], kt_ref[...])
        p = jnp.exp2(s - jnp.max(s, axis=-1, keepdims=True)).astype(BF16)
        o = _dot(p, v_ref[...])
        return o[:, :LANES] / o[:, LANES:LANES + 1]

    for pair in range(C_HEADS // 2):
        for part in range(GQA_SPLIT):
            rows = slice(part * sub, (part + 1) * sub)
            a, b = attend(2 * pair, rows), attend(2 * pair + 1, rows)
            if (2 * pair) // C_GROUP == 0:
                blk = jnp.where(lane < HEAD_DIM, a, pltpu.roll(b, HEAD_DIM, 1))
            else:
                blk = jnp.where(lane < HEAD_DIM, pltpu.roll(a, HEAD_DIM, 1), b)
            o_ref[rows, pair * LANES:(pair + 1) * LANES] = blk.astype(BF16)


def _gqa(q_ext, k_all, v_all):
    bsz, length, qw = q_ext.shape
    lk = k_all.shape[1]
    assert C_KV_HEADS * HEAD_DIM == LANES and C_KV_HEADS == 2
    kt = jnp.swapaxes(k_all, 1, 2)
    v_ext = jnp.concatenate([v_all, jnp.ones_like(v_all)], axis=2)
    tq = _row_tile(length, 512)
    ow = C_HEADS * HEAD_DIM
    return pl.pallas_call(
        _gqa_kernel,
        grid=(bsz, length // tq),
        in_specs=[
            pl.BlockSpec((None, tq, qw), lambda i, j: (i, j, 0)),
            pl.BlockSpec((None, LANES, lk), lambda i, j: (i, 0, 0)),
            pl.BlockSpec((None, lk, 2 * LANES), lambda i, j: (i, 0, 0)),
        ],
        out_specs=pl.BlockSpec((None, tq, ow), lambda i, j: (i, j, 0)),
        out_shape=jax.ShapeDtypeStruct((bsz, length, ow), BF16),
        compiler_params=_cparams("parallel", "parallel"),
    )(q_ext, kt, v_ext)


def _na_block_start(qb, rows):
    return jnp.clip(qb * NA_QROWS - NA_WIN_H // 2, 0, rows - NA_KROWS)


def _na_bias_tables(rel_bias, rows):
    nblk = rows // NA_QROWS
    n_dr, n_dc = 2 * NA_WIN_H - 1, 2 * NA_WIN_W - 1
    i = np.arange(NA_QROWS)[:, None]
    a = np.arange(NA_KROWS)[None, :]
    j = np.arange(GRID_W)[:, None]
    kc = np.arange(GRID_W)[None, :]
    col_start = np.clip(j - NA_WIN_W // 2, 0, GRID_W - NA_WIN_W)
    valid_col = (kc >= col_start) & (kc < col_start + NA_WIN_W)
    dc = np.clip(kc - j + (NA_WIN_W - 1), 0, n_dc - 1)
    onehot_c = (dc[:, :, None] == np.arange(n_dc)).astype(np.float32)
    onehot_r, valid = [], []
    for qb in (0, 1, nblk - 1):
        r = qb * NA_QROWS + i
        r0 = np.clip(r - NA_WIN_H // 2, 0, rows - NA_WIN_H)
        kr = int(np.clip(qb * NA_QROWS - NA_WIN_H // 2, 0, rows - NA_KROWS)) + a
        valid_row = (kr >= r0) & (kr < r0 + NA_WIN_H)
        dr = np.clip(kr - r + (NA_WIN_H - 1), 0, n_dr - 1)
        onehot_r.append((dr[:, :, None] == np.arange(n_dr)).astype(np.float32))
        valid.append(valid_row[:, None, :, None] & valid_col[None, :, None, :])
    onehot_r = jnp.asarray(np.stack(onehot_r))
    valid = np.stack(valid)
    hp = lax.Precision.HIGHEST
    by_row = jnp.einsum("hrc,ziar->zhiac", rel_bias.astype(F32), onehot_r, precision=hp)
    table = jnp.einsum("zhiac,jkc->zhijak", by_row, jnp.asarray(onehot_c), precision=hp)
    table = jnp.where(valid[:, None], table * LOG2E, MASK_VALUE)
    return table.reshape(3 * rel_bias.shape[0], NA_QROWS * GRID_W, NA_KROWS * GRID_W)


def _na_kernel(q_ref, k_ref, v_ref, kc_ref, vc_ref, bias_ref, o_ref, *, rows):
    qb = pl.program_id(1)
    nk = NA_KROWS * GRID_W
    start = pl.multiple_of(_na_block_start(qb, rows) * GRID_W, GRID_W)
    k_all = jnp.concatenate([k_ref[pl.ds(start, nk), :], kc_ref[...]], axis=0)
    v_all = jnp.concatenate([v_ref[pl.ds(start, nk), :], vc_ref[...]], axis=0)
    ones = jnp.ones((k_all.shape[0], LANES), BF16)
    sub = q_ref.shape[0] // NA_SPLIT
    lane = lax.broadcasted_iota(jnp.int32, (sub, LANES), 1)

    def attend(h, rows, k_pair, v_pair):
        s = _dot_nt(q_ref[rows, h * LANES:(h + 1) * LANES], k_pair)
        s = jnp.concatenate([s[:, :nk] + bias_ref[h, rows, :], s[:, nk:]], axis=1)
        p = jnp.exp2(s - jnp.max(s, axis=-1, keepdims=True)).astype(BF16)
        o = _dot(p, v_pair)
        return o[:, :LANES] / o[:, LANES:LANES + 1]

    for pair in range(D_HEADS // 2):
        cols = slice(pair * LANES, (pair + 1) * LANES)
        k_pair = k_all[:, cols]
        v_pair = jnp.concatenate([v_all[:, cols], ones], axis=1)
        for part in range(NA_SPLIT):
            rows = slice(part * sub, (part + 1) * sub)
            even, odd = attend(2 * pair, rows, k_pair, v_pair), attend(2 * pair + 1, rows, k_pair, v_pair)
            o_ref[rows, cols] = jnp.where(lane < HEAD_DIM, even, odd).astype(BF16)


def _neighbourhood(dq, dk, dv, dk_c, dv_c, bias):
    bsz, length, w = dk.shape
    rows = length // GRID_W
    assert rows % NA_QROWS == 0 and rows >= NA_KROWS
    nblk = rows // NA_QROWS
    tq = NA_QROWS * GRID_W
    cl = dk_c.shape[1]

    def bias_class(i, j):
        cls = jnp.where(j == 0, 0, jnp.where(j == nblk - 1, 2, 1))
        return (cls, 0, 0)

    return pl.pallas_call(
        functools.partial(_na_kernel, rows=rows),
        grid=(bsz, nblk),
        in_specs=[
            pl.BlockSpec((None, tq, dq.shape[2]), lambda i, j: (i, j, 0)),
            pl.BlockSpec((None, length, w), lambda i, j: (i, 0, 0)),
            pl.BlockSpec((None, length, w), lambda i, j: (i, 0, 0)),
            pl.BlockSpec((None, cl, w), lambda i, j: (i, 0, 0)),
            pl.BlockSpec((None, cl, w), lambda i, j: (i, 0, 0)),
            pl.BlockSpec((D_HEADS, tq, NA_KROWS * GRID_W), bias_class),
        ],
        out_specs=pl.BlockSpec((None, tq, w), lambda i, j: (i, j, 0)),
        out_shape=jax.ShapeDtypeStruct((bsz, length, w), BF16),
        compiler_params=_cparams("parallel", "arbitrary"),
    )(dq, dk, dv, dk_c, dv_c, bias)


def _router_kernel(h_ref, mod_ref, wr_ref, n_ref, sel_ref, cnt_ref):
    n = _modulate(h_ref[...], mod_ref[3:4, :], mod_ref[4:5, :])
    n_hi = n.astype(BF16)
    n_lo = (n - n_hi.astype(F32)).astype(BF16)
    w = wr_ref[...]
    w_hi = w.astype(BF16)
    w_lo = (w - w_hi.astype(F32)).astype(BF16)
    logits = _dot(n_hi, w_hi) + (_dot(n_lo, w_hi) + _dot(n_hi, w_lo))
    lane = lax.broadcasted_iota(jnp.int32, logits.shape, 1)
    logits = jnp.where(lane < N_EXPERTS, logits, -jnp.inf)
    m1 = jnp.max(logits, axis=-1, keepdims=True)
    i1 = jnp.min(jnp.where(logits == m1, lane, LANES), axis=-1, keepdims=True)
    rest = jnp.where(lane == i1, -jnp.inf, logits)
    m2 = jnp.max(rest, axis=-1, keepdims=True)
    i2 = jnp.min(jnp.where(rest == m2, lane, LANES), axis=-1, keepdims=True)
    e2 = jnp.exp(m2 - m1)
    w1 = 1.0 / (1.0 + e2)
    w2 = e2 / (1.0 + e2)

    ntok = logits.shape[0]
    chosen = jnp.where(lane == i1, 1.0, jnp.where(lane == i2, 1.0, 0.0))
    earlier = (lax.broadcasted_iota(jnp.int32, (ntok, ntok), 1)
               < lax.broadcasted_iota(jnp.int32, (ntok, ntok), 0))
    before = _dot(jnp.where(earlier, 1.0, 0.0).astype(BF16), chosen.astype(BF16))
    count = jnp.sum(chosen, axis=0, keepdims=True).astype(jnp.int32)
    npiece = lax.shift_right_logical(count + (MOE_PIECE - 1), jnp.full_like(count, MOE_PIECE.bit_length() - 1))
    lower = (lax.broadcasted_iota(jnp.int32, (LANES, LANES), 0)
             < lax.broadcasted_iota(jnp.int32, (LANES, LANES), 1))
    npiece_rows = jnp.broadcast_to(npiece.astype(F32), (8, LANES)).astype(BF16)
    piece_off = _dot(npiece_rows, jnp.where(lower, 1.0, 0.0).astype(BF16))[0:1, :]
    row = piece_off * float(MOE_PIECE) + before
    pos1 = jnp.sum(jnp.where(lane == i1, row, 0.0), axis=-1, keepdims=True)
    pos2 = jnp.sum(jnp.where(lane == i2, row, 0.0), axis=-1, keepdims=True)

    sel = jnp.where(lane == 0, i1.astype(F32), jnp.where(lane == 1, i2.astype(F32),
                    jnp.where(lane == 2, w1, jnp.where(lane == 3, w2,
                              jnp.where(lane == 4, pos1, jnp.where(lane == 5, pos2, 0.0))))))
    sel_ref[...] = sel
    cnt_ref[...] = jnp.broadcast_to(npiece.astype(F32), cnt_ref.shape)
    n_ref[...] = n_hi


def _router(h, mod, w_router_padded):
    bsz, length, d = h.shape
    tm = _row_tile(length, MOE_TOKENS)
    tiles = length // tm
    tok = lambda w: pl.BlockSpec((None, tm, w), lambda i, j: (i, j, 0))
    return pl.pallas_call(
        _router_kernel,
        grid=(bsz, tiles),
        in_specs=[
            tok(d),
            pl.BlockSpec((None, 8, d), lambda i, j: (i, 0, 0)),
            pl.BlockSpec((d, LANES), lambda i, j: (0, 0)),
        ],
        out_specs=[tok(d), tok(LANES), pl.BlockSpec((None, None, 8, LANES), lambda i, j: (i, j, 0, 0))],
        out_shape=[
            jax.ShapeDtypeStruct((bsz, length, d), BF16),
            jax.ShapeDtypeStruct((bsz, length, LANES), F32),
            jax.ShapeDtypeStruct((bsz, tiles, 8, LANES), F32),
        ],
        compiler_params=_cparams("parallel", "parallel"),
    )(h, mod, w_router_padded)


def _moe_dispatch_kernel(pos_ref, w_ref, n_ref, xs_ref, gs_ref):
    nrow = xs_ref.shape[0]
    base = pl.program_id(1) * nrow
    row = base + lax.broadcasted_iota(jnp.int32, (nrow, 1), 0)
    eq0 = pos_ref[0:1, :] == row
    eq1 = pos_ref[1:2, :] == row
    perm = jnp.where(eq0, 1.0, jnp.where(eq1, 1.0, 0.0)).astype(BF16)
    xs_ref[...] = _dot(perm, n_ref[...]).astype(BF16)
    gate = jnp.where(eq0, w_ref[0:1, :], jnp.where(eq1, w_ref[1:2, :], 0.0))
    gs_ref[...] = jnp.sum(gate, axis=-1, keepdims=True)


def _moe_expert_kernel(piece_ref, exp_ref, nused_ref, *refs):
    xs_refs, gs_refs = refs[:MOE_GROUP], refs[MOE_GROUP:2 * MOE_GROUP]
    wg_ref, wu_ref, wd_ref, ys_ref = refs[2 * MOE_GROUP:]
    i = pl.program_id(0)

    @pl.when(i < nused_ref[0])
    def _():
        x = jnp.concatenate([r[...] for r in xs_refs], axis=0)
        gate = jnp.concatenate([r[...] for r in gs_refs], axis=0)
        ys_ref[...] = (_swiglu(x, wg_ref, wu_ref, wd_ref, MOE_FF_CHUNK) * gate).astype(BF16)

    @pl.when(i >= nused_ref[0])
    def _():
        ys_ref[...] = jnp.zeros_like(ys_ref)


def _moe_combine_kernel(slot_ref, pos_ref, h_ref, mod_ref, fg_ref, *refs):
    ys = jnp.concatenate([r[...] for r in refs[:-1]], axis=0)
    o_ref = refs[-1]
    nrow = ys.shape[0]
    sub = pos_ref.shape[0] // MOE_COMBINE_SPLIT
    row = lax.broadcasted_iota(jnp.int32, (sub, nrow), 1)
    for part in range(MOE_COMBINE_SPLIT):
        toks = slice(part * sub, (part + 1) * sub)
        perm = jnp.where(pos_ref[toks, 0:1] == row, 1.0,
                         jnp.where(pos_ref[toks, 1:2] == row, 1.0, 0.0)).astype(BF16)
        y = h_ref[toks, :] + mod_ref[5:6, :] * _dot(perm, ys)
        ms = jnp.mean(y * y, axis=-1, keepdims=True)
        o_ref[toks, :] = y * lax.rsqrt(ms + EPS) * fg_ref[...]


def _moe_plan(sel, npiece, n_tiles, tile_tokens, rows_per_tile):
    pieces_per_tile = rows_per_tile // MOE_PIECE
    weights = sel[:, TOP_K:2 * TOP_K].reshape(n_tiles, tile_tokens, TOP_K)
    pos = sel[:, 2 * TOP_K:3 * TOP_K].astype(jnp.int32).reshape(n_tiles, tile_tokens, TOP_K)
    piece_end = jnp.cumsum(npiece.astype(jnp.int32), axis=1)

    b = jnp.arange(pieces_per_tile, dtype=jnp.int32)
    piece_expert = jnp.sum((piece_end[:, None, :] <= b[None, :, None]).astype(jnp.int32), axis=-1)
    flat_expert = piece_expert.reshape(-1)
    n_pieces = flat_expert.shape[0]
    n_slots = n_pieces + N_EXPERTS * MOE_GROUP
    classes = jnp.arange(N_EXPERTS + 1, dtype=jnp.int32)
    cls_onehot = (flat_expert[:, None] == classes).astype(jnp.int32)
    cls_csum = jnp.cumsum(cls_onehot, axis=0)
    piece_rank = jnp.sum((cls_csum - cls_onehot) * cls_onehot, axis=-1)
    n_cls = cls_csum[-1]
    groups = (n_cls[:N_EXPERTS] + MOE_GROUP - 1) // MOE_GROUP
    start = (jnp.cumsum(groups) - groups) * MOE_GROUP
    n_used_groups = jnp.sum(groups)

    slot = jnp.arange(n_slots, dtype=jnp.int32)
    slot_expert = jnp.sum((start[None, :] <= slot[:, None]).astype(jnp.int32), axis=-1) - 1
    slot_taken = (slot - start[slot_expert]) < n_cls[slot_expert]
    free_slots = jnp.argsort(slot_taken.astype(jnp.int32) * n_slots + slot).astype(jnp.int32)
    is_used = flat_expert < N_EXPERTS
    safe_expert = jnp.minimum(flat_expert, N_EXPERTS - 1)
    piece_slot = jnp.where(is_used, start[safe_expert] + piece_rank, free_slots[piece_rank])
    first_unused = jnp.argmax(jnp.logical_not(is_used)).astype(jnp.int32)
    piece_ids = jnp.arange(n_pieces, dtype=jnp.int32)
    slot_piece = jnp.full((n_slots,), first_unused, jnp.int32).at[piece_slot].set(piece_ids, unique_indices=True)
    step_expert = slot_expert[::MOE_GROUP]
    return pos, weights, slot_piece, piece_slot.astype(jnp.int32), step_expert, n_used_groups.reshape(1)


def _moe_final(n2, sel, npiece, h, mod, final_gain, wg, wu, wd):
    bsz, length, d = h.shape
    n_exp, _, ff = wg.shape
    tokens = bsz * length
    tt = _row_tile(length, MOE_TOKENS)
    n_tiles = tokens // tt
    rows = TOP_K * tt + N_EXPERTS * MOE_PIECE
    drows = _divisor_tile(rows, MOE_DISPATCH_ROWS, MOE_PIECE)
    ctok = _row_tile(tt, MOE_COMBINE_TOKENS)
    pos, weights, slot_piece, piece_slot, step_expert, n_used = _moe_plan(
        sel.reshape(tokens, LANES), npiece.reshape(n_tiles, 8, LANES)[:, 0, :N_EXPERTS], n_tiles, tt, rows)

    pos_rows = jnp.swapaxes(pos, 1, 2)
    w_rows = jnp.swapaxes(weights, 1, 2)
    xs, gs = pl.pallas_call(
        _moe_dispatch_kernel,
        grid=(n_tiles, rows // drows),
        in_specs=[
            pl.BlockSpec((None, TOP_K, tt), lambda t, r: (t, 0, 0)),
            pl.BlockSpec((None, TOP_K, tt), lambda t, r: (t, 0, 0)),
            pl.BlockSpec((tt, d), lambda t, r: (t, 0)),
        ],
        out_specs=[
            pl.BlockSpec((None, drows, d), lambda t, r: (t, r, 0)),
            pl.BlockSpec((None, drows, 1), lambda t, r: (t, r, 0)),
        ],
        out_shape=[
            jax.ShapeDtypeStruct((n_tiles, rows, d), BF16),
            jax.ShapeDtypeStruct((n_tiles, rows, 1), F32),
        ],
        compiler_params=_cparams("parallel", "parallel"),
    )(pos_rows, w_rows, n2.reshape(tokens, d))

    n_steps = slot_piece.shape[0] // MOE_GROUP
    resident = pl.Buffered(1)

    def piece_spec(j, width):
        return pl.BlockSpec((MOE_PIECE, width), lambda i, sp, ex, nu: (sp[i * MOE_GROUP + j], 0))

    def weight_spec(shape):
        return pl.BlockSpec((None,) + shape, lambda i, sp, ex, nu: (ex[i], 0, 0), pipeline_mode=resident)

    xs_flat, gs_flat = xs.reshape(n_tiles * rows, d), gs.reshape(n_tiles * rows, 1)
    step_rows = MOE_GROUP * MOE_PIECE
    ys = pl.pallas_call(
        _moe_expert_kernel,
        grid_spec=pltpu.PrefetchScalarGridSpec(
            num_scalar_prefetch=3,
            grid=(n_steps,),
            in_specs=([piece_spec(j, d) for j in range(MOE_GROUP)]
                      + [piece_spec(j, 1) for j in range(MOE_GROUP)]
                      + [weight_spec((d, ff)), weight_spec((d, ff)), weight_spec((ff, d))]),
            out_specs=pl.BlockSpec((step_rows, d), lambda i, sp, ex, nu: (i, 0)),
        ),
        out_shape=jax.ShapeDtypeStruct((n_steps * step_rows, d), BF16),
        compiler_params=_cparams("arbitrary"),
    )(slot_piece, step_expert, n_used, *([xs_flat] * MOE_GROUP), *([gs_flat] * MOE_GROUP), wg, wu, wd)

    tiles_per_seq = length // tt
    pieces_per_tile = rows // MOE_PIECE

    def tile_piece_spec(k):
        return pl.BlockSpec((MOE_PIECE, d), lambda t, c, slot: (slot[t * pieces_per_tile + k], 0))

    out = pl.pallas_call(
        _moe_combine_kernel,
        grid_spec=pltpu.PrefetchScalarGridSpec(
            num_scalar_prefetch=1,
            grid=(n_tiles, tt // ctok),
            in_specs=[
                pl.BlockSpec((None, ctok, TOP_K), lambda t, c, slot: (t, c, 0)),
                pl.BlockSpec((None, ctok, d), lambda t, c, slot: (t, c, 0)),
                pl.BlockSpec((None, 8, d), lambda t, c, slot: (t // tiles_per_seq, 0, 0)),
                pl.BlockSpec((1, d), lambda t, c, slot: (0, 0)),
            ] + [tile_piece_spec(k) for k in range(pieces_per_tile)],
            out_specs=pl.BlockSpec((None, ctok, d), lambda t, c, slot: (t, c, 0)),
        ),
        out_shape=jax.ShapeDtypeStruct((n_tiles, tt, d), F32),
        compiler_params=_cparams("parallel", "parallel"),
    )(piece_slot, pos, h.reshape(n_tiles, tt, d), mod, final_gain, *([ys] * pieces_per_tile))
    return out.reshape(bsz, length, d)


def _group_mean_matrix(width, group):
    return jnp.asarray(np.kron(np.eye(width // group), np.full((group, group), 1.0 / group)), BF16)


def _rope_tables(length):
    t = jnp.arange(length, dtype=jnp.int32)
    row = (t // GRID_W).astype(F32)
    col = (t % GRID_W).astype(F32)
    n_axis = HEAD_DIM // 4
    inv_freq = ROPE_THETA ** (-jnp.arange(n_axis, dtype=F32) / n_axis)
    ang = jnp.concatenate([row[:, None] * inv_freq, col[:, None] * inv_freq], axis=-1)
    cos = jnp.repeat(jnp.cos(ang), 2, axis=-1)
    sin = jnp.repeat(jnp.sin(ang), 2, axis=-1)
    sign = jnp.tile(jnp.asarray([-1.0, 1.0], F32), HEAD_DIM // 2)
    reps = LANES // HEAD_DIM
    return jnp.tile(cos, (1, reps)), jnp.tile(sin * sign, (1, reps))


def _spread_heads(w_q, upper_half):
    d = w_q.shape[0]
    n_heads = w_q.shape[1] // HEAD_DIM
    w = w_q.reshape(d, n_heads, HEAD_DIM)
    zeros = jnp.zeros_like(w)
    upper = jnp.asarray(upper_half, bool)[None, :, None]
    lo = jnp.where(upper, zeros, w)
    hi = jnp.where(upper, w, zeros)
    return jnp.concatenate([lo, hi], axis=2).reshape(d, n_heads * LANES)


def kernel(x, c, ctx, c_ctx, w_ada, b_ada, w_in_ab, v_gain, w_spatial, b_spatial, w_out_ab,
           w_gate_dense, w_up_dense, w_down_dense, w_in_cd, q_gain, k_gain, rel_bias, w_out_cd,
           w_router, w_gate_moe, w_up_moe, w_down_moe, final_gain):
    bsz, length, d = x.shape
    depth = w_ada.shape[0]
    assert depth == 2 and bsz <= 8
    aw = v_gain.shape[1]
    group_dim = aw // A_GROUPS
    bgd = (w_in_ab.shape[2] - 2 * aw) // B_GROUPS

    cond = jnp.zeros((16, d), F32).at[:bsz].set(c).at[8].set(c_ctx)
    ada = _adaln(cond, w_ada, b_ada).reshape(depth, 16, 6, d)
    pad = jnp.zeros((depth, 16, 2, d), F32)
    ada = jnp.concatenate([ada, pad], axis=2)
    mod_lat = [ada[i, :bsz] for i in range(depth)]
    mod_ctx = [jnp.broadcast_to(ada[i, 8], (bsz, 8, d)) for i in range(depth)]

    cmat, smat = _dft_tables(bgd, bgd ** -0.5)
    p_ab = dict(
        w_in=w_in_ab[0].astype(BF16),
        v_gain=v_gain[0].reshape(1, aw),
        gmean=_group_mean_matrix(aw, group_dim),
        w_sp=w_spatial[0].transpose(1, 0, 2).reshape(CHUNK, A_GROUPS * CHUNK).astype(BF16),
        b_sp=jnp.repeat(b_spatial[0].T, group_dim, axis=1),
        cmat=cmat, smat=smat,
        w_out=w_out_ab[0].astype(BF16),
    )
    wg0, wu0, wd0 = (w.astype(BF16) for w in (w_gate_dense[0], w_up_dense[0], w_down_dense[0]))
    h = _proj_residual(*_mix_ab(x, mod_lat[0], p_ab), p_ab["w_out"], x, mod_lat[0])
    hc = _proj_residual(*_mix_ab(ctx, mod_ctx[0], p_ab), p_ab["w_out"], ctx, mod_ctx[0])
    h = _dense_ffn(h, mod_lat[0], wg0, wu0, wd0)
    hc = _dense_ffn(hc, mod_ctx[0], wg0, wu0, wd0)

    cos, sin = _rope_tables(length)
    qw = C_HEADS * HEAD_DIM
    kw = C_KV_HEADS * HEAD_DIM
    w_cd = w_in_cd[0]
    dq0, dq1 = qw + 2 * kw, qw + 2 * kw + D_HEADS * HEAD_DIM
    p_cd = dict(
        w_in=jnp.concatenate([
            _spread_heads(w_cd[:, :qw], [h // C_GROUP == 1 for h in range(C_HEADS)]),
            w_cd[:, qw:dq0],
            _spread_heads(w_cd[:, dq0:dq1], [h % 2 == 1 for h in range(D_HEADS)]),
            w_cd[:, dq1:]], axis=1).astype(BF16),
        gmean=_group_mean_matrix(kw, HEAD_DIM),
        q_gain=jnp.tile(q_gain[0], C_HEADS * LANES // HEAD_DIM).reshape(1, C_HEADS * LANES),
        k_gain=jnp.tile(k_gain[0], C_KV_HEADS).reshape(1, kw),
        cos=cos, sin=sin,
    )
    cq, ck, cv, dq, dk, dv = _project_cd(h, mod_lat[1], p_cd, rope=True)
    p_cd_ctx = dict(p_cd, cos=cos[:hc.shape[1]], sin=sin[:hc.shape[1]])
    _, ck_c, cv_c, _, dk_c, dv_c = _project_cd(hc, mod_ctx[1], p_cd_ctx, rope=False)
    o_c = _gqa(cq, jnp.concatenate([ck_c, ck], axis=1), jnp.concatenate([cv_c, cv], axis=1))
    o_d = _neighbourhood(dq, dk, dv, dk_c, dv_c, _na_bias_tables(rel_bias[0], length // GRID_W))
    wr = jnp.zeros((d, LANES), F32).at[:, :N_EXPERTS].set(w_router[0])
    h = _proj_residual(o_c, o_d, w_out_cd[0].astype(BF16), h, mod_lat[1])
    n2, sel, npiece = _router(h, mod_lat[1], wr)
    return _moe_final(n2, sel, npiece, h, mod_lat[1], final_gain.reshape(1, d),
                      w_gate_moe[0].astype(BF16), w_up_moe[0].astype(BF16), w_down_moe[0].astype(BF16))
```

```python
import functools
import math

import numpy as np
import jax
import jax.numpy as jnp
from jax import lax
from jax.experimental import pallas as pl
from jax.experimental.pallas import tpu as pltpu

F32 = jnp.float32
BF16 = jnp.bfloat16

GRID_W = 64
EPS = 1e-6
CHUNK = 128
A_GROUPS = 8
B_GROUPS = 4
HEAD_DIM = 64
C_HEADS = 8
C_KV_HEADS = 2
C_GROUP = C_HEADS // C_KV_HEADS
D_HEADS = 8
NA_WIN_H = 8
NA_WIN_W = 16
ROPE_THETA = 10000.0
N_EXPERTS = 8
TOP_K = 2

LANES = 128
GQA_SPLIT = 4
NA_QROWS = 4
NA_SPLIT = 2
NA_KROWS = NA_QROWS + NA_WIN_H
MASK_VALUE = -1e30
LOG2E = 1.4426950408889634
VMEM_LIMIT = 56 * 1024 * 1024

DENSE_FF_CHUNK = 1536
MOE_TOKENS = 1024
MOE_PIECE = 32
MOE_GROUP = 16
MOE_DISPATCH_ROWS = 1280
MOE_COMBINE_TOKENS = 1024
MOE_COMBINE_SPLIT = 4
MOE_FF_CHUNK = 1792


def _cparams(*sem):
    return pltpu.CompilerParams(dimension_semantics=sem, vmem_limit_bytes=VMEM_LIMIT)


def _modulate(h, shift, scale):
    ms = jnp.mean(h * h, axis=-1, keepdims=True)
    return h * lax.rsqrt(ms + EPS) * (1.0 + scale) + shift


def _dot(a, b):
    return jnp.dot(a, b, preferred_element_type=F32)


def _dot_nt(a, b):
    return lax.dot_general(a, b, (((1,), (1,)), ((), ())), preferred_element_type=F32)


def _row_tile(length, target):
    t = min(length, target)
    assert length % t == 0
    return t


def _adaln_kernel(c_ref, w_ref, b_ref, o_ref):
    s = jax.nn.silu(c_ref[...]).astype(BF16)
    o_ref[...] = _dot(s, w_ref[...].astype(BF16)) + b_ref[...]


def _adaln(cond, w_ada, b_ada):
    depth, d, n = w_ada.shape
    r = cond.shape[0]
    tn = 512
    return pl.pallas_call(
        _adaln_kernel,
        grid=(depth, n // tn),
        in_specs=[
            pl.BlockSpec((r, d), lambda i, j: (0, 0)),
            pl.BlockSpec((None, d, tn), lambda i, j: (i, 0, j)),
            pl.BlockSpec((None, 1, tn), lambda i, j: (i, 0, j)),
        ],
        out_specs=pl.BlockSpec((None, r, tn), lambda i, j: (i, 0, j)),
        out_shape=jax.ShapeDtypeStruct((depth, r, n), F32),
        compiler_params=_cparams("parallel", "parallel"),
    )(cond, w_ada, b_ada.reshape(depth, 1, n))


def _ab_in_kernel(h_ref, mod_ref, w_ref, vg_ref, gm_ref, wsp_ref, bsp_ref, cm_ref, sm_ref,
                  a_ref, fc_ref, fs_ref, *, aw):
    tm = h_ref.shape[0]
    n = _modulate(h_ref[...], mod_ref[0:1, :], mod_ref[1:2, :])
    z = _dot(n.astype(BF16), w_ref[...])
    u = jax.nn.gelu(z[:, :aw])
    v = jax.nn.gelu(z[:, aw:2 * aw])
    fb = z[:, 2 * aw:].astype(BF16)
    ms = _dot((v * v).astype(BF16), gm_ref[...])
    vn = v * lax.rsqrt(ms + EPS) * vg_ref[...]
    group_dim = aw // A_GROUPS
    lane_group = lax.broadcasted_iota(jnp.int32, (CHUNK, aw), 1) // group_dim
    for c in range(tm // CHUNK):
        rows = slice(c * CHUNK, (c + 1) * CHUNK)
        vc = vn[rows, :]
        stack = jnp.concatenate(
            [jnp.where(lane_group == g, vc, 0.0).astype(BF16) for g in range(A_GROUPS)], axis=0)
        s = _dot(wsp_ref[...], stack) + bsp_ref[...]
        a_ref[rows, :] = (u[rows, :] * s).astype(BF16)
    bw = fb.shape[1] // B_GROUPS
    for g in range(B_GROUPS):
        cols = slice(g * bw, (g + 1) * bw)
        fc_ref[:, cols] = _dot(fb[:, cols], cm_ref[...]).astype(BF16)
        fs_ref[:, cols] = _dot(fb[:, cols], sm_ref[...]).astype(BF16)


def _seq_dft_kernel(c_ref, s_ref, fc_ref, fs_ref, o_ref):
    o_ref[...] = (_dot(c_ref[...], fc_ref[...]) - _dot(s_ref[...], fs_ref[...])).astype(BF16)


def _proj_residual_kernel(a_ref, b_ref, w_ref, h_ref, mod_ref, o_ref):
    ka = a_ref.shape[1]
    y = _dot(a_ref[...], w_ref[:ka, :]) + _dot(b_ref[...], w_ref[ka:, :])
    o_ref[...] = h_ref[...] + mod_ref[2:3, :] * y


def _proj_residual(a, b, w, h, mod):
    bsz, length, d = h.shape
    tm = _row_tile(length, 1024)
    ka, kb = a.shape[2], b.shape[2]
    return pl.pallas_call(
        _proj_residual_kernel,
        grid=(bsz, length // tm),
        in_specs=[
            pl.BlockSpec((None, tm, ka), lambda i, j: (i, j, 0)),
            pl.BlockSpec((None, tm, kb), lambda i, j: (i, j, 0)),
            pl.BlockSpec((ka + kb, d), lambda i, j: (0, 0)),
            pl.BlockSpec((None, tm, d), lambda i, j: (i, j, 0)),
            pl.BlockSpec((None, 8, d), lambda i, j: (i, 0, 0)),
        ],
        out_specs=pl.BlockSpec((None, tm, d), lambda i, j: (i, j, 0)),
        out_shape=jax.ShapeDtypeStruct((bsz, length, d), F32),
        compiler_params=_cparams("parallel", "parallel"),
    )(a, b, w, h, mod)


def _dft_tables(n, scale):
    k = jnp.arange(n, dtype=jnp.int32)
    ang = ((k[:, None] * k[None, :]) % n).astype(F32) * (2.0 * math.pi / n)
    return (jnp.cos(ang) * scale).astype(BF16), (jnp.sin(ang) * scale).astype(BF16)


def _mix_ab(h, mod, p):
    bsz, length, d = h.shape
    aw = p["v_gain"].shape[1]
    bw_total = p["w_in"].shape[1] - 2 * aw
    const = lambda i, j: (0, 0)
    tm = _row_tile(length, 512)
    f_spec = pl.BlockSpec((None, tm, bw_total), lambda i, j: (i, j, 0))
    f_shape = jax.ShapeDtypeStruct((bsz, length, bw_total), BF16)
    a_out, fc, fs = pl.pallas_call(
        functools.partial(_ab_in_kernel, aw=aw),
        grid=(bsz, length // tm),
        in_specs=[
            pl.BlockSpec((None, tm, d), lambda i, j: (i, j, 0)),
            pl.BlockSpec((None, 8, d), lambda i, j: (i, 0, 0)),
            pl.BlockSpec(p["w_in"].shape, const),
            pl.BlockSpec(p["v_gain"].shape, const),
            pl.BlockSpec(p["gmean"].shape, const),
            pl.BlockSpec(p["w_sp"].shape, const),
            pl.BlockSpec(p["b_sp"].shape, const),
            pl.BlockSpec(p["cmat"].shape, const),
            pl.BlockSpec(p["smat"].shape, const),
        ],
        out_specs=[pl.BlockSpec((None, tm, aw), lambda i, j: (i, j, 0)), f_spec, f_spec],
        out_shape=[jax.ShapeDtypeStruct((bsz, length, aw), BF16), f_shape, f_shape],
        compiler_params=_cparams("parallel", "parallel"),
    )(h, mod, p["w_in"], p["v_gain"], p["gmean"], p["w_sp"], p["b_sp"], p["cmat"], p["smat"])
    return a_out, _seq_dft(fc, fs)


def _seq_dft_direct(fc, fs):
    bsz, length, width = fc.shape
    cl, sl = _dft_tables(length, length ** -0.5)
    tk = _row_tile(length, 512)
    return pl.pallas_call(
        _seq_dft_kernel,
        grid=(length // tk, bsz),
        in_specs=[
            pl.BlockSpec((tk, length), lambda k, b: (k, 0)),
            pl.BlockSpec((tk, length), lambda k, b: (k, 0)),
            pl.BlockSpec((None, length, width), lambda k, b: (b, 0, 0)),
            pl.BlockSpec((None, length, width), lambda k, b: (b, 0, 0)),
        ],
        out_specs=pl.BlockSpec((None, tk, width), lambda k, b: (b, k, 0)),
        out_shape=jax.ShapeDtypeStruct((bsz, length, width), BF16),
        compiler_params=_cparams("parallel", "parallel"),
    )(cl, sl, fc, fs)


def _fft_rows_kernel(fc_ref, fs_ref, ma_ref, tc_ref, ts_ref, yr_ref, yi_ref):
    n1 = fc_ref.shape[0]
    y = _dot(ma_ref[...], jnp.concatenate([fc_ref[...], fs_ref[...]], axis=0))
    yr, yi = y[:n1, :], y[n1:, :]
    c, s = tc_ref[...], ts_ref[...]
    yr_ref[...] = (yr * c + yi * s).astype(BF16)
    yi_ref[...] = (yi * c - yr * s).astype(BF16)


def _fft_cols_kernel(yr_ref, yi_ref, mb_ref, o_ref):
    width = yr_ref.shape[2]
    for j in range(yr_ref.shape[0]):
        y = jnp.concatenate([yr_ref[j], yi_ref[j]], axis=0)
        o_ref[:, j * width:(j + 1) * width] = _dot(mb_ref[...], y).astype(BF16)


def _seq_dft(fc, fs):
    bsz, length, width = fc.shape
    n2 = GRID_W
    n1 = length // n2
    if n1 < 16 or length % n2:
        return _seq_dft_direct(fc, fs)
    ncol = n2 * width

    def angles(a, b, period):
        prod = (jnp.arange(a, dtype=jnp.int32)[:, None] * jnp.arange(b, dtype=jnp.int32)[None, :]) % period
        return prod.astype(F32) * (2.0 * math.pi / period)

    th = angles(n1, n1, n1)
    c1, s1 = jnp.cos(th) * n1 ** -0.5, jnp.sin(th) * n1 ** -0.5
    ma = jnp.concatenate([jnp.concatenate([c1, -s1], axis=1),
                          jnp.concatenate([-s1, -c1], axis=1)], axis=0).astype(BF16)
    ph = angles(n1, n2, length)
    tc = jnp.repeat(jnp.cos(ph), width, axis=1)
    ts = jnp.repeat(jnp.sin(ph), width, axis=1)
    ps = angles(n2, n2, n2)
    mb = (jnp.concatenate([jnp.cos(ps), jnp.sin(ps)], axis=1) * n2 ** -0.5).astype(BF16)

    cols = _divisor_tile(ncol, 4096)
    row_blk = lambda b, j: (b, 0, j)
    yr, yi = pl.pallas_call(
        _fft_rows_kernel,
        grid=(bsz, ncol // cols),
        in_specs=[
            pl.BlockSpec((None, n1, cols), row_blk),
            pl.BlockSpec((None, n1, cols), row_blk),
            pl.BlockSpec((2 * n1, 2 * n1), lambda b, j: (0, 0)),
            pl.BlockSpec((n1, cols), lambda b, j: (0, j)),
            pl.BlockSpec((n1, cols), lambda b, j: (0, j)),
        ],
        out_specs=[pl.BlockSpec((None, n1, cols), row_blk)] * 2,
        out_shape=[jax.ShapeDtypeStruct((bsz, n1, ncol), BF16)] * 2,
        compiler_params=_cparams("parallel", "parallel"),
    )(fc.reshape(bsz, n1, ncol), fs.reshape(bsz, n1, ncol), ma, tc, ts)

    kb = _divisor_tile(n1, 8, 1)
    out = pl.pallas_call(
        _fft_cols_kernel,
        grid=(bsz, n1 // kb),
        in_specs=[
            pl.BlockSpec((None, kb, n2, width), lambda b, k: (b, k, 0, 0)),
            pl.BlockSpec((None, kb, n2, width), lambda b, k: (b, k, 0, 0)),
            pl.BlockSpec((n2, 2 * n2), lambda b, k: (0, 0)),
        ],
        out_specs=pl.BlockSpec((None, n2, kb * width), lambda b, k: (b, 0, k)),
        out_shape=jax.ShapeDtypeStruct((bsz, n2, n1 * width), BF16),
        compiler_params=_cparams("parallel", "parallel"),
    )(yr.reshape(bsz, n1, n2, width), yi.reshape(bsz, n1, n2, width), mb)
    return out.reshape(bsz, length, width)


def _swiglu(x, wg_ref, wu_ref, wd_ref, chunk):
    ff = wg_ref.shape[1]
    y = jnp.zeros((x.shape[0], wd_ref.shape[1]), F32)
    for f in range(0, ff, chunk):
        cols = slice(f, min(f + chunk, ff))
        a = jax.nn.silu(_dot(x, wg_ref[:, cols])) * _dot(x, wu_ref[:, cols])
        y = y + _dot(a.astype(BF16), wd_ref[cols, :])
    return y


def _dense_ffn_kernel(h_ref, mod_ref, wg_ref, wu_ref, wd_ref, o_ref):
    h = h_ref[...]
    n = _modulate(h, mod_ref[3:4, :], mod_ref[4:5, :]).astype(BF16)
    o_ref[...] = h + mod_ref[5:6, :] * _swiglu(n, wg_ref, wu_ref, wd_ref, DENSE_FF_CHUNK)


def _divisor_tile(n, target, unit=LANES):
    best = None
    for t in range(unit, min(n, target) + 1, unit):
        if n % t == 0:
            best = t
    assert best is not None
    return best


def _dense_ffn(h, mod, wg, wu, wd):
    bsz, length, d = h.shape
    ff = wg.shape[1]
    tm = _row_tile(length, 512)
    tok = pl.BlockSpec((None, tm, d), lambda i, j: (i, j, 0))
    resident = pl.Buffered(1)
    return pl.pallas_call(
        _dense_ffn_kernel,
        grid=(bsz, length // tm),
        in_specs=[
            tok,
            pl.BlockSpec((None, 8, d), lambda i, j: (i, 0, 0)),
            pl.BlockSpec((d, ff), lambda i, j: (0, 0), pipeline_mode=resident),
            pl.BlockSpec((d, ff), lambda i, j: (0, 0), pipeline_mode=resident),
            pl.BlockSpec((ff, d), lambda i, j: (0, 0), pipeline_mode=resident),
        ],
        out_specs=tok,
        out_shape=jax.ShapeDtypeStruct((bsz, length, d), F32),
        compiler_params=_cparams("parallel", "parallel"),
    )(h, mod, wg, wu, wd)


def _pair_swap(x):
    width = x.shape[-1]
    lane = lax.broadcasted_iota(jnp.int32, x.shape, x.ndim - 1)
    nxt = pltpu.roll(x, width - 1, x.ndim - 1)
    prv = pltpu.roll(x, 1, x.ndim - 1)
    return jnp.where(lane % 2 == 0, nxt, prv)


def _rope(x, cos, sin):
    reps = x.shape[1] // LANES
    cos = jnp.concatenate([cos] * reps, axis=1) if reps > 1 else cos
    sin = jnp.concatenate([sin] * reps, axis=1) if reps > 1 else sin
    return x * cos + _pair_swap(x) * sin


def _cd_in_kernel(h_ref, mod_ref, w_ref, gm_ref, qg_ref, kg_ref, cos_ref, sin_ref,
                  cq_ref, ck_ref, cv_ref, dq_ref, dk_ref, dv_ref, *, rope):
    n = _modulate(h_ref[...], mod_ref[0:1, :], mod_ref[1:2, :])
    z = _dot(n.astype(BF16), w_ref[...])
    qw, kw, dqw, dw = cq_ref.shape[1], ck_ref.shape[1], dq_ref.shape[1], dk_ref.shape[1]
    o1, o2, o3, o4, o5 = qw, qw + kw, qw + 2 * kw, qw + 2 * kw + dqw, qw + 2 * kw + dqw + dw
    scale = HEAD_DIM ** -0.5

    def head_norm(t, gain):
        ms = _dot((t * t).astype(BF16), gm_ref[:t.shape[1], :t.shape[1]])
        return t * lax.rsqrt(ms + EPS) * gain

    cq = head_norm(z[:, :o1], qg_ref[...])
    ck = head_norm(z[:, o1:o2], kg_ref[...])
    if rope:
        cq = _rope(cq, cos_ref[...], sin_ref[...])
        ck = _rope(ck, cos_ref[...], sin_ref[...])
    scale = scale * LOG2E
    cq_ref[...] = (cq * scale).astype(BF16)
    ck_ref[...] = ck.astype(BF16)
    cv_ref[...] = z[:, o2:o3].astype(BF16)
    dq_ref[...] = (z[:, o3:o4] * scale).astype(BF16)
    dk_ref[...] = z[:, o4:o5].astype(BF16)
    dv_ref[...] = z[:, o5:].astype(BF16)


def _project_cd(h, mod, p, rope):
    bsz, length, d = h.shape
    qw, kw, dw = C_HEADS * HEAD_DIM, C_KV_HEADS * HEAD_DIM, D_HEADS * HEAD_DIM
    tm = _row_tile(length, 512)
    const = lambda i, j: (0, 0)
    tok = lambda w: pl.BlockSpec((None, tm, w), lambda i, j: (i, j, 0))
    widths = (qw, kw, kw, dw, dw, dw)
    return pl.pallas_call(
        functools.partial(_cd_in_kernel, rope=rope),
        grid=(bsz, length // tm),
        in_specs=[
            tok(d),
            pl.BlockSpec((None, 8, d), lambda i, j: (i, 0, 0)),
            pl.BlockSpec(p["w_in"].shape, const),
            pl.BlockSpec(p["gmean"].shape, const),
            pl.BlockSpec(p["q_gain"].shape, const),
            pl.BlockSpec(p["k_gain"].shape, const),
            pl.BlockSpec((tm, LANES), lambda i, j: (j, 0)),
            pl.BlockSpec((tm, LANES), lambda i, j: (j, 0)),
        ],
        out_specs=[tok(w) for w in widths],
        out_shape=[jax.ShapeDtypeStruct((bsz, length, w), BF16) for w in widths],
        compiler_params=_cparams("parallel", "parallel"),
    )(h, mod, p["w_in"], p["gmean"], p["q_gain"], p["k_gain"], p["cos"], p["sin"])


def _gqa_kernel(q_ref, kt_ref, v_ref, o_ref):
    tq = q_ref.shape[0]
    sub = tq // GQA_SPLIT
    lane = lax.broadcasted_iota(jnp.int32, (sub, LANES), 1)

    def attend(h, rows):
        block = q_ref[rows, (h % C_GROUP) * LANES:(h % C_GROUP + 1) * LANES]
        own = (lane >= HEAD_DIM) if h // C_GROUP else (lane < HEAD_DIM)
        s = _dot(jnp.where(own, block, jnp.zeros_like(block)), kt_ref[...])
        p = jnp.exp2(s - jnp.max(s, axis=-1, keepdims=True)).astype(BF16)
        o = _dot(p, v_ref[...])
        return o[:, :LANES] / o[:, LANES:LANES + 1]

    for pair in range(C_HEADS // 2):
        for part in range(GQA_SPLIT):
            rows = slice(part * sub, (part + 1) * sub)
            a, b = attend(2 * pair, rows), attend(2 * pair + 1, rows)
            if (2 * pair) // C_GROUP == 0:
                blk = jnp.where(lane < HEAD_DIM, a, pltpu.roll(b, HEAD_DIM, 1))
            else:
                blk = jnp.where(lane < HEAD_DIM, pltpu.roll(a, HEAD_DIM, 1), b)
            o_ref[rows, pair * LANES:(pair + 1) * LANES] = blk.astype(BF16)


def _gqa(q_paired, k_all, v_all):
    bsz, length, qw = q_paired.shape
    lk = k_all.shape[1]
    assert C_KV_HEADS * HEAD_DIM == LANES and C_KV_HEADS == 2
    kt = jnp.swapaxes(k_all, 1, 2)
    v_ext = jnp.concatenate([v_all, jnp.ones_like(v_all)], axis=2)
    tq = _row_tile(length, 512)
    ow = C_HEADS * HEAD_DIM
    return pl.pallas_call(
        _gqa_kernel,
        grid=(bsz, length // tq),
        in_specs=[
            pl.BlockSpec((None, tq, qw), lambda i, j: (i, j, 0)),
            pl.BlockSpec((None, LANES, lk), lambda i, j: (i, 0, 0)),
            pl.BlockSpec((None, lk, 2 * LANES), lambda i, j: (i, 0, 0)),
        ],
        out_specs=pl.BlockSpec((None, tq, ow), lambda i, j: (i, j, 0)),
        out_shape=jax.ShapeDtypeStruct((bsz, length, ow), BF16),
        compiler_params=_cparams("parallel", "parallel"),
    )(q_paired, kt, v_ext)


def _na_block_start(qb, rows):
    return jnp.clip(qb * NA_QROWS - NA_WIN_H // 2, 0, rows - NA_KROWS)


def _na_bias_tables(rel_bias, rows):
    nblk = rows // NA_QROWS
    n_dr, n_dc = 2 * NA_WIN_H - 1, 2 * NA_WIN_W - 1
    i = np.arange(NA_QROWS)[:, None]
    a = np.arange(NA_KROWS)[None, :]
    j = np.arange(GRID_W)[:, None]
    kc = np.arange(GRID_W)[None, :]
    col_start = np.clip(j - NA_WIN_W // 2, 0, GRID_W - NA_WIN_W)
    valid_col = (kc >= col_start) & (kc < col_start + NA_WIN_W)
    dc = np.clip(kc - j + (NA_WIN_W - 1), 0, n_dc - 1)
    onehot_c = (dc[:, :, None] == np.arange(n_dc)).astype(np.float32)
    onehot_r, valid = [], []
    for qb in (0, 1, nblk - 1):
        r = qb * NA_QROWS + i
        r0 = np.clip(r - NA_WIN_H // 2, 0, rows - NA_WIN_H)
        kr = int(np.clip(qb * NA_QROWS - NA_WIN_H // 2, 0, rows - NA_KROWS)) + a
        valid_row = (kr >= r0) & (kr < r0 + NA_WIN_H)
        dr = np.clip(kr - r + (NA_WIN_H - 1), 0, n_dr - 1)
        onehot_r.append((dr[:, :, None] == np.arange(n_dr)).astype(np.float32))
        valid.append(valid_row[:, None, :, None] & valid_col[None, :, None, :])
    onehot_r = jnp.asarray(np.stack(onehot_r))
    valid = np.stack(valid)
    hp = lax.Precision.HIGHEST
    by_row = jnp.einsum("hrc,ziar->zhiac", rel_bias.astype(F32), onehot_r, precision=hp)
    table = jnp.einsum("zhiac,jkc->zhijak", by_row, jnp.asarray(onehot_c), precision=hp)
    table = jnp.where(valid[:, None], table * LOG2E, MASK_VALUE)
    return table.reshape(3 * rel_bias.shape[0], NA_QROWS * GRID_W, NA_KROWS * GRID_W)


def _na_kernel(q_ref, k_ref, v_ref, kc_ref, vc_ref, bias_ref, o_ref, *, rows):
    qb = pl.program_id(1)
    nk = NA_KROWS * GRID_W
    start = pl.multiple_of(_na_block_start(qb, rows) * GRID_W, GRID_W)
    k_all = jnp.concatenate([k_ref[pl.ds(start, nk), :], kc_ref[...]], axis=0)
    v_all = jnp.concatenate([v_ref[pl.ds(start, nk), :], vc_ref[...]], axis=0)
    ones = jnp.ones((k_all.shape[0], LANES), BF16)
    sub = q_ref.shape[0] // NA_SPLIT
    lane = lax.broadcasted_iota(jnp.int32, (sub, LANES), 1)

    def attend(h, rows, k_pair, v_pair):
        block = q_ref[rows, (h // 2) * LANES:(h // 2 + 1) * LANES]
        own = (lane >= HEAD_DIM) if h % 2 else (lane < HEAD_DIM)
        s = _dot_nt(jnp.where(own, block, jnp.zeros_like(block)), k_pair)
        s = jnp.concatenate([s[:, :nk] + bias_ref[h, rows, :], s[:, nk:]], axis=1)
        p = jnp.exp2(s - jnp.max(s, axis=-1, keepdims=True)).astype(BF16)
        o = _dot(p, v_pair)
        return o[:, :LANES] / o[:, LANES:LANES + 1]

    for pair in range(D_HEADS // 2):
        cols = slice(pair * LANES, (pair + 1) * LANES)
        k_pair = k_all[:, cols]
        v_pair = jnp.concatenate([v_all[:, cols], ones], axis=1)
        for part in range(NA_SPLIT):
            rows = slice(part * sub, (part + 1) * sub)
            even, odd = attend(2 * pair, rows, k_pair, v_pair), attend(2 * pair + 1, rows, k_pair, v_pair)
            o_ref[rows, cols] = jnp.where(lane < HEAD_DIM, even, odd).astype(BF16)


def _neighbourhood(dq, dk, dv, dk_c, dv_c, bias):
    bsz, length, w = dk.shape
    rows = length // GRID_W
    assert rows % NA_QROWS == 0 and rows >= NA_KROWS
    nblk = rows // NA_QROWS
    tq = NA_QROWS * GRID_W
    cl = dk_c.shape[1]

    def bias_class(i, j):
        cls = jnp.where(j == 0, 0, jnp.where(j == nblk - 1, 2, 1))
        return (cls, 0, 0)

    return pl.pallas_call(
        functools.partial(_na_kernel, rows=rows),
        grid=(bsz, nblk),
        in_specs=[
            pl.BlockSpec((None, tq, dq.shape[2]), lambda i, j: (i, j, 0)),
            pl.BlockSpec((None, length, w), lambda i, j: (i, 0, 0)),
            pl.BlockSpec((None, length, w), lambda i, j: (i, 0, 0)),
            pl.BlockSpec((None, cl, w), lambda i, j: (i, 0, 0)),
            pl.BlockSpec((None, cl, w), lambda i, j: (i, 0, 0)),
            pl.BlockSpec((D_HEADS, tq, NA_KROWS * GRID_W), bias_class),
        ],
        out_specs=pl.BlockSpec((None, tq, w), lambda i, j: (i, j, 0)),
        out_shape=jax.ShapeDtypeStruct((bsz, length, w), BF16),
        compiler_params=_cparams("parallel", "arbitrary"),
    )(dq, dk, dv, dk_c, dv_c, bias)


def _router_kernel(h_ref, mod_ref, wr_ref, n_ref, sel_ref, cnt_ref):
    n = _modulate(h_ref[...], mod_ref[3:4, :], mod_ref[4:5, :])
    n_hi = n.astype(BF16)
    n_lo = (n - n_hi.astype(F32)).astype(BF16)
    w = wr_ref[...]
    w_hi = w.astype(BF16)
    w_lo = (w - w_hi.astype(F32)).astype(BF16)
    logits = _dot(n_hi, w_hi) + (_dot(n_lo, w_hi) + _dot(n_hi, w_lo))
    lane = lax.broadcasted_iota(jnp.int32, logits.shape, 1)
    logits = jnp.where(lane < N_EXPERTS, logits, -jnp.inf)
    m1 = jnp.max(logits, axis=-1, keepdims=True)
    i1 = jnp.min(jnp.where(logits == m1, lane, LANES), axis=-1, keepdims=True)
    rest = jnp.where(lane == i1, -jnp.inf, logits)
    m2 = jnp.max(rest, axis=-1, keepdims=True)
    i2 = jnp.min(jnp.where(rest == m2, lane, LANES), axis=-1, keepdims=True)
    e2 = jnp.exp(m2 - m1)
    w1 = 1.0 / (1.0 + e2)
    w2 = e2 / (1.0 + e2)

    ntok = logits.shape[0]
    chosen = jnp.where(lane == i1, 1.0, jnp.where(lane == i2, 1.0, 0.0))
    earlier = (lax.broadcasted_iota(jnp.int32, (ntok, ntok), 1)
               < lax.broadcasted_iota(jnp.int32, (ntok, ntok), 0))
    before = _dot(jnp.where(earlier, 1.0, 0.0).astype(BF16), chosen.astype(BF16))
    count = jnp.sum(chosen, axis=0, keepdims=True).astype(jnp.int32)
    npiece = lax.shift_right_logical(count + (MOE_PIECE - 1), jnp.full_like(count, MOE_PIECE.bit_length() - 1))
    lower = (lax.broadcasted_iota(jnp.int32, (LANES, LANES), 0)
             < lax.broadcasted_iota(jnp.int32, (LANES, LANES), 1))
    npiece_rows = jnp.broadcast_to(npiece.astype(F32), (8, LANES)).astype(BF16)
    piece_off = _dot(npiece_rows, jnp.where(lower, 1.0, 0.0).astype(BF16))[0:1, :]
    row = piece_off * float(MOE_PIECE) + before
    pos1 = jnp.sum(jnp.where(lane == i1, row, 0.0), axis=-1, keepdims=True)
    pos2 = jnp.sum(jnp.where(lane == i2, row, 0.0), axis=-1, keepdims=True)

    sel = jnp.where(lane == 0, i1.astype(F32), jnp.where(lane == 1, i2.astype(F32),
                    jnp.where(lane == 2, w1, jnp.where(lane == 3, w2,
                              jnp.where(lane == 4, pos1, jnp.where(lane == 5, pos2, 0.0))))))
    sel_ref[...] = sel
    cnt_ref[...] = jnp.broadcast_to(npiece.astype(F32), cnt_ref.shape)
    n_ref[...] = n_hi


def _router(h, mod, w_router_padded):
    bsz, length, d = h.shape
    tm = _row_tile(length, MOE_TOKENS)
    tiles = length // tm
    tok = lambda w: pl.BlockSpec((None, tm, w), lambda i, j: (i, j, 0))
    return pl.pallas_call(
        _router_kernel,
        grid=(bsz, tiles),
        in_specs=[
            tok(d),
            pl.BlockSpec((None, 8, d), lambda i, j: (i, 0, 0)),
            pl.BlockSpec((d, LANES), lambda i, j: (0, 0)),
        ],
        out_specs=[tok(d), tok(LANES), pl.BlockSpec((None, None, 8, LANES), lambda i, j: (i, j, 0, 0))],
        out_shape=[
            jax.ShapeDtypeStruct((bsz, length, d), BF16),
            jax.ShapeDtypeStruct((bsz, length, LANES), F32),
            jax.ShapeDtypeStruct((bsz, tiles, 8, LANES), F32),
        ],
        compiler_params=_cparams("parallel", "parallel"),
    )(h, mod, w_router_padded)


def _moe_dispatch_kernel(pos_ref, w_ref, n_ref, xs_ref, gs_ref):
    nrow = xs_ref.shape[0]
    base = pl.program_id(1) * nrow
    row = base + lax.broadcasted_iota(jnp.int32, (nrow, 1), 0)
    eq0 = pos_ref[0:1, :] == row
    eq1 = pos_ref[1:2, :] == row
    perm = jnp.where(eq0, 1.0, jnp.where(eq1, 1.0, 0.0)).astype(BF16)
    xs_ref[...] = _dot(perm, n_ref[...]).astype(BF16)
    gate = jnp.where(eq0, w_ref[0:1, :], jnp.where(eq1, w_ref[1:2, :], 0.0))
    gs_ref[...] = jnp.sum(gate, axis=-1, keepdims=True)


def _moe_expert_kernel(piece_ref, exp_ref, nused_ref, *refs):
    xs_refs, gs_refs = refs[:MOE_GROUP], refs[MOE_GROUP:2 * MOE_GROUP]
    wg_ref, wu_ref, wd_ref, ys_ref = refs[2 * MOE_GROUP:]
    i = pl.program_id(0)

    @pl.when(i < nused_ref[0])
    def _():
        x = jnp.concatenate([r[...] for r in xs_refs], axis=0)
        gate = jnp.concatenate([r[...] for r in gs_refs], axis=0)
        ys_ref[...] = (_swiglu(x, wg_ref, wu_ref, wd_ref, MOE_FF_CHUNK) * gate).astype(BF16)

    @pl.when(i >= nused_ref[0])
    def _():
        ys_ref[...] = jnp.zeros_like(ys_ref)


def _moe_combine_kernel(slot_ref, pos_ref, h_ref, mod_ref, fg_ref, *refs):
    ys = jnp.concatenate([r[...] for r in refs[:-1]], axis=0)
    o_ref = refs[-1]
    nrow = ys.shape[0]
    sub = pos_ref.shape[0] // MOE_COMBINE_SPLIT
    row = lax.broadcasted_iota(jnp.int32, (sub, nrow), 1)
    for part in range(MOE_COMBINE_SPLIT):
        toks = slice(part * sub, (part + 1) * sub)
        perm = jnp.where(pos_ref[toks, 0:1] == row, 1.0,
                         jnp.where(pos_ref[toks, 1:2] == row, 1.0, 0.0)).astype(BF16)
        y = h_ref[toks, :] + mod_ref[5:6, :] * _dot(perm, ys)
        ms = jnp.mean(y * y, axis=-1, keepdims=True)
        o_ref[toks, :] = y * lax.rsqrt(ms + EPS) * fg_ref[...]


def _moe_plan(sel, npiece, n_tiles, tile_tokens, rows_per_tile):
    pieces_per_tile = rows_per_tile // MOE_PIECE
    weights = sel[:, TOP_K:2 * TOP_K].reshape(n_tiles, tile_tokens, TOP_K)
    pos = sel[:, 2 * TOP_K:3 * TOP_K].astype(jnp.int32).reshape(n_tiles, tile_tokens, TOP_K)
    piece_end = jnp.cumsum(npiece.astype(jnp.int32), axis=1)

    b = jnp.arange(pieces_per_tile, dtype=jnp.int32)
    piece_expert = jnp.sum((piece_end[:, None, :] <= b[None, :, None]).astype(jnp.int32), axis=-1)
    flat_expert = piece_expert.reshape(-1)
    n_pieces = flat_expert.shape[0]
    n_slots = n_pieces + N_EXPERTS * MOE_GROUP
    classes = jnp.arange(N_EXPERTS + 1, dtype=jnp.int32)
    cls_onehot = (flat_expert[:, None] == classes).astype(jnp.int32)
    cls_csum = jnp.cumsum(cls_onehot, axis=0)
    piece_rank = jnp.sum((cls_csum - cls_onehot) * cls_onehot, axis=-1)
    n_cls = cls_csum[-1]
    groups = (n_cls[:N_EXPERTS] + MOE_GROUP - 1) // MOE_GROUP
    start = (jnp.cumsum(groups) - groups) * MOE_GROUP
    n_used_groups = jnp.sum(groups)

    slot = jnp.arange(n_slots, dtype=jnp.int32)
    slot_expert = jnp.sum((start[None, :] <= slot[:, None]).astype(jnp.int32), axis=-1) - 1
    slot_taken = (slot - start[slot_expert]) < n_cls[slot_expert]
    free_slots = jnp.argsort(slot_taken.astype(jnp.int32) * n_slots + slot).astype(jnp.int32)
    is_used = flat_expert < N_EXPERTS
    safe_expert = jnp.minimum(flat_expert, N_EXPERTS - 1)
    piece_slot = jnp.where(is_used, start[safe_expert] + piece_rank, free_slots[piece_rank])
    first_unused = jnp.argmax(jnp.logical_not(is_used)).astype(jnp.int32)
    piece_ids = jnp.arange(n_pieces, dtype=jnp.int32)
    slot_piece = jnp.full((n_slots,), first_unused, jnp.int32).at[piece_slot].set(piece_ids, unique_indices=True)
    step_expert = slot_expert[::MOE_GROUP]
    return pos, weights, slot_piece, piece_slot.astype(jnp.int32), step_expert, n_used_groups.reshape(1)


def _moe_final(n2, sel, npiece, h, mod, final_gain, wg, wu, wd):
    bsz, length, d = h.shape
    n_exp, _, ff = wg.shape
    tokens = bsz * length
    tt = _row_tile(length, MOE_TOKENS)
    n_tiles = tokens // tt
    rows = TOP_K * tt + N_EXPERTS * MOE_PIECE
    drows = _divisor_tile(rows, MOE_DISPATCH_ROWS, MOE_PIECE)
    ctok = _row_tile(tt, MOE_COMBINE_TOKENS)
    pos, weights, slot_piece, piece_slot, step_expert, n_used = _moe_plan(
        sel.reshape(tokens, LANES), npiece.reshape(n_tiles, 8, LANES)[:, 0, :N_EXPERTS], n_tiles, tt, rows)

    pos_rows = jnp.swapaxes(pos, 1, 2)
    w_rows = jnp.swapaxes(weights, 1, 2)
    xs, gs = pl.pallas_call(
        _moe_dispatch_kernel,
        grid=(n_tiles, rows // drows),
        in_specs=[
            pl.BlockSpec((None, TOP_K, tt), lambda t, r: (t, 0, 0)),
            pl.BlockSpec((None, TOP_K, tt), lambda t, r: (t, 0, 0)),
            pl.BlockSpec((tt, d), lambda t, r: (t, 0)),
        ],
        out_specs=[
            pl.BlockSpec((None, drows, d), lambda t, r: (t, r, 0)),
            pl.BlockSpec((None, drows, 1), lambda t, r: (t, r, 0)),
        ],
        out_shape=[
            jax.ShapeDtypeStruct((n_tiles, rows, d), BF16),
            jax.ShapeDtypeStruct((n_tiles, rows, 1), F32),
        ],
        compiler_params=_cparams("parallel", "parallel"),
    )(pos_rows, w_rows, n2.reshape(tokens, d))

    n_steps = slot_piece.shape[0] // MOE_GROUP
    resident = pl.Buffered(1)

    def piece_spec(j, width):
        return pl.BlockSpec((MOE_PIECE, width), lambda i, sp, ex, nu: (sp[i * MOE_GROUP + j], 0))

    def weight_spec(shape):
        return pl.BlockSpec((None,) + shape, lambda i, sp, ex, nu: (ex[i], 0, 0), pipeline_mode=resident)

    xs_flat, gs_flat = xs.reshape(n_tiles * rows, d), gs.reshape(n_tiles * rows, 1)
    step_rows = MOE_GROUP * MOE_PIECE
    ys = pl.pallas_call(
        _moe_expert_kernel,
        grid_spec=pltpu.PrefetchScalarGridSpec(
            num_scalar_prefetch=3,
            grid=(n_steps,),
            in_specs=([piece_spec(j, d) for j in range(MOE_GROUP)]
                      + [piece_spec(j, 1) for j in range(MOE_GROUP)]
                      + [weight_spec((d, ff)), weight_spec((d, ff)), weight_spec((ff, d))]),
            out_specs=pl.BlockSpec((step_rows, d), lambda i, sp, ex, nu: (i, 0)),
        ),
        out_shape=jax.ShapeDtypeStruct((n_steps * step_rows, d), BF16),
        compiler_params=_cparams("arbitrary"),
    )(slot_piece, step_expert, n_used, *([xs_flat] * MOE_GROUP), *([gs_flat] * MOE_GROUP), wg, wu, wd)

    tiles_per_seq = length // tt
    pieces_per_tile = rows // MOE_PIECE

    def tile_piece_spec(k):
        return pl.BlockSpec((MOE_PIECE, d), lambda t, c, slot: (slot[t * pieces_per_tile + k], 0))

    out = pl.pallas_call(
        _moe_combine_kernel,
        grid_spec=pltpu.PrefetchScalarGridSpec(
            num_scalar_prefetch=1,
            grid=(n_tiles, tt // ctok),
            in_specs=[
                pl.BlockSpec((None, ctok, TOP_K), lambda t, c, slot: (t, c, 0)),
                pl.BlockSpec((None, ctok, d), lambda t, c, slot: (t, c, 0)),
                pl.BlockSpec((None, 8, d), lambda t, c, slot: (t // tiles_per_seq, 0, 0)),
                pl.BlockSpec((1, d), lambda t, c, slot: (0, 0)),
            ] + [tile_piece_spec(k) for k in range(pieces_per_tile)],
            out_specs=pl.BlockSpec((None, ctok, d), lambda t, c, slot: (t, c, 0)),
        ),
        out_shape=jax.ShapeDtypeStruct((n_tiles, tt, d), F32),
        compiler_params=_cparams("parallel", "parallel"),
    )(piece_slot, pos, h.reshape(n_tiles, tt, d), mod, final_gain, *([ys] * pieces_per_tile))
    return out.reshape(bsz, length, d)


def _group_mean_matrix(width, group):
    return jnp.asarray(np.kron(np.eye(width // group), np.full((group, group), 1.0 / group)), BF16)


def _rope_tables(length):
    t = jnp.arange(length, dtype=jnp.int32)
    row = (t // GRID_W).astype(F32)
    col = (t % GRID_W).astype(F32)
    n_axis = HEAD_DIM // 4
    inv_freq = ROPE_THETA ** (-jnp.arange(n_axis, dtype=F32) / n_axis)
    ang = jnp.concatenate([row[:, None] * inv_freq, col[:, None] * inv_freq], axis=-1)
    cos = jnp.repeat(jnp.cos(ang), 2, axis=-1)
    sin = jnp.repeat(jnp.sin(ang), 2, axis=-1)
    sign = jnp.tile(jnp.asarray([-1.0, 1.0], F32), HEAD_DIM // 2)
    reps = LANES // HEAD_DIM
    return jnp.tile(cos, (1, reps)), jnp.tile(sin * sign, (1, reps))


def kernel(x, c, ctx, c_ctx, w_ada, b_ada, w_in_ab, v_gain, w_spatial, b_spatial, w_out_ab,
           w_gate_dense, w_up_dense, w_down_dense, w_in_cd, q_gain, k_gain, rel_bias, w_out_cd,
           w_router, w_gate_moe, w_up_moe, w_down_moe, final_gain):
    bsz, length, d = x.shape
    depth = w_ada.shape[0]
    assert depth == 2 and bsz <= 8
    aw = v_gain.shape[1]
    group_dim = aw // A_GROUPS
    bgd = (w_in_ab.shape[2] - 2 * aw) // B_GROUPS

    cond = jnp.zeros((16, d), F32).at[:bsz].set(c).at[8].set(c_ctx)
    ada = _adaln(cond, w_ada, b_ada).reshape(depth, 16, 6, d)
    pad = jnp.zeros((depth, 16, 2, d), F32)
    ada = jnp.concatenate([ada, pad], axis=2)
    mod_lat = [ada[i, :bsz] for i in range(depth)]
    mod_ctx = [jnp.broadcast_to(ada[i, 8], (bsz, 8, d)) for i in range(depth)]

    cmat, smat = _dft_tables(bgd, bgd ** -0.5)
    p_ab = dict(
        w_in=w_in_ab[0].astype(BF16),
        v_gain=v_gain[0].reshape(1, aw),
        gmean=_group_mean_matrix(aw, group_dim),
        w_sp=w_spatial[0].transpose(1, 0, 2).reshape(CHUNK, A_GROUPS * CHUNK).astype(BF16),
        b_sp=jnp.repeat(b_spatial[0].T, group_dim, axis=1),
        cmat=cmat, smat=smat,
        w_out=w_out_ab[0].astype(BF16),
    )
    wg0, wu0, wd0 = (w.astype(BF16) for w in (w_gate_dense[0], w_up_dense[0], w_down_dense[0]))
    h = _proj_residual(*_mix_ab(x, mod_lat[0], p_ab), p_ab["w_out"], x, mod_lat[0])
    hc = _proj_residual(*_mix_ab(ctx, mod_ctx[0], p_ab), p_ab["w_out"], ctx, mod_ctx[0])
    h = _dense_ffn(h, mod_lat[0], wg0, wu0, wd0)
    hc = _dense_ffn(hc, mod_ctx[0], wg0, wu0, wd0)

    cos, sin = _rope_tables(length)
    qw = C_HEADS * HEAD_DIM
    kw = C_KV_HEADS * HEAD_DIM
    w_cd = w_in_cd[0]
    w_q = w_cd[:, :qw].reshape(d, C_KV_HEADS, C_GROUP, HEAD_DIM).transpose(0, 2, 1, 3).reshape(d, qw)
    p_cd = dict(
        w_in=jnp.concatenate([w_q, w_cd[:, qw:]], axis=1).astype(BF16),
        gmean=_group_mean_matrix(qw, HEAD_DIM),
        q_gain=jnp.tile(q_gain[0], C_HEADS).reshape(1, qw),
        k_gain=jnp.tile(k_gain[0], C_KV_HEADS).reshape(1, kw),
        cos=cos, sin=sin,
    )
    cq, ck, cv, dq, dk, dv = _project_cd(h, mod_lat[1], p_cd, rope=True)
    p_cd_ctx = dict(p_cd, cos=cos[:hc.shape[1]], sin=sin[:hc.shape[1]])
    _, ck_c, cv_c, _, dk_c, dv_c = _project_cd(hc, mod_ctx[1], p_cd_ctx, rope=False)
    o_c = _gqa(cq, jnp.concatenate([ck_c, ck], axis=1), jnp.concatenate([cv_c, cv], axis=1))
    o_d = _neighbourhood(dq, dk, dv, dk_c, dv_c, _na_bias_tables(rel_bias[0], length // GRID_W))
    wr = jnp.zeros((d, LANES), F32).at[:, :N_EXPERTS].set(w_router[0])
    h = _proj_residual(o_c, o_d, w_out_cd[0].astype(BF16), h, mod_lat[1])
    n2, sel, npiece = _router(h, mod_lat[1], wr)
    return _moe_final(n2, sel, npiece, h, mod_lat[1], final_gain.reshape(1, d),
                      w_gate_moe[0].astype(BF16), w_up_moe[0].astype(BF16), w_down_moe[0].astype(BF16))
```

```python
import functools
import math

import numpy as np
import jax
import jax.numpy as jnp
from jax import lax
from jax.experimental import pallas as pl
from jax.experimental.pallas import tpu as pltpu

F32 = jnp.float32
BF16 = jnp.bfloat16

GRID_W = 64
EPS = 1e-6
CHUNK = 128
A_GROUPS = 8
B_GROUPS = 4
HEAD_DIM = 64
C_HEADS = 8
C_KV_HEADS = 2
C_GROUP = C_HEADS // C_KV_HEADS
D_HEADS = 8
NA_WIN_H = 8
NA_WIN_W = 16
ROPE_THETA = 10000.0
N_EXPERTS = 8
TOP_K = 2

LANES = 128
GQA_SPLIT = 4
NA_QROWS = 4
NA_SPLIT = 2
NA_KROWS = NA_QROWS + NA_WIN_H
MASK_VALUE = -1e30
LOG2E = 1.4426950408889634
VMEM_LIMIT = 56 * 1024 * 1024

DENSE_FF_CHUNK = 1536
MOE_TOKENS = 1024
MOE_PIECE = 32
MOE_GROUP = 8
MOE_DISPATCH_ROWS = 1280
MOE_COMBINE_TOKENS = 1024
MOE_COMBINE_SPLIT = 4
MOE_FF_CHUNK = 1792


def _cparams(*sem):
    return pltpu.CompilerParams(dimension_semantics=sem, vmem_limit_bytes=VMEM_LIMIT)


def _modulate(h, shift, scale):
    ms = jnp.mean(h * h, axis=-1, keepdims=True)
    return h * lax.rsqrt(ms + EPS) * (1.0 + scale) + shift


def _dot(a, b):
    return jnp.dot(a, b, preferred_element_type=F32)


def _dot_nt(a, b):
    return lax.dot_general(a, b, (((1,), (1,)), ((), ())), preferred_element_type=F32)


def _row_tile(length, target):
    t = min(length, target)
    assert length % t == 0
    return t


def _adaln_kernel(c_ref, w_ref, b_ref, o_ref):
    s = jax.nn.silu(c_ref[...]).astype(BF16)
    o_ref[...] = _dot(s, w_ref[...].astype(BF16)) + b_ref[...]


def _adaln(cond, w_ada, b_ada):
    depth, d, n = w_ada.shape
    r = cond.shape[0]
    tn = 512
    return pl.pallas_call(
        _adaln_kernel,
        grid=(depth, n // tn),
        in_specs=[
            pl.BlockSpec((r, d), lambda i, j: (0, 0)),
            pl.BlockSpec((None, d, tn), lambda i, j: (i, 0, j)),
            pl.BlockSpec((None, 1, tn), lambda i, j: (i, 0, j)),
        ],
        out_specs=pl.BlockSpec((None, r, tn), lambda i, j: (i, 0, j)),
        out_shape=jax.ShapeDtypeStruct((depth, r, n), F32),
        compiler_params=_cparams("parallel", "parallel"),
    )(cond, w_ada, b_ada.reshape(depth, 1, n))


def _ab_in_kernel(h_ref, mod_ref, w_ref, vg_ref, gm_ref, wsp_ref, bsp_ref, cm_ref, sm_ref,
                  a_ref, fc_ref, fs_ref, *, aw):
    tm = h_ref.shape[0]
    n = _modulate(h_ref[...], mod_ref[0:1, :], mod_ref[1:2, :])
    z = _dot(n.astype(BF16), w_ref[...])
    u = jax.nn.gelu(z[:, :aw])
    v = jax.nn.gelu(z[:, aw:2 * aw])
    fb = z[:, 2 * aw:].astype(BF16)
    ms = _dot((v * v).astype(BF16), gm_ref[...])
    vn = v * lax.rsqrt(ms + EPS) * vg_ref[...]
    group_dim = aw // A_GROUPS
    lane_group = lax.broadcasted_iota(jnp.int32, (CHUNK, aw), 1) // group_dim
    for c in range(tm // CHUNK):
        rows = slice(c * CHUNK, (c + 1) * CHUNK)
        vc = vn[rows, :]
        stack = jnp.concatenate(
            [jnp.where(lane_group == g, vc, 0.0).astype(BF16) for g in range(A_GROUPS)], axis=0)
        s = _dot(wsp_ref[...], stack) + bsp_ref[...]
        a_ref[rows, :] = (u[rows, :] * s).astype(BF16)
    bw = fb.shape[1] // B_GROUPS
    for g in range(B_GROUPS):
        cols = slice(g * bw, (g + 1) * bw)
        fc_ref[:, cols] = _dot(fb[:, cols], cm_ref[...]).astype(BF16)
        fs_ref[:, cols] = _dot(fb[:, cols], sm_ref[...]).astype(BF16)


def _seq_dft_kernel(c_ref, s_ref, fc_ref, fs_ref, o_ref):
    o_ref[...] = (_dot(c_ref[...], fc_ref[...]) - _dot(s_ref[...], fs_ref[...])).astype(BF16)


def _proj_residual_kernel(a_ref, b_ref, w_ref, h_ref, mod_ref, o_ref):
    ka = a_ref.shape[1]
    y = _dot(a_ref[...], w_ref[:ka, :]) + _dot(b_ref[...], w_ref[ka:, :])
    o_ref[...] = h_ref[...] + mod_ref[2:3, :] * y


def _proj_residual(a, b, w, h, mod):
    bsz, length, d = h.shape
    tm = _row_tile(length, 1024)
    ka, kb = a.shape[2], b.shape[2]
    return pl.pallas_call(
        _proj_residual_kernel,
        grid=(bsz, length // tm),
        in_specs=[
            pl.BlockSpec((None, tm, ka), lambda i, j: (i, j, 0)),
            pl.BlockSpec((None, tm, kb), lambda i, j: (i, j, 0)),
            pl.BlockSpec((ka + kb, d), lambda i, j: (0, 0)),
            pl.BlockSpec((None, tm, d), lambda i, j: (i, j, 0)),
            pl.BlockSpec((None, 8, d), lambda i, j: (i, 0, 0)),
        ],
        out_specs=pl.BlockSpec((None, tm, d), lambda i, j: (i, j, 0)),
        out_shape=jax.ShapeDtypeStruct((bsz, length, d), F32),
        compiler_params=_cparams("parallel", "parallel"),
    )(a, b, w, h, mod)


def _dft_tables(n, scale):
    k = jnp.arange(n, dtype=jnp.int32)
    ang = ((k[:, None] * k[None, :]) % n).astype(F32) * (2.0 * math.pi / n)
    return (jnp.cos(ang) * scale).astype(BF16), (jnp.sin(ang) * scale).astype(BF16)


def _mix_ab(h, mod, p):
    bsz, length, d = h.shape
    aw = p["v_gain"].shape[1]
    bw_total = p["w_in"].shape[1] - 2 * aw
    const = lambda i, j: (0, 0)
    tm = _row_tile(length, 512)
    f_spec = pl.BlockSpec((None, tm, bw_total), lambda i, j: (i, j, 0))
    f_shape = jax.ShapeDtypeStruct((bsz, length, bw_total), BF16)
    a_out, fc, fs = pl.pallas_call(
        functools.partial(_ab_in_kernel, aw=aw),
        grid=(bsz, length // tm),
        in_specs=[
            pl.BlockSpec((None, tm, d), lambda i, j: (i, j, 0)),
            pl.BlockSpec((None, 8, d), lambda i, j: (i, 0, 0)),
            pl.BlockSpec(p["w_in"].shape, const),
            pl.BlockSpec(p["v_gain"].shape, const),
            pl.BlockSpec(p["gmean"].shape, const),
            pl.BlockSpec(p["w_sp"].shape, const),
            pl.BlockSpec(p["b_sp"].shape, const),
            pl.BlockSpec(p["cmat"].shape, const),
            pl.BlockSpec(p["smat"].shape, const),
        ],
        out_specs=[pl.BlockSpec((None, tm, aw), lambda i, j: (i, j, 0)), f_spec, f_spec],
        out_shape=[jax.ShapeDtypeStruct((bsz, length, aw), BF16), f_shape, f_shape],
        compiler_params=_cparams("parallel", "parallel"),
    )(h, mod, p["w_in"], p["v_gain"], p["gmean"], p["w_sp"], p["b_sp"], p["cmat"], p["smat"])
    return a_out, _seq_dft(fc, fs)


def _seq_dft_direct(fc, fs):
    bsz, length, width = fc.shape
    cl, sl = _dft_tables(length, length ** -0.5)
    tk = _row_tile(length, 512)
    return pl.pallas_call(
        _seq_dft_kernel,
        grid=(length // tk, bsz),
        in_specs=[
            pl.BlockSpec((tk, length), lambda k, b: (k, 0)),
            pl.BlockSpec((tk, length), lambda k, b: (k, 0)),
            pl.BlockSpec((None, length, width), lambda k, b: (b, 0, 0)),
            pl.BlockSpec((None, length, width), lambda k, b: (b, 0, 0)),
        ],
        out_specs=pl.BlockSpec((None, tk, width), lambda k, b: (b, k, 0)),
        out_shape=jax.ShapeDtypeStruct((bsz, length, width), BF16),
        compiler_params=_cparams("parallel", "parallel"),
    )(cl, sl, fc, fs)


def _fft_rows_kernel(fc_ref, fs_ref, ma_ref, tc_ref, ts_ref, yr_ref, yi_ref):
    n1 = fc_ref.shape[0]
    y = _dot(ma_ref[...], jnp.concatenate([fc_ref[...], fs_ref[...]], axis=0))
    yr, yi = y[:n1, :], y[n1:, :]
    c, s = tc_ref[...], ts_ref[...]
    yr_ref[...] = (yr * c + yi * s).astype(BF16)
    yi_ref[...] = (yi * c - yr * s).astype(BF16)


def _fft_cols_kernel(yr_ref, yi_ref, mb_ref, o_ref):
    width = yr_ref.shape[2]
    for j in range(yr_ref.shape[0]):
        y = jnp.concatenate([yr_ref[j], yi_ref[j]], axis=0)
        o_ref[:, j * width:(j + 1) * width] = _dot(mb_ref[...], y).astype(BF16)


def _seq_dft(fc, fs):
    bsz, length, width = fc.shape
    n2 = GRID_W
    n1 = length // n2
    if n1 < 16 or length % n2:
        return _seq_dft_direct(fc, fs)
    ncol = n2 * width

    def angles(a, b, period):
        prod = (jnp.arange(a, dtype=jnp.int32)[:, None] * jnp.arange(b, dtype=jnp.int32)[None, :]) % period
        return prod.astype(F32) * (2.0 * math.pi / period)

    th = angles(n1, n1, n1)
    c1, s1 = jnp.cos(th) * n1 ** -0.5, jnp.sin(th) * n1 ** -0.5
    ma = jnp.concatenate([jnp.concatenate([c1, -s1], axis=1),
                          jnp.concatenate([-s1, -c1], axis=1)], axis=0).astype(BF16)
    ph = angles(n1, n2, length)
    tc = jnp.repeat(jnp.cos(ph), width, axis=1)
    ts = jnp.repeat(jnp.sin(ph), width, axis=1)
    ps = angles(n2, n2, n2)
    mb = (jnp.concatenate([jnp.cos(ps), jnp.sin(ps)], axis=1) * n2 ** -0.5).astype(BF16)

    cols = _divisor_tile(ncol, 4096)
    row_blk = lambda b, j: (b, 0, j)
    yr, yi = pl.pallas_call(
        _fft_rows_kernel,
        grid=(bsz, ncol // cols),
        in_specs=[
            pl.BlockSpec((None, n1, cols), row_blk),
            pl.BlockSpec((None, n1, cols), row_blk),
            pl.BlockSpec((2 * n1, 2 * n1), lambda b, j: (0, 0)),
            pl.BlockSpec((n1, cols), lambda b, j: (0, j)),
            pl.BlockSpec((n1, cols), lambda b, j: (0, j)),
        ],
        out_specs=[pl.BlockSpec((None, n1, cols), row_blk)] * 2,
        out_shape=[jax.ShapeDtypeStruct((bsz, n1, ncol), BF16)] * 2,
        compiler_params=_cparams("parallel", "parallel"),
    )(fc.reshape(bsz, n1, ncol), fs.reshape(bsz, n1, ncol), ma, tc, ts)

    kb = _divisor_tile(n1, 8, 1)
    out = pl.pallas_call(
        _fft_cols_kernel,
        grid=(bsz, n1 // kb),
        in_specs=[
            pl.BlockSpec((None, kb, n2, width), lambda b, k: (b, k, 0, 0)),
            pl.BlockSpec((None, kb, n2, width), lambda b, k: (b, k, 0, 0)),
            pl.BlockSpec((n2, 2 * n2), lambda b, k: (0, 0)),
        ],
        out_specs=pl.BlockSpec((None, n2, kb * width), lambda b, k: (b, 0, k)),
        out_shape=jax.ShapeDtypeStruct((bsz, n2, n1 * width), BF16),
        compiler_params=_cparams("parallel", "parallel"),
    )(yr.reshape(bsz, n1, n2, width), yi.reshape(bsz, n1, n2, width), mb)
    return out.reshape(bsz, length, width)


def _swiglu(x, wg_ref, wu_ref, wd_ref, chunk):
    ff = wg_ref.shape[1]
    y = jnp.zeros((x.shape[0], wd_ref.shape[1]), F32)
    for f in range(0, ff, chunk):
        cols = slice(f, min(f + chunk, ff))
        a = jax.nn.silu(_dot(x, wg_ref[:, cols])) * _dot(x, wu_ref[:, cols])
        y = y + _dot(a.astype(BF16), wd_ref[cols, :])
    return y


def _dense_ffn_kernel(h_ref, mod_ref, wg_ref, wu_ref, wd_ref, o_ref):
    h = h_ref[...]
    n = _modulate(h, mod_ref[3:4, :], mod_ref[4:5, :]).astype(BF16)
    o_ref[...] = h + mod_ref[5:6, :] * _swiglu(n, wg_ref, wu_ref, wd_ref, DENSE_FF_CHUNK)


def _divisor_tile(n, target, unit=LANES):
    best = None
    for t in range(unit, min(n, target) + 1, unit):
        if n % t == 0:
            best = t
    assert best is not None
    return best


def _dense_ffn(h, mod, wg, wu, wd):
    bsz, length, d = h.shape
    ff = wg.shape[1]
    tm = _row_tile(length, 512)
    tok = pl.BlockSpec((None, tm, d), lambda i, j: (i, j, 0))
    resident = pl.Buffered(1)
    return pl.pallas_call(
        _dense_ffn_kernel,
        grid=(bsz, length // tm),
        in_specs=[
            tok,
            pl.BlockSpec((None, 8, d), lambda i, j: (i, 0, 0)),
            pl.BlockSpec((d, ff), lambda i, j: (0, 0), pipeline_mode=resident),
            pl.BlockSpec((d, ff), lambda i, j: (0, 0), pipeline_mode=resident),
            pl.BlockSpec((ff, d), lambda i, j: (0, 0), pipeline_mode=resident),
        ],
        out_specs=tok,
        out_shape=jax.ShapeDtypeStruct((bsz, length, d), F32),
        compiler_params=_cparams("parallel", "parallel"),
    )(h, mod, wg, wu, wd)


def _pair_swap(x):
    width = x.shape[-1]
    lane = lax.broadcasted_iota(jnp.int32, x.shape, x.ndim - 1)
    nxt = pltpu.roll(x, width - 1, x.ndim - 1)
    prv = pltpu.roll(x, 1, x.ndim - 1)
    return jnp.where(lane % 2 == 0, nxt, prv)


def _rope(x, cos, sin):
    reps = x.shape[1] // LANES
    cos = jnp.concatenate([cos] * reps, axis=1) if reps > 1 else cos
    sin = jnp.concatenate([sin] * reps, axis=1) if reps > 1 else sin
    return x * cos + _pair_swap(x) * sin


def _cd_in_kernel(h_ref, mod_ref, w_ref, gm_ref, qg_ref, kg_ref, cos_ref, sin_ref,
                  cq_ref, ck_ref, cv_ref, dq_ref, dk_ref, dv_ref, *, rope):
    n = _modulate(h_ref[...], mod_ref[0:1, :], mod_ref[1:2, :])
    z = _dot(n.astype(BF16), w_ref[...])
    qw, kw, dqw, dw = cq_ref.shape[1], ck_ref.shape[1], dq_ref.shape[1], dk_ref.shape[1]
    o1, o2, o3, o4, o5 = qw, qw + kw, qw + 2 * kw, qw + 2 * kw + dqw, qw + 2 * kw + dqw + dw
    scale = HEAD_DIM ** -0.5

    def head_norm(t, gain):
        ms = _dot((t * t).astype(BF16), gm_ref[:t.shape[1], :t.shape[1]])
        return t * lax.rsqrt(ms + EPS) * gain

    cq = head_norm(z[:, :o1], qg_ref[...])
    ck = head_norm(z[:, o1:o2], kg_ref[...])
    if rope:
        cq = _rope(cq, cos_ref[...], sin_ref[...])
        ck = _rope(ck, cos_ref[...], sin_ref[...])
    scale = scale * LOG2E
    cq_ref[...] = (cq * scale).astype(BF16)
    ck_ref[...] = ck.astype(BF16)
    cv_ref[...] = z[:, o2:o3].astype(BF16)
    dq_ref[...] = (z[:, o3:o4] * scale).astype(BF16)
    dk_ref[...] = z[:, o4:o5].astype(BF16)
    dv_ref[...] = z[:, o5:].astype(BF16)


def _project_cd(h, mod, p, rope):
    bsz, length, d = h.shape
    qw, kw, dw = C_HEADS * HEAD_DIM, C_KV_HEADS * HEAD_DIM, D_HEADS * HEAD_DIM
    tm = _row_tile(length, 512)
    const = lambda i, j: (0, 0)
    tok = lambda w: pl.BlockSpec((None, tm, w), lambda i, j: (i, j, 0))
    widths = (qw, kw, kw, dw, dw, dw)
    return pl.pallas_call(
        functools.partial(_cd_in_kernel, rope=rope),
        grid=(bsz, length // tm),
        in_specs=[
            tok(d),
            pl.BlockSpec((None, 8, d), lambda i, j: (i, 0, 0)),
            pl.BlockSpec(p["w_in"].shape, const),
            pl.BlockSpec(p["gmean"].shape, const),
            pl.BlockSpec(p["q_gain"].shape, const),
            pl.BlockSpec(p["k_gain"].shape, const),
            pl.BlockSpec((tm, LANES), lambda i, j: (j, 0)),
            pl.BlockSpec((tm, LANES), lambda i, j: (j, 0)),
        ],
        out_specs=[tok(w) for w in widths],
        out_shape=[jax.ShapeDtypeStruct((bsz, length, w), BF16) for w in widths],
        compiler_params=_cparams("parallel", "parallel"),
    )(h, mod, p["w_in"], p["gmean"], p["q_gain"], p["k_gain"], p["cos"], p["sin"])


def _gqa_kernel(q_ref, kt_ref, v_ref, o_ref):
    tq = q_ref.shape[0]
    sub = tq // GQA_SPLIT
    lane = lax.broadcasted_iota(jnp.int32, (sub, LANES), 1)

    def attend(h, rows):
        block = q_ref[rows, (h % C_GROUP) * LANES:(h % C_GROUP + 1) * LANES]
        own = (lane >= HEAD_DIM) if h // C_GROUP else (lane < HEAD_DIM)
        s = _dot(jnp.where(own, block, jnp.zeros_like(block)), kt_ref[...])
        p = jnp.exp2(s - jnp.max(s, axis=-1, keepdims=True)).astype(BF16)
        o = _dot(p, v_ref[...])
        return o[:, :LANES] / o[:, LANES:LANES + 1]

    for pair in range(C_HEADS // 2):
        for part in range(GQA_SPLIT):
            rows = slice(part * sub, (part + 1) * sub)
            a, b = attend(2 * pair, rows), attend(2 * pair + 1, rows)
            if (2 * pair) // C_GROUP == 0:
                blk = jnp.where(lane < HEAD_DIM, a, pltpu.roll(b, HEAD_DIM, 1))
            else:
                blk = jnp.where(lane < HEAD_DIM, pltpu.roll(a, HEAD_DIM, 1), b)
            o_ref[rows, pair * LANES:(pair + 1) * LANES] = blk.astype(BF16)


def _gqa(q_paired, k_all, v_all):
    bsz, length, qw = q_paired.shape
    lk = k_all.shape[1]
    assert C_KV_HEADS * HEAD_DIM == LANES and C_KV_HEADS == 2
    kt = jnp.swapaxes(k_all, 1, 2)
    v_ext = jnp.concatenate([v_all, jnp.ones_like(v_all)], axis=2)
    tq = _row_tile(length, 512)
    ow = C_HEADS * HEAD_DIM
    return pl.pallas_call(
        _gqa_kernel,
        grid=(bsz, length // tq),
        in_specs=[
            pl.BlockSpec((None, tq, qw), lambda i, j: (i, j, 0)),
            pl.BlockSpec((None, LANES, lk), lambda i, j: (i, 0, 0)),
            pl.BlockSpec((None, lk, 2 * LANES), lambda i, j: (i, 0, 0)),
        ],
        out_specs=pl.BlockSpec((None, tq, ow), lambda i, j: (i, j, 0)),
        out_shape=jax.ShapeDtypeStruct((bsz, length, ow), BF16),
        compiler_params=_cparams("parallel", "parallel"),
    )(q_paired, kt, v_ext)


def _na_block_start(qb, rows):
    return jnp.clip(qb * NA_QROWS - NA_WIN_H // 2, 0, rows - NA_KROWS)


def _na_bias_tables(rel_bias, rows):
    nblk = rows // NA_QROWS
    n_dr, n_dc = 2 * NA_WIN_H - 1, 2 * NA_WIN_W - 1
    i = np.arange(NA_QROWS)[:, None]
    a = np.arange(NA_KROWS)[None, :]
    j = np.arange(GRID_W)[:, None]
    kc = np.arange(GRID_W)[None, :]
    col_start = np.clip(j - NA_WIN_W // 2, 0, GRID_W - NA_WIN_W)
    valid_col = (kc >= col_start) & (kc < col_start + NA_WIN_W)
    dc = np.clip(kc - j + (NA_WIN_W - 1), 0, n_dc - 1)
    onehot_c = (dc[:, :, None] == np.arange(n_dc)).astype(np.float32)
    onehot_r, valid = [], []
    for qb in (0, 1, nblk - 1):
        r = qb * NA_QROWS + i
        r0 = np.clip(r - NA_WIN_H // 2, 0, rows - NA_WIN_H)
        kr = int(np.clip(qb * NA_QROWS - NA_WIN_H // 2, 0, rows - NA_KROWS)) + a
        valid_row = (kr >= r0) & (kr < r0 + NA_WIN_H)
        dr = np.clip(kr - r + (NA_WIN_H - 1), 0, n_dr - 1)
        onehot_r.append((dr[:, :, None] == np.arange(n_dr)).astype(np.float32))
        valid.append(valid_row[:, None, :, None] & valid_col[None, :, None, :])
    onehot_r = jnp.asarray(np.stack(onehot_r))
    valid = np.stack(valid)
    hp = lax.Precision.HIGHEST
    by_row = jnp.einsum("hrc,ziar->zhiac", rel_bias.astype(F32), onehot_r, precision=hp)
    table = jnp.einsum("zhiac,jkc->zhijak", by_row, jnp.asarray(onehot_c), precision=hp)
    table = jnp.where(valid[:, None], table * LOG2E, MASK_VALUE)
    return table.reshape(3 * rel_bias.shape[0], NA_QROWS * GRID_W, NA_KROWS * GRID_W)


def _na_kernel(q_ref, k_ref, v_ref, kc_ref, vc_ref, bias_ref, o_ref, *, rows):
    qb = pl.program_id(1)
    nk = NA_KROWS * GRID_W
    start = pl.multiple_of(_na_block_start(qb, rows) * GRID_W, GRID_W)
    k_nb = k_ref[pl.ds(start, nk), :]
    v_nb = v_ref[pl.ds(start, nk), :]
    nc = kc_ref.shape[0]
    sub = q_ref.shape[0] // NA_SPLIT
    lane = lax.broadcasted_iota(jnp.int32, (sub, LANES), 1)

    def attend(h, rows, keys, values):
        block = q_ref[rows, (h // 2) * LANES:(h // 2 + 1) * LANES]
        own = (lane >= HEAD_DIM) if h % 2 else (lane < HEAD_DIM)
        q = jnp.where(own, block, jnp.zeros_like(block))
        s = jnp.concatenate([_dot_nt(q, keys[0]) + bias_ref[h, rows, :], _dot_nt(q, keys[1])], axis=1)
        p = jnp.exp2(s - jnp.max(s, axis=-1, keepdims=True)).astype(BF16)
        o = _dot(p[:, :nk], values[0]) + _dot(p[:, nk:], values[1])
        return o[:, :LANES] / o[:, LANES:LANES + 1]

    for pair in range(D_HEADS // 2):
        cols = slice(pair * LANES, (pair + 1) * LANES)
        keys = (k_nb[:, cols], kc_ref[:, cols])
        values = (jnp.concatenate([v_nb[:, cols], jnp.ones((nk, LANES), BF16)], axis=1),
                  jnp.concatenate([vc_ref[:, cols], jnp.ones((nc, LANES), BF16)], axis=1))
        for part in range(NA_SPLIT):
            rows = slice(part * sub, (part + 1) * sub)
            even, odd = attend(2 * pair, rows, keys, values), attend(2 * pair + 1, rows, keys, values)
            o_ref[rows, cols] = jnp.where(lane < HEAD_DIM, even, odd).astype(BF16)


def _neighbourhood(dq, dk, dv, dk_c, dv_c, bias):
    bsz, length, w = dk.shape
    rows = length // GRID_W
    assert rows % NA_QROWS == 0 and rows >= NA_KROWS
    nblk = rows // NA_QROWS
    tq = NA_QROWS * GRID_W
    cl = dk_c.shape[1]

    def bias_class(i, j):
        cls = jnp.where(j == 0, 0, jnp.where(j == nblk - 1, 2, 1))
        return (cls, 0, 0)

    return pl.pallas_call(
        functools.partial(_na_kernel, rows=rows),
        grid=(bsz, nblk),
        in_specs=[
            pl.BlockSpec((None, tq, dq.shape[2]), lambda i, j: (i, j, 0)),
            pl.BlockSpec((None, length, w), lambda i, j: (i, 0, 0)),
            pl.BlockSpec((None, length, w), lambda i, j: (i, 0, 0)),
            pl.BlockSpec((None, cl, w), lambda i, j: (i, 0, 0)),
            pl.BlockSpec((None, cl, w), lambda i, j: (i, 0, 0)),
            pl.BlockSpec((D_HEADS, tq, NA_KROWS * GRID_W), bias_class),
        ],
        out_specs=pl.BlockSpec((None, tq, w), lambda i, j: (i, j, 0)),
        out_shape=jax.ShapeDtypeStruct((bsz, length, w), BF16),
        compiler_params=_cparams("parallel", "arbitrary"),
    )(dq, dk, dv, dk_c, dv_c, bias)


def _router_kernel(h_ref, mod_ref, wr_ref, n_ref, sel_ref, cnt_ref):
    n = _modulate(h_ref[...], mod_ref[3:4, :], mod_ref[4:5, :])
    n_hi = n.astype(BF16)
    n_lo = (n - n_hi.astype(F32)).astype(BF16)
    w = wr_ref[...]
    w_hi = w.astype(BF16)
    w_lo = (w - w_hi.astype(F32)).astype(BF16)
    logits = _dot(n_hi, w_hi) + (_dot(n_lo, w_hi) + _dot(n_hi, w_lo))
    lane = lax.broadcasted_iota(jnp.int32, logits.shape, 1)
    logits = jnp.where(lane < N_EXPERTS, logits, -jnp.inf)
    m1 = jnp.max(logits, axis=-1, keepdims=True)
    i1 = jnp.min(jnp.where(logits == m1, lane, LANES), axis=-1, keepdims=True)
    rest = jnp.where(lane == i1, -jnp.inf, logits)
    m2 = jnp.max(rest, axis=-1, keepdims=True)
    i2 = jnp.min(jnp.where(rest == m2, lane, LANES), axis=-1, keepdims=True)
    e2 = jnp.exp(m2 - m1)
    w1 = 1.0 / (1.0 + e2)
    w2 = e2 / (1.0 + e2)

    ntok = logits.shape[0]
    chosen = jnp.where(lane == i1, 1.0, jnp.where(lane == i2, 1.0, 0.0))
    earlier = (lax.broadcasted_iota(jnp.int32, (ntok, ntok), 1)
               < lax.broadcasted_iota(jnp.int32, (ntok, ntok), 0))
    before = _dot(jnp.where(earlier, 1.0, 0.0).astype(BF16), chosen.astype(BF16))
    count = jnp.sum(chosen, axis=0, keepdims=True).astype(jnp.int32)
    npiece = lax.shift_right_logical(count + (MOE_PIECE - 1), jnp.full_like(count, MOE_PIECE.bit_length() - 1))
    lower = (lax.broadcasted_iota(jnp.int32, (LANES, LANES), 0)
             < lax.broadcasted_iota(jnp.int32, (LANES, LANES), 1))
    npiece_rows = jnp.broadcast_to(npiece.astype(F32), (8, LANES)).astype(BF16)
    piece_off = _dot(npiece_rows, jnp.where(lower, 1.0, 0.0).astype(BF16))[0:1, :]
    row = piece_off * float(MOE_PIECE) + before
    pos1 = jnp.sum(jnp.where(lane == i1, row, 0.0), axis=-1, keepdims=True)
    pos2 = jnp.sum(jnp.where(lane == i2, row, 0.0), axis=-1, keepdims=True)

    sel = jnp.where(lane == 0, i1.astype(F32), jnp.where(lane == 1, i2.astype(F32),
                    jnp.where(lane == 2, w1, jnp.where(lane == 3, w2,
                              jnp.where(lane == 4, pos1, jnp.where(lane == 5, pos2, 0.0))))))
    sel_ref[...] = sel
    cnt_ref[...] = jnp.broadcast_to(npiece.astype(F32), cnt_ref.shape)
    n_ref[...] = n_hi


def _router(h, mod, w_router_padded):
    bsz, length, d = h.shape
    tm = _row_tile(length, MOE_TOKENS)
    tiles = length // tm
    tok = lambda w: pl.BlockSpec((None, tm, w), lambda i, j: (i, j, 0))
    return pl.pallas_call(
        _router_kernel,
        grid=(bsz, tiles),
        in_specs=[
            tok(d),
            pl.BlockSpec((None, 8, d), lambda i, j: (i, 0, 0)),
            pl.BlockSpec((d, LANES), lambda i, j: (0, 0)),
        ],
        out_specs=[tok(d), tok(LANES), pl.BlockSpec((None, None, 8, LANES), lambda i, j: (i, j, 0, 0))],
        out_shape=[
            jax.ShapeDtypeStruct((bsz, length, d), BF16),
            jax.ShapeDtypeStruct((bsz, length, LANES), F32),
            jax.ShapeDtypeStruct((bsz, tiles, 8, LANES), F32),
        ],
        compiler_params=_cparams("parallel", "parallel"),
    )(h, mod, w_router_padded)


def _moe_dispatch_kernel(pos_ref, w_ref, n_ref, xs_ref, gs_ref):
    nrow = xs_ref.shape[0]
    base = pl.program_id(1) * nrow
    row = base + lax.broadcasted_iota(jnp.int32, (nrow, 1), 0)
    eq0 = pos_ref[0:1, :] == row
    eq1 = pos_ref[1:2, :] == row
    perm = jnp.where(eq0, 1.0, jnp.where(eq1, 1.0, 0.0)).astype(BF16)
    xs_ref[...] = _dot(perm, n_ref[...]).astype(BF16)
    gate = jnp.where(eq0, w_ref[0:1, :], jnp.where(eq1, w_ref[1:2, :], 0.0))
    gs_ref[...] = jnp.sum(gate, axis=-1, keepdims=True)


def _moe_expert_kernel(piece_ref, exp_ref, nused_ref, *refs):
    xs_refs, gs_refs = refs[:MOE_GROUP], refs[MOE_GROUP:2 * MOE_GROUP]
    wg_ref, wu_ref, wd_ref, ys_ref = refs[2 * MOE_GROUP:]
    i = pl.program_id(0)

    @pl.when(i < nused_ref[0])
    def _():
        x = jnp.concatenate([r[...] for r in xs_refs], axis=0)
        gate = jnp.concatenate([r[...] for r in gs_refs], axis=0)
        ys_ref[...] = (_swiglu(x, wg_ref, wu_ref, wd_ref, MOE_FF_CHUNK) * gate).astype(BF16)

    @pl.when(i >= nused_ref[0])
    def _():
        ys_ref[...] = jnp.zeros_like(ys_ref)


def _moe_combine_kernel(slot_ref, pos_ref, h_ref, mod_ref, fg_ref, *refs):
    ys = jnp.concatenate([r[...] for r in refs[:-1]], axis=0)
    o_ref = refs[-1]
    nrow = ys.shape[0]
    sub = pos_ref.shape[0] // MOE_COMBINE_SPLIT
    row = lax.broadcasted_iota(jnp.int32, (sub, nrow), 1)
    for part in range(MOE_COMBINE_SPLIT):
        toks = slice(part * sub, (part + 1) * sub)
        perm = jnp.where(pos_ref[toks, 0:1] == row, 1.0,
                         jnp.where(pos_ref[toks, 1:2] == row, 1.0, 0.0)).astype(BF16)
        y = h_ref[toks, :] + mod_ref[5:6, :] * _dot(perm, ys)
        ms = jnp.mean(y * y, axis=-1, keepdims=True)
        o_ref[toks, :] = y * lax.rsqrt(ms + EPS) * fg_ref[...]


def _moe_plan(sel, npiece, n_tiles, tile_tokens, rows_per_tile):
    pieces_per_tile = rows_per_tile // MOE_PIECE
    weights = sel[:, TOP_K:2 * TOP_K].reshape(n_tiles, tile_tokens, TOP_K)
    pos = sel[:, 2 * TOP_K:3 * TOP_K].astype(jnp.int32).reshape(n_tiles, tile_tokens, TOP_K)
    piece_end = jnp.cumsum(npiece.astype(jnp.int32), axis=1)

    b = jnp.arange(pieces_per_tile, dtype=jnp.int32)
    piece_expert = jnp.sum((piece_end[:, None, :] <= b[None, :, None]).astype(jnp.int32), axis=-1)
    flat_expert = piece_expert.reshape(-1)
    n_pieces = flat_expert.shape[0]
    n_slots = n_pieces + N_EXPERTS * MOE_GROUP
    classes = jnp.arange(N_EXPERTS + 1, dtype=jnp.int32)
    cls_onehot = (flat_expert[:, None] == classes).astype(jnp.int32)
    cls_csum = jnp.cumsum(cls_onehot, axis=0)
    piece_rank = jnp.sum((cls_csum - cls_onehot) * cls_onehot, axis=-1)
    n_cls = cls_csum[-1]
    groups = (n_cls[:N_EXPERTS] + MOE_GROUP - 1) // MOE_GROUP
    start = (jnp.cumsum(groups) - groups) * MOE_GROUP
    n_used_groups = jnp.sum(groups)

    slot = jnp.arange(n_slots, dtype=jnp.int32)
    slot_expert = jnp.sum((start[None, :] <= slot[:, None]).astype(jnp.int32), axis=-1) - 1
    slot_taken = (slot - start[slot_expert]) < n_cls[slot_expert]
    free_slots = jnp.argsort(slot_taken.astype(jnp.int32) * n_slots + slot).astype(jnp.int32)
    is_used = flat_expert < N_EXPERTS
    safe_expert = jnp.minimum(flat_expert, N_EXPERTS - 1)
    piece_slot = jnp.where(is_used, start[safe_expert] + piece_rank, free_slots[piece_rank])
    first_unused = jnp.argmax(jnp.logical_not(is_used)).astype(jnp.int32)
    piece_ids = jnp.arange(n_pieces, dtype=jnp.int32)
    slot_piece = jnp.full((n_slots,), first_unused, jnp.int32).at[piece_slot].set(piece_ids, unique_indices=True)
    step_expert = slot_expert[::MOE_GROUP]
    return pos, weights, slot_piece, piece_slot.astype(jnp.int32), step_expert, n_used_groups.reshape(1)


def _moe_final(n2, sel, npiece, h, mod, final_gain, wg, wu, wd):
    bsz, length, d = h.shape
    n_exp, _, ff = wg.shape
    tokens = bsz * length
    tt = _row_tile(length, MOE_TOKENS)
    n_tiles = tokens // tt
    rows = TOP_K * tt + N_EXPERTS * MOE_PIECE
    drows = _divisor_tile(rows, MOE_DISPATCH_ROWS, MOE_PIECE)
    ctok = _row_tile(tt, MOE_COMBINE_TOKENS)
    pos, weights, slot_piece, piece_slot, step_expert, n_used = _moe_plan(
        sel.reshape(tokens, LANES), npiece.reshape(n_tiles, 8, LANES)[:, 0, :N_EXPERTS], n_tiles, tt, rows)

    pos_rows = jnp.swapaxes(pos, 1, 2)
    w_rows = jnp.swapaxes(weights, 1, 2)
    xs, gs = pl.pallas_call(
        _moe_dispatch_kernel,
        grid=(n_tiles, rows // drows),
        in_specs=[
            pl.BlockSpec((None, TOP_K, tt), lambda t, r: (t, 0, 0)),
            pl.BlockSpec((None, TOP_K, tt), lambda t, r: (t, 0, 0)),
            pl.BlockSpec((tt, d), lambda t, r: (t, 0)),
        ],
        out_specs=[
            pl.BlockSpec((None, drows, d), lambda t, r: (t, r, 0)),
            pl.BlockSpec((None, drows, 1), lambda t, r: (t, r, 0)),
        ],
        out_shape=[
            jax.ShapeDtypeStruct((n_tiles, rows, d), BF16),
            jax.ShapeDtypeStruct((n_tiles, rows, 1), F32),
        ],
        compiler_params=_cparams("parallel", "parallel"),
    )(pos_rows, w_rows, n2.reshape(tokens, d))

    n_steps = slot_piece.shape[0] // MOE_GROUP
    resident = pl.Buffered(1)

    def piece_spec(j, width):
        return pl.BlockSpec((MOE_PIECE, width), lambda i, sp, ex, nu: (sp[i * MOE_GROUP + j], 0))

    def weight_spec(shape):
        return pl.BlockSpec((None,) + shape, lambda i, sp, ex, nu: (ex[i], 0, 0), pipeline_mode=resident)

    xs_flat, gs_flat = xs.reshape(n_tiles * rows, d), gs.reshape(n_tiles * rows, 1)
    step_rows = MOE_GROUP * MOE_PIECE
    ys = pl.pallas_call(
        _moe_expert_kernel,
        grid_spec=pltpu.PrefetchScalarGridSpec(
            num_scalar_prefetch=3,
            grid=(n_steps,),
            in_specs=([piece_spec(j, d) for j in range(MOE_GROUP)]
                      + [piece_spec(j, 1) for j in range(MOE_GROUP)]
                      + [weight_spec((d, ff)), weight_spec((d, ff)), weight_spec((ff, d))]),
            out_specs=pl.BlockSpec((step_rows, d), lambda i, sp, ex, nu: (i, 0)),
        ),
        out_shape=jax.ShapeDtypeStruct((n_steps * step_rows, d), BF16),
        compiler_params=_cparams("arbitrary"),
    )(slot_piece, step_expert, n_used, *([xs_flat] * MOE_GROUP), *([gs_flat] * MOE_GROUP), wg, wu, wd)

    tiles_per_seq = length // tt
    pieces_per_tile = rows // MOE_PIECE

    def tile_piece_spec(k):
        return pl.BlockSpec((MOE_PIECE, d), lambda t, c, slot: (slot[t * pieces_per_tile + k], 0))

    out = pl.pallas_call(
        _moe_combine_kernel,
        grid_spec=pltpu.PrefetchScalarGridSpec(
            num_scalar_prefetch=1,
            grid=(n_tiles, tt // ctok),
            in_specs=[
                pl.BlockSpec((None, ctok, TOP_K), lambda t, c, slot: (t, c, 0)),
                pl.BlockSpec((None, ctok, d), lambda t, c, slot: (t, c, 0)),
                pl.BlockSpec((None, 8, d), lambda t, c, slot: (t // tiles_per_seq, 0, 0)),
                pl.BlockSpec((1, d), lambda t, c, slot: (0, 0)),
            ] + [tile_piece_spec(k) for k in range(pieces_per_tile)],
            out_specs=pl.BlockSpec((None, ctok, d), lambda t, c, slot: (t, c, 0)),
        ),
        out_shape=jax.ShapeDtypeStruct((n_tiles, tt, d), F32),
        compiler_params=_cparams("parallel", "parallel"),
    )(piece_slot, pos, h.reshape(n_tiles, tt, d), mod, final_gain, *([ys] * pieces_per_tile))
    return out.reshape(bsz, length, d)


def _group_mean_matrix(width, group):
    return jnp.asarray(np.kron(np.eye(width // group), np.full((group, group), 1.0 / group)), BF16)


def _rope_tables(length):
    t = jnp.arange(length, dtype=jnp.int32)
    row = (t // GRID_W).astype(F32)
    col = (t % GRID_W).astype(F32)
    n_axis = HEAD_DIM // 4
    inv_freq = ROPE_THETA ** (-jnp.arange(n_axis, dtype=F32) / n_axis)
    ang = jnp.concatenate([row[:, None] * inv_freq, col[:, None] * inv_freq], axis=-1)
    cos = jnp.repeat(jnp.cos(ang), 2, axis=-1)
    sin = jnp.repeat(jnp.sin(ang), 2, axis=-1)
    sign = jnp.tile(jnp.asarray([-1.0, 1.0], F32), HEAD_DIM // 2)
    reps = LANES // HEAD_DIM
    return jnp.tile(cos, (1, reps)), jnp.tile(sin * sign, (1, reps))


def kernel(x, c, ctx, c_ctx, w_ada, b_ada, w_in_ab, v_gain, w_spatial, b_spatial, w_out_ab,
           w_gate_dense, w_up_dense, w_down_dense, w_in_cd, q_gain, k_gain, rel_bias, w_out_cd,
           w_router, w_gate_moe, w_up_moe, w_down_moe, final_gain):
    bsz, length, d = x.shape
    depth = w_ada.shape[0]
    assert depth == 2 and bsz <= 8
    aw = v_gain.shape[1]
    group_dim = aw // A_GROUPS
    bgd = (w_in_ab.shape[2] - 2 * aw) // B_GROUPS

    cond = jnp.zeros((16, d), F32).at[:bsz].set(c).at[8].set(c_ctx)
    ada = _adaln(cond, w_ada, b_ada).reshape(depth, 16, 6, d)
    pad = jnp.zeros((depth, 16, 2, d), F32)
    ada = jnp.concatenate([ada, pad], axis=2)
    mod_lat = [ada[i, :bsz] for i in range(depth)]
    mod_ctx = [jnp.broadcast_to(ada[i, 8], (bsz, 8, d)) for i in range(depth)]

    cmat, smat = _dft_tables(bgd, bgd ** -0.5)
    p_ab = dict(
        w_in=w_in_ab[0].astype(BF16),
        v_gain=v_gain[0].reshape(1, aw),
        gmean=_group_mean_matrix(aw, group_dim),
        w_sp=w_spatial[0].transpose(1, 0, 2).reshape(CHUNK, A_GROUPS * CHUNK).astype(BF16),
        b_sp=jnp.repeat(b_spatial[0].T, group_dim, axis=1),
        cmat=cmat, smat=smat,
        w_out=w_out_ab[0].astype(BF16),
    )
    wg0, wu0, wd0 = (w.astype(BF16) for w in (w_gate_dense[0], w_up_dense[0], w_down_dense[0]))
    h = _proj_residual(*_mix_ab(x, mod_lat[0], p_ab), p_ab["w_out"], x, mod_lat[0])
    hc = _proj_residual(*_mix_ab(ctx, mod_ctx[0], p_ab), p_ab["w_out"], ctx, mod_ctx[0])
    h = _dense_ffn(h, mod_lat[0], wg0, wu0, wd0)
    hc = _dense_ffn(hc, mod_ctx[0], wg0, wu0, wd0)

    cos, sin = _rope_tables(length)
    qw = C_HEADS * HEAD_DIM
    kw = C_KV_HEADS * HEAD_DIM
    w_cd = w_in_cd[0]
    w_q = w_cd[:, :qw].reshape(d, C_KV_HEADS, C_GROUP, HEAD_DIM).transpose(0, 2, 1, 3).reshape(d, qw)
    p_cd = dict(
        w_in=jnp.concatenate([w_q, w_cd[:, qw:]], axis=1).astype(BF16),
        gmean=_group_mean_matrix(qw, HEAD_DIM),
        q_gain=jnp.tile(q_gain[0], C_HEADS).reshape(1, qw),
        k_gain=jnp.tile(k_gain[0], C_KV_HEADS).reshape(1, kw),
        cos=cos, sin=sin,
    )
    cq, ck, cv, dq, dk, dv = _project_cd(h, mod_lat[1], p_cd, rope=True)
    p_cd_ctx = dict(p_cd, cos=cos[:hc.shape[1]], sin=sin[:hc.shape[1]])
    _, ck_c, cv_c, _, dk_c, dv_c = _project_cd(hc, mod_ctx[1], p_cd_ctx, rope=False)
    o_c = _gqa(cq, jnp.concatenate([ck_c, ck], axis=1), jnp.concatenate([cv_c, cv], axis=1))
    o_d = _neighbourhood(dq, dk, dv, dk_c, dv_c, _na_bias_tables(rel_bias[0], length // GRID_W))
    wr = jnp.zeros((d, LANES), F32).at[:, :N_EXPERTS].set(w_router[0])
    h = _proj_residual(o_c, o_d, w_out_cd[0].astype(BF16), h, mod_lat[1])
    n2, sel, npiece = _router(h, mod_lat[1], wr)
    return _moe_final(n2, sel, npiece, h, mod_lat[1], final_gain.reshape(1, d),
                      w_gate_moe[0].astype(BF16), w_up_moe[0].astype(BF16), w_down_moe[0].astype(BF16))
```

```python
import functools
import math

import numpy as np
import jax
import jax.numpy as jnp
from jax import lax
from jax.experimental import pallas as pl
from jax.experimental.pallas import tpu as pltpu

F32 = jnp.float32
BF16 = jnp.bfloat16

GRID_W = 64
EPS = 1e-6
CHUNK = 128
A_GROUPS = 8
B_GROUPS = 4
HEAD_DIM = 64
C_HEADS = 8
C_KV_HEADS = 2
C_GROUP = C_HEADS // C_KV_HEADS
D_HEADS = 8
NA_WIN_H = 8
NA_WIN_W = 16
ROPE_THETA = 10000.0
N_EXPERTS = 8
TOP_K = 2

LANES = 128
GQA_SPLIT = 4
NA_QROWS = 4
NA_SPLIT = 2
NA_KROWS = NA_QROWS + NA_WIN_H
MASK_VALUE = -1e30
LOG2E = 1.4426950408889634
VMEM_LIMIT = 56 * 1024 * 1024

DENSE_FF_CHUNK = 1536
MOE_TOKENS = 1024
MOE_PIECE = 32
MOE_GROUP = 8
MOE_DISPATCH_ROWS = 1280
MOE_COMBINE_TOKENS = 1024
MOE_COMBINE_SPLIT = 4
MOE_FF_CHUNK = 1792


def _cparams(*sem):
    return pltpu.CompilerParams(dimension_semantics=sem, vmem_limit_bytes=VMEM_LIMIT)


def _modulate(h, shift, scale):
    ms = jnp.mean(h * h, axis=-1, keepdims=True)
    return h * lax.rsqrt(ms + EPS) * (1.0 + scale) + shift


def _dot(a, b):
    return jnp.dot(a, b, preferred_element_type=F32)


def _dot_nt(a, b):
    return lax.dot_general(a, b, (((1,), (1,)), ((), ())), preferred_element_type=F32)


def _row_tile(length, target):
    t = min(length, target)
    assert length % t == 0
    return t


def _adaln_kernel(c_ref, w_ref, b_ref, o_ref):
    s = jax.nn.silu(c_ref[...]).astype(BF16)
    o_ref[...] = _dot(s, w_ref[...].astype(BF16)) + b_ref[...]


def _adaln(cond, w_ada, b_ada):
    depth, d, n = w_ada.shape
    r = cond.shape[0]
    tn = 512
    return pl.pallas_call(
        _adaln_kernel,
        grid=(depth, n // tn),
        in_specs=[
            pl.BlockSpec((r, d), lambda i, j: (0, 0)),
            pl.BlockSpec((None, d, tn), lambda i, j: (i, 0, j)),
            pl.BlockSpec((None, 1, tn), lambda i, j: (i, 0, j)),
        ],
        out_specs=pl.BlockSpec((None, r, tn), lambda i, j: (i, 0, j)),
        out_shape=jax.ShapeDtypeStruct((depth, r, n), F32),
        compiler_params=_cparams("parallel", "parallel"),
    )(cond, w_ada, b_ada.reshape(depth, 1, n))


def _ab_in_kernel(h_ref, mod_ref, w_ref, vg_ref, gm_ref, wsp_ref, bsp_ref, cm_ref, sm_ref,
                  a_ref, fc_ref, fs_ref, *, aw):
    tm = h_ref.shape[0]
    n = _modulate(h_ref[...], mod_ref[0:1, :], mod_ref[1:2, :])
    z = _dot(n.astype(BF16), w_ref[...])
    u = jax.nn.gelu(z[:, :aw])
    v = jax.nn.gelu(z[:, aw:2 * aw])
    fb = z[:, 2 * aw:].astype(BF16)
    ms = _dot((v * v).astype(BF16), gm_ref[...])
    vn = v * lax.rsqrt(ms + EPS) * vg_ref[...]
    group_dim = aw // A_GROUPS
    lane_group = lax.broadcasted_iota(jnp.int32, (CHUNK, aw), 1) // group_dim
    for c in range(tm // CHUNK):
        rows = slice(c * CHUNK, (c + 1) * CHUNK)
        vc = vn[rows, :]
        stack = jnp.concatenate(
            [jnp.where(lane_group == g, vc, 0.0).astype(BF16) for g in range(A_GROUPS)], axis=0)
        s = _dot(wsp_ref[...], stack) + bsp_ref[...]
        a_ref[rows, :] = (u[rows, :] * s).astype(BF16)
    bw = fb.shape[1] // B_GROUPS
    for g in range(B_GROUPS):
        cols = slice(g * bw, (g + 1) * bw)
        fc_ref[:, cols] = _dot(fb[:, cols], cm_ref[...]).astype(BF16)
        fs_ref[:, cols] = _dot(fb[:, cols], sm_ref[...]).astype(BF16)


def _seq_dft_kernel(c_ref, s_ref, fc_ref, fs_ref, o_ref):
    o_ref[...] = (_dot(c_ref[...], fc_ref[...]) - _dot(s_ref[...], fs_ref[...])).astype(BF16)


def _proj_residual_kernel(a_ref, b_ref, w_ref, h_ref, mod_ref, o_ref):
    ka = a_ref.shape[1]
    y = _dot(a_ref[...], w_ref[:ka, :]) + _dot(b_ref[...], w_ref[ka:, :])
    o_ref[...] = h_ref[...] + mod_ref[2:3, :] * y


def _proj_residual(a, b, w, h, mod):
    bsz, length, d = h.shape
    tm = _row_tile(length, 1024)
    ka, kb = a.shape[2], b.shape[2]
    return pl.pallas_call(
        _proj_residual_kernel,
        grid=(bsz, length // tm),
        in_specs=[
            pl.BlockSpec((None, tm, ka), lambda i, j: (i, j, 0)),
            pl.BlockSpec((None, tm, kb), lambda i, j: (i, j, 0)),
            pl.BlockSpec((ka + kb, d), lambda i, j: (0, 0)),
            pl.BlockSpec((None, tm, d), lambda i, j: (i, j, 0)),
            pl.BlockSpec((None, 8, d), lambda i, j: (i, 0, 0)),
        ],
        out_specs=pl.BlockSpec((None, tm, d), lambda i, j: (i, j, 0)),
        out_shape=jax.ShapeDtypeStruct((bsz, length, d), F32),
        compiler_params=_cparams("parallel", "parallel"),
    )(a, b, w, h, mod)


def _dft_tables(n, scale):
    k = jnp.arange(n, dtype=jnp.int32)
    ang = ((k[:, None] * k[None, :]) % n).astype(F32) * (2.0 * math.pi / n)
    return (jnp.cos(ang) * scale).astype(BF16), (jnp.sin(ang) * scale).astype(BF16)


def _mix_ab(h, mod, p):
    bsz, length, d = h.shape
    aw = p["v_gain"].shape[1]
    bw_total = p["w_in"].shape[1] - 2 * aw
    const = lambda i, j: (0, 0)
    tm = _row_tile(length, 512)
    f_spec = pl.BlockSpec((None, tm, bw_total), lambda i, j: (i, j, 0))
    f_shape = jax.ShapeDtypeStruct((bsz, length, bw_total), BF16)
    a_out, fc, fs = pl.pallas_call(
        functools.partial(_ab_in_kernel, aw=aw),
        grid=(bsz, length // tm),
        in_specs=[
            pl.BlockSpec((None, tm, d), lambda i, j: (i, j, 0)),
            pl.BlockSpec((None, 8, d), lambda i, j: (i, 0, 0)),
            pl.BlockSpec(p["w_in"].shape, const),
            pl.BlockSpec(p["v_gain"].shape, const),
            pl.BlockSpec(p["gmean"].shape, const),
            pl.BlockSpec(p["w_sp"].shape, const),
            pl.BlockSpec(p["b_sp"].shape, const),
            pl.BlockSpec(p["cmat"].shape, const),
            pl.BlockSpec(p["smat"].shape, const),
        ],
        out_specs=[pl.BlockSpec((None, tm, aw), lambda i, j: (i, j, 0)), f_spec, f_spec],
        out_shape=[jax.ShapeDtypeStruct((bsz, length, aw), BF16), f_shape, f_shape],
        compiler_params=_cparams("parallel", "parallel"),
    )(h, mod, p["w_in"], p["v_gain"], p["gmean"], p["w_sp"], p["b_sp"], p["cmat"], p["smat"])
    return a_out, _seq_dft(fc, fs)


def _seq_dft_direct(fc, fs):
    bsz, length, width = fc.shape
    cl, sl = _dft_tables(length, length ** -0.5)
    tk = _row_tile(length, 512)
    return pl.pallas_call(
        _seq_dft_kernel,
        grid=(length // tk, bsz),
        in_specs=[
            pl.BlockSpec((tk, length), lambda k, b: (k, 0)),
            pl.BlockSpec((tk, length), lambda k, b: (k, 0)),
            pl.BlockSpec((None, length, width), lambda k, b: (b, 0, 0)),
            pl.BlockSpec((None, length, width), lambda k, b: (b, 0, 0)),
        ],
        out_specs=pl.BlockSpec((None, tk, width), lambda k, b: (b, k, 0)),
        out_shape=jax.ShapeDtypeStruct((bsz, length, width), BF16),
        compiler_params=_cparams("parallel", "parallel"),
    )(cl, sl, fc, fs)


def _fft_rows_kernel(fc_ref, fs_ref, ma_ref, tc_ref, ts_ref, yr_ref, yi_ref):
    n1 = fc_ref.shape[0]
    y = _dot(ma_ref[...], jnp.concatenate([fc_ref[...], fs_ref[...]], axis=0))
    yr, yi = y[:n1, :], y[n1:, :]
    c, s = tc_ref[...], ts_ref[...]
    yr_ref[...] = (yr * c + yi * s).astype(BF16)
    yi_ref[...] = (yi * c - yr * s).astype(BF16)


def _fft_cols_kernel(yr_ref, yi_ref, mb_ref, o_ref):
    width = yr_ref.shape[2]
    for j in range(yr_ref.shape[0]):
        y = jnp.concatenate([yr_ref[j], yi_ref[j]], axis=0)
        o_ref[:, j * width:(j + 1) * width] = _dot(mb_ref[...], y).astype(BF16)


def _seq_dft(fc, fs):
    bsz, length, width = fc.shape
    n2 = GRID_W
    n1 = length // n2
    if n1 < 16 or length % n2:
        return _seq_dft_direct(fc, fs)
    ncol = n2 * width

    def angles(a, b, period):
        prod = (jnp.arange(a, dtype=jnp.int32)[:, None] * jnp.arange(b, dtype=jnp.int32)[None, :]) % period
        return prod.astype(F32) * (2.0 * math.pi / period)

    th = angles(n1, n1, n1)
    c1, s1 = jnp.cos(th) * n1 ** -0.5, jnp.sin(th) * n1 ** -0.5
    ma = jnp.concatenate([jnp.concatenate([c1, -s1], axis=1),
                          jnp.concatenate([-s1, -c1], axis=1)], axis=0).astype(BF16)
    ph = angles(n1, n2, length)
    tc = jnp.repeat(jnp.cos(ph), width, axis=1)
    ts = jnp.repeat(jnp.sin(ph), width, axis=1)
    ps = angles(n2, n2, n2)
    mb = (jnp.concatenate([jnp.cos(ps), jnp.sin(ps)], axis=1) * n2 ** -0.5).astype(BF16)

    cols = _divisor_tile(ncol, 4096)
    row_blk = lambda b, j: (b, 0, j)
    yr, yi = pl.pallas_call(
        _fft_rows_kernel,
        grid=(bsz, ncol // cols),
        in_specs=[
            pl.BlockSpec((None, n1, cols), row_blk),
            pl.BlockSpec((None, n1, cols), row_blk),
            pl.BlockSpec((2 * n1, 2 * n1), lambda b, j: (0, 0)),
            pl.BlockSpec((n1, cols), lambda b, j: (0, j)),
            pl.BlockSpec((n1, cols), lambda b, j: (0, j)),
        ],
        out_specs=[pl.BlockSpec((None, n1, cols), row_blk)] * 2,
        out_shape=[jax.ShapeDtypeStruct((bsz, n1, ncol), BF16)] * 2,
        compiler_params=_cparams("parallel", "parallel"),
    )(fc.reshape(bsz, n1, ncol), fs.reshape(bsz, n1, ncol), ma, tc, ts)

    kb = _divisor_tile(n1, 8, 1)
    out = pl.pallas_call(
        _fft_cols_kernel,
        grid=(bsz, n1 // kb),
        in_specs=[
            pl.BlockSpec((None, kb, n2, width), lambda b, k: (b, k, 0, 0)),
            pl.BlockSpec((None, kb, n2, width), lambda b, k: (b, k, 0, 0)),
            pl.BlockSpec((n2, 2 * n2), lambda b, k: (0, 0)),
        ],
        out_specs=pl.BlockSpec((None, n2, kb * width), lambda b, k: (b, 0, k)),
        out_shape=jax.ShapeDtypeStruct((bsz, n2, n1 * width), BF16),
        compiler_params=_cparams("parallel", "parallel"),
    )(yr.reshape(bsz, n1, n2, width), yi.reshape(bsz, n1, n2, width), mb)
    return out.reshape(bsz, length, width)


def _swiglu(x, wg_ref, wu_ref, wd_ref, chunk):
    ff = wg_ref.shape[1]
    y = jnp.zeros((x.shape[0], wd_ref.shape[1]), F32)
    for f in range(0, ff, chunk):
        cols = slice(f, min(f + chunk, ff))
        a = jax.nn.silu(_dot(x, wg_ref[:, cols])) * _dot(x, wu_ref[:, cols])
        y = y + _dot(a.astype(BF16), wd_ref[cols, :])
    return y


def _dense_ffn_kernel(h_ref, mod_ref, wg_ref, wu_ref, wd_ref, o_ref):
    h = h_ref[...]
    n = _modulate(h, mod_ref[3:4, :], mod_ref[4:5, :]).astype(BF16)
    o_ref[...] = h + mod_ref[5:6, :] * _swiglu(n, wg_ref, wu_ref, wd_ref, DENSE_FF_CHUNK)


def _divisor_tile(n, target, unit=LANES):
    best = None
    for t in range(unit, min(n, target) + 1, unit):
        if n % t == 0:
            best = t
    assert best is not None
    return best


def _dense_ffn(h, mod, wg, wu, wd):
    bsz, length, d = h.shape
    ff = wg.shape[1]
    tm = _row_tile(length, 512)
    tok = pl.BlockSpec((None, tm, d), lambda i, j: (i, j, 0))
    resident = pl.Buffered(1)
    return pl.pallas_call(
        _dense_ffn_kernel,
        grid=(bsz, length // tm),
        in_specs=[
            tok,
            pl.BlockSpec((None, 8, d), lambda i, j: (i, 0, 0)),
            pl.BlockSpec((d, ff), lambda i, j: (0, 0), pipeline_mode=resident),
            pl.BlockSpec((d, ff), lambda i, j: (0, 0), pipeline_mode=resident),
            pl.BlockSpec((ff, d), lambda i, j: (0, 0), pipeline_mode=resident),
        ],
        out_specs=tok,
        out_shape=jax.ShapeDtypeStruct((bsz, length, d), F32),
        compiler_params=_cparams("parallel", "parallel"),
    )(h, mod, wg, wu, wd)


def _pair_swap(x):
    width = x.shape[-1]
    lane = lax.broadcasted_iota(jnp.int32, x.shape, x.ndim - 1)
    nxt = pltpu.roll(x, width - 1, x.ndim - 1)
    prv = pltpu.roll(x, 1, x.ndim - 1)
    return jnp.where(lane % 2 == 0, nxt, prv)


def _rope(x, cos, sin):
    reps = x.shape[1] // LANES
    cos = jnp.concatenate([cos] * reps, axis=1) if reps > 1 else cos
    sin = jnp.concatenate([sin] * reps, axis=1) if reps > 1 else sin
    return x * cos + _pair_swap(x) * sin


def _cd_in_kernel(h_ref, mod_ref, w_ref, gm_ref, qg_ref, kg_ref, cos_ref, sin_ref,
                  cq_ref, ck_ref, cv_ref, dq_ref, dk_ref, dv_ref, *, rope):
    n = _modulate(h_ref[...], mod_ref[0:1, :], mod_ref[1:2, :])
    z = _dot(n.astype(BF16), w_ref[...])
    qw, kw, dqw, dw = cq_ref.shape[1], ck_ref.shape[1], dq_ref.shape[1], dk_ref.shape[1]
    o1, o2, o3, o4, o5 = qw, qw + kw, qw + 2 * kw, qw + 2 * kw + dqw, qw + 2 * kw + dqw + dw
    scale = HEAD_DIM ** -0.5

    def head_norm(t, gain):
        ms = _dot((t * t).astype(BF16), gm_ref[:t.shape[1], :t.shape[1]])
        return t * lax.rsqrt(ms + EPS) * gain

    cq = head_norm(z[:, :o1], qg_ref[...])
    ck = head_norm(z[:, o1:o2], kg_ref[...])
    if rope:
        cq = _rope(cq, cos_ref[...], sin_ref[...])
        ck = _rope(ck, cos_ref[...], sin_ref[...])
    scale = scale * LOG2E
    cq_ref[...] = (cq * scale).astype(BF16)
    ck_ref[...] = ck.astype(BF16)
    cv_ref[...] = z[:, o2:o3].astype(BF16)
    dq_ref[...] = (z[:, o3:o4] * scale).astype(BF16)
    dk_ref[...] = z[:, o4:o5].astype(BF16)
    dv_ref[...] = z[:, o5:].astype(BF16)


def _project_cd(h, mod, p, rope):
    bsz, length, d = h.shape
    qw, kw, dw = C_HEADS * HEAD_DIM, C_KV_HEADS * HEAD_DIM, D_HEADS * HEAD_DIM
    tm = _row_tile(length, 512)
    const = lambda i, j: (0, 0)
    tok = lambda w: pl.BlockSpec((None, tm, w), lambda i, j: (i, j, 0))
    widths = (qw, kw, kw, dw, dw, dw)
    return pl.pallas_call(
        functools.partial(_cd_in_kernel, rope=rope),
        grid=(bsz, length // tm),
        in_specs=[
            tok(d),
            pl.BlockSpec((None, 8, d), lambda i, j: (i, 0, 0)),
            pl.BlockSpec(p["w_in"].shape, const),
            pl.BlockSpec(p["gmean"].shape, const),
            pl.BlockSpec(p["q_gain"].shape, const),
            pl.BlockSpec(p["k_gain"].shape, const),
            pl.BlockSpec((tm, LANES), lambda i, j: (j, 0)),
            pl.BlockSpec((tm, LANES), lambda i, j: (j, 0)),
        ],
        out_specs=[tok(w) for w in widths],
        out_shape=[jax.ShapeDtypeStruct((bsz, length, w), BF16) for w in widths],
        compiler_params=_cparams("parallel", "parallel"),
    )(h, mod, p["w_in"], p["gmean"], p["q_gain"], p["k_gain"], p["cos"], p["sin"])


def _gqa_kernel(q_ref, kt_ref, v_ref, o_ref):
    tq = q_ref.shape[0]
    sub = tq // GQA_SPLIT
    lane = lax.broadcasted_iota(jnp.int32, (sub, LANES), 1)

    def attend(h, rows):
        block = q_ref[rows, (h % C_GROUP) * LANES:(h % C_GROUP + 1) * LANES]
        own = (lane >= HEAD_DIM) if h // C_GROUP else (lane < HEAD_DIM)
        s = _dot(jnp.where(own, block, jnp.zeros_like(block)), kt_ref[...])
        p = jnp.exp2(s - jnp.max(s, axis=-1, keepdims=True)).astype(BF16)
        o = _dot(p, v_ref[...])
        return o[:, :LANES] / o[:, LANES:LANES + 1]

    for pair in range(C_HEADS // 2):
        for part in range(GQA_SPLIT):
            rows = slice(part * sub, (part + 1) * sub)
            a, b = attend(2 * pair, rows), attend(2 * pair + 1, rows)
            if (2 * pair) // C_GROUP == 0:
                blk = jnp.where(lane < HEAD_DIM, a, pltpu.roll(b, HEAD_DIM, 1))
            else:
                blk = jnp.where(lane < HEAD_DIM, pltpu.roll(a, HEAD_DIM, 1), b)
            o_ref[rows, pair * LANES:(pair + 1) * LANES] = blk.astype(BF16)


def _gqa(q_paired, k_all, v_all):
    bsz, length, qw = q_paired.shape
    lk = k_all.shape[1]
    assert C_KV_HEADS * HEAD_DIM == LANES and C_KV_HEADS == 2
    kt = jnp.swapaxes(k_all, 1, 2)
    v_ext = jnp.concatenate([v_all, jnp.ones_like(v_all)], axis=2)
    tq = _row_tile(length, 512)
    ow = C_HEADS * HEAD_DIM
    return pl.pallas_call(
        _gqa_kernel,
        grid=(bsz, length // tq),
        in_specs=[
            pl.BlockSpec((None, tq, qw), lambda i, j: (i, j, 0)),
            pl.BlockSpec((None, LANES, lk), lambda i, j: (i, 0, 0)),
            pl.BlockSpec((None, lk, 2 * LANES), lambda i, j: (i, 0, 0)),
        ],
        out_specs=pl.BlockSpec((None, tq, ow), lambda i, j: (i, j, 0)),
        out_shape=jax.ShapeDtypeStruct((bsz, length, ow), BF16),
        compiler_params=_cparams("parallel", "parallel"),
    )(q_paired, kt, v_ext)


def _na_block_start(qb, rows):
    return jnp.clip(qb * NA_QROWS - NA_WIN_H // 2, 0, rows - NA_KROWS)


def _na_bias_tables(rel_bias, rows):
    nblk = rows // NA_QROWS
    n_dr, n_dc = 2 * NA_WIN_H - 1, 2 * NA_WIN_W - 1
    i = np.arange(NA_QROWS)[:, None]
    a = np.arange(NA_KROWS)[None, :]
    j = np.arange(GRID_W)[:, None]
    kc = np.arange(GRID_W)[None, :]
    col_start = np.clip(j - NA_WIN_W // 2, 0, GRID_W - NA_WIN_W)
    valid_col = (kc >= col_start) & (kc < col_start + NA_WIN_W)
    dc = np.clip(kc - j + (NA_WIN_W - 1), 0, n_dc - 1)
    onehot_c = (dc[:, :, None] == np.arange(n_dc)).astype(np.float32)
    onehot_r, valid = [], []
    for qb in (0, 1, nblk - 1):
        r = qb * NA_QROWS + i
        r0 = np.clip(r - NA_WIN_H // 2, 0, rows - NA_WIN_H)
        kr = int(np.clip(qb * NA_QROWS - NA_WIN_H // 2, 0, rows - NA_KROWS)) + a
        valid_row = (kr >= r0) & (kr < r0 + NA_WIN_H)
        dr = np.clip(kr - r + (NA_WIN_H - 1), 0, n_dr - 1)
        onehot_r.append((dr[:, :, None] == np.arange(n_dr)).astype(np.float32))
        valid.append(valid_row[:, None, :, None] & valid_col[None, :, None, :])
    onehot_r = jnp.asarray(np.stack(onehot_r))
    valid = np.stack(valid)
    hp = lax.Precision.HIGHEST
    by_row = jnp.einsum("hrc,ziar->zhiac", rel_bias.astype(F32), onehot_r, precision=hp)
    table = jnp.einsum("zhiac,jkc->zhijak", by_row, jnp.asarray(onehot_c), precision=hp)
    table = jnp.where(valid[:, None], table * LOG2E, MASK_VALUE)
    return table.reshape(3 * rel_bias.shape[0], NA_QROWS * GRID_W, NA_KROWS * GRID_W)


def _na_kernel(q_ref, k_ref, v_ref, kc_ref, vc_ref, bias_ref, o_ref, *, rows):
    qb = pl.program_id(1)
    nk = NA_KROWS * GRID_W
    start = pl.multiple_of(_na_block_start(qb, rows) * GRID_W, GRID_W)
    k_nb = k_ref[pl.ds(start, nk), :]
    v_nb = v_ref[pl.ds(start, nk), :]
    nc = kc_ref.shape[0]
    sub = q_ref.shape[0] // NA_SPLIT
    lane = lax.broadcasted_iota(jnp.int32, (sub, LANES), 1)

    def attend(h, rows, keys, values):
        block = q_ref[rows, (h // 2) * LANES:(h // 2 + 1) * LANES]
        own = (lane >= HEAD_DIM) if h % 2 else (lane < HEAD_DIM)
        q = jnp.where(own, block, jnp.zeros_like(block))
        s = jnp.concatenate([_dot_nt(q, keys[0]) + bias_ref[h, rows, :], _dot_nt(q, keys[1])], axis=1)
        p = jnp.exp2(s - jnp.max(s, axis=-1, keepdims=True)).astype(BF16)
        o = _dot(p[:, :nk], values[0]) + _dot(p[:, nk:], values[1])
        return o[:, :LANES] / o[:, LANES:LANES + 1]

    for pair in range(D_HEADS // 2):
        cols = slice(pair * LANES, (pair + 1) * LANES)
        keys = (k_nb[:, cols], kc_ref[:, cols])
        values = (jnp.concatenate([v_nb[:, cols], jnp.ones((nk, LANES), BF16)], axis=1),
                  jnp.concatenate([vc_ref[:, cols], jnp.ones((nc, LANES), BF16)], axis=1))
        for part in range(NA_SPLIT):
            rows = slice(part * sub, (part + 1) * sub)
            even, odd = attend(2 * pair, rows, keys, values), attend(2 * pair + 1, rows, keys, values)
            o_ref[rows, cols] = jnp.where(lane < HEAD_DIM, even, odd).astype(BF16)


def _neighbourhood(dq, dk, dv, dk_c, dv_c, bias):
    bsz, length, w = dk.shape
    rows = length // GRID_W
    assert rows % NA_QROWS == 0 and rows >= NA_KROWS
    nblk = rows // NA_QROWS
    tq = NA_QROWS * GRID_W
    cl = dk_c.shape[1]

    def bias_class(i, j):
        cls = jnp.where(j == 0, 0, jnp.where(j == nblk - 1, 2, 1))
        return (cls, 0, 0)

    return pl.pallas_call(
        functools.partial(_na_kernel, rows=rows),
        grid=(bsz, nblk),
        in_specs=[
            pl.BlockSpec((None, tq, dq.shape[2]), lambda i, j: (i, j, 0)),
            pl.BlockSpec((None, length, w), lambda i, j: (i, 0, 0)),
            pl.BlockSpec((None, length, w), lambda i, j: (i, 0, 0)),
            pl.BlockSpec((None, cl, w), lambda i, j: (i, 0, 0)),
            pl.BlockSpec((None, cl, w), lambda i, j: (i, 0, 0)),
            pl.BlockSpec((D_HEADS, tq, NA_KROWS * GRID_W), bias_class),
        ],
        out_specs=pl.BlockSpec((None, tq, w), lambda i, j: (i, j, 0)),
        out_shape=jax.ShapeDtypeStruct((bsz, length, w), BF16),
        compiler_params=_cparams("parallel", "arbitrary"),
    )(dq, dk, dv, dk_c, dv_c, bias)


def _router_kernel(h_ref, mod_ref, wr_ref, n_ref, sel_ref, cnt_ref):
    n = _modulate(h_ref[...], mod_ref[3:4, :], mod_ref[4:5, :])
    n_hi = n.astype(BF16)
    n_lo = (n - n_hi.astype(F32)).astype(BF16)
    w = wr_ref[...]
    w_hi = w.astype(BF16)
    w_lo = (w - w_hi.astype(F32)).astype(BF16)
    logits = _dot(n_hi, w_hi) + (_dot(n_lo, w_hi) + _dot(n_hi, w_lo))
    lane = lax.broadcasted_iota(jnp.int32, logits.shape, 1)
    logits = jnp.where(lane < N_EXPERTS, logits, -jnp.inf)
    m1 = jnp.max(logits, axis=-1, keepdims=True)
    i1 = jnp.min(jnp.where(logits == m1, lane, LANES), axis=-1, keepdims=True)
    rest = jnp.where(lane == i1, -jnp.inf, logits)
    m2 = jnp.max(rest, axis=-1, keepdims=True)
    i2 = jnp.min(jnp.where(rest == m2, lane, LANES), axis=-1, keepdims=True)
    e2 = jnp.exp(m2 - m1)
    w1 = 1.0 / (1.0 + e2)
    w2 = e2 / (1.0 + e2)

    ntok = logits.shape[0]
    chosen = jnp.where(lane == i1, 1.0, jnp.where(lane == i2, 1.0, 0.0))
    earlier = (lax.broadcasted_iota(jnp.int32, (ntok, ntok), 1)
               < lax.broadcasted_iota(jnp.int32, (ntok, ntok), 0))
    before = _dot(jnp.where(earlier, 1.0, 0.0).astype(BF16), chosen.astype(BF16))
    count = jnp.sum(chosen, axis=0, keepdims=True).astype(jnp.int32)
    npiece = lax.shift_right_logical(count + (MOE_PIECE - 1), jnp.full_like(count, MOE_PIECE.bit_length() - 1))
    lower = (lax.broadcasted_iota(jnp.int32, (LANES, LANES), 0)
             < lax.broadcasted_iota(jnp.int32, (LANES, LANES), 1))
    npiece_rows = jnp.broadcast_to(npiece.astype(F32), (8, LANES)).astype(BF16)
    piece_off = _dot(npiece_rows, jnp.where(lower, 1.0, 0.0).astype(BF16))[0:1, :]
    row = piece_off * float(MOE_PIECE) + before
    pos1 = jnp.sum(jnp.where(lane == i1, row, 0.0), axis=-1, keepdims=True)
    pos2 = jnp.sum(jnp.where(lane == i2, row, 0.0), axis=-1, keepdims=True)

    sel = jnp.where(lane == 0, i1.astype(F32), jnp.where(lane == 1, i2.astype(F32),
                    jnp.where(lane == 2, w1, jnp.where(lane == 3, w2,
                              jnp.where(lane == 4, pos1, jnp.where(lane == 5, pos2, 0.0))))))
    sel_ref[...] = sel
    cnt_ref[...] = jnp.broadcast_to(npiece.astype(F32), cnt_ref.shape)
    n_ref[...] = n_hi


def _router(h, mod, w_router_padded):
    bsz, length, d = h.shape
    tm = _row_tile(length, MOE_TOKENS)
    tiles = length // tm
    tok = lambda w: pl.BlockSpec((None, tm, w), lambda i, j: (i, j, 0))
    return pl.pallas_call(
        _router_kernel,
        grid=(bsz, tiles),
        in_specs=[
            tok(d),
            pl.BlockSpec((None, 8, d), lambda i, j: (i, 0, 0)),
            pl.BlockSpec((d, LANES), lambda i, j: (0, 0)),
        ],
        out_specs=[tok(d), tok(LANES), pl.BlockSpec((None, None, 8, LANES), lambda i, j: (i, j, 0, 0))],
        out_shape=[
            jax.ShapeDtypeStruct((bsz, length, d), BF16),
            jax.ShapeDtypeStruct((bsz, length, LANES), F32),
            jax.ShapeDtypeStruct((bsz, tiles, 8, LANES), F32),
        ],
        compiler_params=_cparams("parallel", "parallel"),
    )(h, mod, w_router_padded)


def _moe_dispatch_kernel(pos_ref, w_ref, n_ref, xs_ref):
    nrow, d = xs_ref.shape[0], n_ref.shape[1]
    base = pl.program_id(1) * nrow
    row = base + lax.broadcasted_iota(jnp.int32, (nrow, 1), 0)
    eq0 = pos_ref[0:1, :] == row
    eq1 = pos_ref[1:2, :] == row
    perm = jnp.where(eq0, 1.0, jnp.where(eq1, 1.0, 0.0)).astype(BF16)
    xs_ref[:, :d] = _dot(perm, n_ref[...]).astype(BF16)
    gate = jnp.sum(jnp.where(eq0, w_ref[0:1, :], jnp.where(eq1, w_ref[1:2, :], 0.0)), axis=-1, keepdims=True)
    hi = gate.astype(BF16).astype(F32)
    mid = (gate - hi).astype(BF16).astype(F32)
    lo = (gate - hi) - mid
    lane = lax.broadcasted_iota(jnp.int32, (nrow, LANES), 1)
    parts = jnp.where(lane == 0, hi, jnp.where(lane == 1, mid, jnp.where(lane == 2, lo, 0.0)))
    xs_ref[:, d:] = parts.astype(BF16)


def _moe_expert_kernel(piece_ref, exp_ref, nused_ref, *refs):
    xs_refs = refs[:MOE_GROUP]
    wg_ref, wu_ref, wd_ref, ys_ref = refs[MOE_GROUP:]
    i = pl.program_id(0)

    @pl.when(i < nused_ref[0])
    def _():
        d = wg_ref.shape[0]
        xe = jnp.concatenate([r[...] for r in xs_refs], axis=0)
        gate = jnp.sum(xe[:, d:].astype(F32), axis=-1, keepdims=True)
        ys_ref[...] = (_swiglu(xe[:, :d], wg_ref, wu_ref, wd_ref, MOE_FF_CHUNK) * gate).astype(BF16)

    @pl.when(i >= nused_ref[0])
    def _():
        ys_ref[...] = jnp.zeros_like(ys_ref)


def _moe_combine_kernel(slot_ref, pos_ref, h_ref, mod_ref, fg_ref, *refs):
    ys = jnp.concatenate([r[...] for r in refs[:-1]], axis=0)
    o_ref = refs[-1]
    nrow = ys.shape[0]
    sub = pos_ref.shape[0] // MOE_COMBINE_SPLIT
    row = lax.broadcasted_iota(jnp.int32, (sub, nrow), 1)
    for part in range(MOE_COMBINE_SPLIT):
        toks = slice(part * sub, (part + 1) * sub)
        perm = jnp.where(pos_ref[toks, 0:1] == row, 1.0,
                         jnp.where(pos_ref[toks, 1:2] == row, 1.0, 0.0)).astype(BF16)
        y = h_ref[toks, :] + mod_ref[5:6, :] * _dot(perm, ys)
        ms = jnp.mean(y * y, axis=-1, keepdims=True)
        o_ref[toks, :] = y * lax.rsqrt(ms + EPS) * fg_ref[...]


def _moe_plan(sel, npiece, n_tiles, tile_tokens, rows_per_tile):
    pieces_per_tile = rows_per_tile // MOE_PIECE
    weights = sel[:, TOP_K:2 * TOP_K].reshape(n_tiles, tile_tokens, TOP_K)
    pos = sel[:, 2 * TOP_K:3 * TOP_K].astype(jnp.int32).reshape(n_tiles, tile_tokens, TOP_K)
    piece_end = jnp.cumsum(npiece.astype(jnp.int32), axis=1)

    b = jnp.arange(pieces_per_tile, dtype=jnp.int32)
    piece_expert = jnp.sum((piece_end[:, None, :] <= b[None, :, None]).astype(jnp.int32), axis=-1)
    flat_expert = piece_expert.reshape(-1)
    n_pieces = flat_expert.shape[0]
    n_slots = n_pieces + N_EXPERTS * MOE_GROUP
    classes = jnp.arange(N_EXPERTS + 1, dtype=jnp.int32)
    cls_onehot = (flat_expert[:, None] == classes).astype(jnp.int32)
    cls_csum = jnp.cumsum(cls_onehot, axis=0)
    piece_rank = jnp.sum((cls_csum - cls_onehot) * cls_onehot, axis=-1)
    n_cls = cls_csum[-1]
    groups = (n_cls[:N_EXPERTS] + MOE_GROUP - 1) // MOE_GROUP
    start = (jnp.cumsum(groups) - groups) * MOE_GROUP
    n_used_groups = jnp.sum(groups)

    slot = jnp.arange(n_slots, dtype=jnp.int32)
    slot_expert = jnp.sum((start[None, :] <= slot[:, None]).astype(jnp.int32), axis=-1) - 1
    slot_taken = (slot - start[slot_expert]) < n_cls[slot_expert]
    free_slots = jnp.argsort(slot_taken.astype(jnp.int32) * n_slots + slot).astype(jnp.int32)
    is_used = flat_expert < N_EXPERTS
    safe_expert = jnp.minimum(flat_expert, N_EXPERTS - 1)
    piece_slot = jnp.where(is_used, start[safe_expert] + piece_rank, free_slots[piece_rank])
    first_unused = jnp.argmax(jnp.logical_not(is_used)).astype(jnp.int32)
    piece_ids = jnp.arange(n_pieces, dtype=jnp.int32)
    slot_piece = jnp.full((n_slots,), first_unused, jnp.int32).at[piece_slot].set(piece_ids, unique_indices=True)
    step_expert = slot_expert[::MOE_GROUP]
    return pos, weights, slot_piece, piece_slot.astype(jnp.int32), step_expert, n_used_groups.reshape(1)


def _moe_final(n2, sel, npiece, h, mod, final_gain, wg, wu, wd):
    bsz, length, d = h.shape
    n_exp, _, ff = wg.shape
    tokens = bsz * length
    tt = _row_tile(length, MOE_TOKENS)
    n_tiles = tokens // tt
    rows = TOP_K * tt + N_EXPERTS * MOE_PIECE
    drows = _divisor_tile(rows, MOE_DISPATCH_ROWS, MOE_PIECE)
    ctok = _row_tile(tt, MOE_COMBINE_TOKENS)
    pos, weights, slot_piece, piece_slot, step_expert, n_used = _moe_plan(
        sel.reshape(tokens, LANES), npiece.reshape(n_tiles, 8, LANES)[:, 0, :N_EXPERTS], n_tiles, tt, rows)

    pos_rows = jnp.swapaxes(pos, 1, 2)
    w_rows = jnp.swapaxes(weights, 1, 2)
    xw = d + LANES
    xs = pl.pallas_call(
        _moe_dispatch_kernel,
        grid=(n_tiles, rows // drows),
        in_specs=[
            pl.BlockSpec((None, TOP_K, tt), lambda t, r: (t, 0, 0)),
            pl.BlockSpec((None, TOP_K, tt), lambda t, r: (t, 0, 0)),
            pl.BlockSpec((tt, d), lambda t, r: (t, 0)),
        ],
        out_specs=pl.BlockSpec((None, drows, xw), lambda t, r: (t, r, 0)),
        out_shape=jax.ShapeDtypeStruct((n_tiles, rows, xw), BF16),
        compiler_params=_cparams("parallel", "parallel"),
    )(pos_rows, w_rows, n2.reshape(tokens, d))

    n_steps = slot_piece.shape[0] // MOE_GROUP
    resident = pl.Buffered(1)

    def piece_spec(j):
        return pl.BlockSpec((MOE_PIECE, xw), lambda i, sp, ex, nu: (sp[i * MOE_GROUP + j], 0))

    def weight_spec(shape):
        return pl.BlockSpec((None,) + shape, lambda i, sp, ex, nu: (ex[i], 0, 0), pipeline_mode=resident)

    xs_flat = xs.reshape(n_tiles * rows, xw)
    step_rows = MOE_GROUP * MOE_PIECE
    ys = pl.pallas_call(
        _moe_expert_kernel,
        grid_spec=pltpu.PrefetchScalarGridSpec(
            num_scalar_prefetch=3,
            grid=(n_steps,),
            in_specs=([piece_spec(j) for j in range(MOE_GROUP)]
                      + [weight_spec((d, ff)), weight_spec((d, ff)), weight_spec((ff, d))]),
            out_specs=pl.BlockSpec((step_rows, d), lambda i, sp, ex, nu: (i, 0)),
        ),
        out_shape=jax.ShapeDtypeStruct((n_steps * step_rows, d), BF16),
        compiler_params=_cparams("arbitrary"),
    )(slot_piece, step_expert, n_used, *([xs_flat] * MOE_GROUP), wg, wu, wd)

    tiles_per_seq = length // tt
    pieces_per_tile = rows // MOE_PIECE

    def tile_piece_spec(k):
        return pl.BlockSpec((MOE_PIECE, d), lambda t, c, slot: (slot[t * pieces_per_tile + k], 0))

    out = pl.pallas_call(
        _moe_combine_kernel,
        grid_spec=pltpu.PrefetchScalarGridSpec(
            num_scalar_prefetch=1,
            grid=(n_tiles, tt // ctok),
            in_specs=[
                pl.BlockSpec((None, ctok, TOP_K), lambda t, c, slot: (t, c, 0)),
                pl.BlockSpec((None, ctok, d), lambda t, c, slot: (t, c, 0)),
                pl.BlockSpec((None, 8, d), lambda t, c, slot: (t // tiles_per_seq, 0, 0)),
                pl.BlockSpec((1, d), lambda t, c, slot: (0, 0)),
            ] + [tile_piece_spec(k) for k in range(pieces_per_tile)],
            out_specs=pl.BlockSpec((None, ctok, d), lambda t, c, slot: (t, c, 0)),
        ),
        out_shape=jax.ShapeDtypeStruct((n_tiles, tt, d), F32),
        compiler_params=_cparams("parallel", "parallel"),
    )(piece_slot, pos, h.reshape(n_tiles, tt, d), mod, final_gain, *([ys] * pieces_per_tile))
    return out.reshape(bsz, length, d)


def _group_mean_matrix(width, group):
    return jnp.asarray(np.kron(np.eye(width // group), np.full((group, group), 1.0 / group)), BF16)


def _rope_tables(length):
    t = jnp.arange(length, dtype=jnp.int32)
    row = (t // GRID_W).astype(F32)
    col = (t % GRID_W).astype(F32)
    n_axis = HEAD_DIM // 4
    inv_freq = ROPE_THETA ** (-jnp.arange(n_axis, dtype=F32) / n_axis)
    ang = jnp.concatenate([row[:, None] * inv_freq, col[:, None] * inv_freq], axis=-1)
    cos = jnp.repeat(jnp.cos(ang), 2, axis=-1)
    sin = jnp.repeat(jnp.sin(ang), 2, axis=-1)
    sign = jnp.tile(jnp.asarray([-1.0, 1.0], F32), HEAD_DIM // 2)
    reps = LANES // HEAD_DIM
    return jnp.tile(cos, (1, reps)), jnp.tile(sin * sign, (1, reps))


def kernel(x, c, ctx, c_ctx, w_ada, b_ada, w_in_ab, v_gain, w_spatial, b_spatial, w_out_ab,
           w_gate_dense, w_up_dense, w_down_dense, w_in_cd, q_gain, k_gain, rel_bias, w_out_cd,
           w_router, w_gate_moe, w_up_moe, w_down_moe, final_gain):
    bsz, length, d = x.shape
    depth = w_ada.shape[0]
    assert depth == 2 and bsz <= 8
    aw = v_gain.shape[1]
    group_dim = aw // A_GROUPS
    bgd = (w_in_ab.shape[2] - 2 * aw) // B_GROUPS

    cond = jnp.zeros((16, d), F32).at[:bsz].set(c).at[8].set(c_ctx)
    ada = _adaln(cond, w_ada, b_ada).reshape(depth, 16, 6, d)
    pad = jnp.zeros((depth, 16, 2, d), F32)
    ada = jnp.concatenate([ada, pad], axis=2)
    mod_lat = [ada[i, :bsz] for i in range(depth)]
    mod_ctx = [jnp.broadcast_to(ada[i, 8], (bsz, 8, d)) for i in range(depth)]

    cmat, smat = _dft_tables(bgd, bgd ** -0.5)
    p_ab = dict(
        w_in=w_in_ab[0].astype(BF16),
        v_gain=v_gain[0].reshape(1, aw),
        gmean=_group_mean_matrix(aw, group_dim),
        w_sp=w_spatial[0].transpose(1, 0, 2).reshape(CHUNK, A_GROUPS * CHUNK).astype(BF16),
        b_sp=jnp.repeat(b_spatial[0].T, group_dim, axis=1),
        cmat=cmat, smat=smat,
        w_out=w_out_ab[0].astype(BF16),
    )
    wg0, wu0, wd0 = (w.astype(BF16) for w in (w_gate_dense[0], w_up_dense[0], w_down_dense[0]))
    h = _proj_residual(*_mix_ab(x, mod_lat[0], p_ab), p_ab["w_out"], x, mod_lat[0])
    hc = _proj_residual(*_mix_ab(ctx, mod_ctx[0], p_ab), p_ab["w_out"], ctx, mod_ctx[0])
    h = _dense_ffn(h, mod_lat[0], wg0, wu0, wd0)
    hc = _dense_ffn(hc, mod_ctx[0], wg0, wu0, wd0)

    cos, sin = _rope_tables(length)
    qw = C_HEADS * HEAD_DIM
    kw = C_KV_HEADS * HEAD_DIM
    w_cd = w_in_cd[0]
    w_q = w_cd[:, :qw].reshape(d, C_KV_HEADS, C_GROUP, HEAD_DIM).transpose(0, 2, 1, 3).reshape(d, qw)
    p_cd = dict(
        w_in=jnp.concatenate([w_q, w_cd[:, qw:]], axis=1).astype(BF16),
        gmean=_group_mean_matrix(qw, HEAD_DIM),
        q_gain=jnp.tile(q_gain[0], C_HEADS).reshape(1, qw),
        k_gain=jnp.tile(k_gain[0], C_KV_HEADS).reshape(1, kw),
        cos=cos, sin=sin,
    )
    cq, ck, cv, dq, dk, dv = _project_cd(h, mod_lat[1], p_cd, rope=True)
    p_cd_ctx = dict(p_cd, cos=cos[:hc.shape[1]], sin=sin[:hc.shape[1]])
    _, ck_c, cv_c, _, dk_c, dv_c = _project_cd(hc, mod_ctx[1], p_cd_ctx, rope=False)
    o_c = _gqa(cq, jnp.concatenate([ck_c, ck], axis=1), jnp.concatenate([cv_c, cv], axis=1))
    o_d = _neighbourhood(dq, dk, dv, dk_c, dv_c, _na_bias_tables(rel_bias[0], length // GRID_W))
    wr = jnp.zeros((d, LANES), F32).at[:, :N_EXPERTS].set(w_router[0])
    h = _proj_residual(o_c, o_d, w_out_cd[0].astype(BF16), h, mod_lat[1])
    n2, sel, npiece = _router(h, mod_lat[1], wr)
    return _moe_final(n2, sel, npiece, h, mod_lat[1], final_gain.reshape(1, d),
                      w_gate_moe[0].astype(BF16), w_up_moe[0].astype(BF16), w_down_moe[0].astype(BF16))
```
